```python
import jax, jax.numpy as jnp
from jax import lax
import numpy as np

D_MODEL = 1024
BATCH = 2
SEQ = 8192
DEPTH = 1

GRID_W = 64
CTX_LEN = 256
ATT_HEADS = 8
ATT_KV_HEADS = 2
ATT_HEAD_DIM = 64
ATT_GROUP = ATT_HEADS // ATT_KV_HEADS
ATT_WIDTH = ATT_HEADS * ATT_HEAD_DIM
ATT_KV_WIDTH = ATT_KV_HEADS * ATT_HEAD_DIM
RET_HEADS = 4
RET_HEAD_DIM = 128
RET_WIDTH = RET_HEADS * RET_HEAD_DIM
MIX_WIDTH = ATT_WIDTH + RET_WIDTH
IN_SIZES = (ATT_WIDTH, ATT_KV_WIDTH, ATT_KV_WIDTH, ATT_WIDTH, RET_WIDTH, RET_WIDTH, RET_WIDTH, RET_WIDTH)
IN_WIDTH = sum(IN_SIZES)
Q_BLOCK = 128
RET_CHUNK = 128
ROPE_THETA = 10000.0
NORM_EPS = 1e-6

kernel_name = "hymba_style_gqa_retention_prefix_ctx"


def rms_norm(x, w):
    xf = x.astype(jnp.float32)
    y = xf * lax.rsqrt(jnp.mean(xf * xf, axis=-1, keepdims=True) + NORM_EPS)
    return (y * w.astype(jnp.float32)).astype(x.dtype)


def split_proj(p):
    offsets = [int(o) for o in np.cumsum(IN_SIZES)[:-1]]
    return jnp.split(p, offsets, axis=-1)


def axial_rope_tables(n_rows, head_dim):
    rows, cols = jnp.meshgrid(jnp.arange(n_rows, dtype=jnp.float32),
                              jnp.arange(GRID_W, dtype=jnp.float32), indexing="ij")
    rows = rows.reshape(-1)
    cols = cols.reshape(-1)
    n_axis = head_dim // 4
    inv_freq = ROPE_THETA ** (-jnp.arange(n_axis, dtype=jnp.float32) / n_axis)
    ang = jnp.concatenate([rows[:, None] * inv_freq, cols[:, None] * inv_freq], axis=-1)
    return jnp.cos(ang), jnp.sin(ang)


def apply_rope(x, cos, sin):
    half = x.shape[-1] // 2
    xf = x.astype(jnp.float32)
    x1, x2 = xf[..., :half], xf[..., half:]
    c = cos[None, :, None, :]
    s = sin[None, :, None, :]
    return jnp.concatenate([x1 * c - x2 * s, x1 * s + x2 * c], axis=-1).astype(x.dtype)


def attention_heads(qa, ka, va, q_norm_w, k_norm_w):
    b, t, _ = qa.shape
    q = rms_norm(qa.reshape(b, t, ATT_HEADS, ATT_HEAD_DIM), q_norm_w)
    k = rms_norm(ka.reshape(b, t, ATT_KV_HEADS, ATT_HEAD_DIM), k_norm_w)
    v = va.reshape(b, t, ATT_KV_HEADS, ATT_HEAD_DIM)
    return q, k, v


def latent_attention(q_lat, k_lat, v_lat, k_ctx, v_ctx):
    b, t, _, _ = q_lat.shape
    scale = ATT_HEAD_DIM ** -0.5
    k_all = jnp.concatenate([k_ctx, k_lat], axis=1)
    v_all = jnp.concatenate([v_ctx, v_lat], axis=1)
    nb = t // Q_BLOCK
    qb = q_lat.reshape(b, nb, Q_BLOCK, ATT_KV_HEADS, ATT_GROUP, ATT_HEAD_DIM)
    qb = qb.transpose(1, 0, 3, 4, 2, 5)

    def one_block(q_blk):
        s = jnp.einsum("bkgqd,bskd->bkgqs", q_blk, k_all,
                       preferred_element_type=jnp.float32) * scale
        p = jax.nn.softmax(s, axis=-1).astype(v_all.dtype)
        return jnp.einsum("bkgqs,bskd->bkgqd", p, v_all)

    o = lax.map(one_block, qb)
    return o.transpose(1, 0, 4, 2, 3, 5).reshape(b, t, ATT_WIDTH)


def context_attention(q, k, v):
    b, l, _, _ = q.shape
    qg = q.reshape(b, l, ATT_KV_HEADS, ATT_GROUP, ATT_HEAD_DIM)
    s = jnp.einsum("bqkgd,bskd->bkgqs", qg, k,
                   preferred_element_type=jnp.float32) * (ATT_HEAD_DIM ** -0.5)
    p = jax.nn.softmax(s, axis=-1).astype(v.dtype)
    o = jnp.einsum("bkgqs,bskd->bqkgd", p, v)
    return o.reshape(b, l, ATT_WIDTH)


def retention_heads(qr, kr, vr, cos=None, sin=None):
    b, t, _ = qr.shape
    q = qr.reshape(b, t, RET_HEADS, RET_HEAD_DIM)
    k = kr.reshape(b, t, RET_HEADS, RET_HEAD_DIM)
    v = vr.reshape(b, t, RET_HEADS, RET_HEAD_DIM)
    if cos is not None:
        q = apply_rope(q, cos, sin)
        k = apply_rope(k, cos, sin)
    k = k * (RET_HEAD_DIM ** -0.5)
    to_bhtd = lambda a: a.transpose(0, 2, 1, 3).astype(jnp.float32)
    return to_bhtd(q), to_bhtd(k), to_bhtd(v)


def retention_chunked(q, k, v, log_gamma, state0):
    b, h, t, dk = q.shape
    dv = v.shape[-1]
    n = t // RET_CHUNK
    qc = q.reshape(b, h, n, RET_CHUNK, dk)
    kc = k.reshape(b, h, n, RET_CHUNK, dk)
    vc = v.reshape(b, h, n, RET_CHUNK, dv)
    idx = jnp.arange(RET_CHUNK, dtype=jnp.float32)
    diff = idx[:, None] - idx[None, :]
    lg = log_gamma[:, None, None]
    decay_in = jnp.where(diff >= 0, jnp.exp(lg * jnp.maximum(diff, 0.0)), 0.0)
    q_decay = jnp.exp(log_gamma[:, None] * (idx + 1.0))
    k_decay = jnp.exp(log_gamma[:, None] * (RET_CHUNK - 1.0 - idx))
    chunk_decay = jnp.exp(log_gamma * RET_CHUNK)

    scores = jnp.einsum("bhnid,bhnjd->bhnij", qc, kc) * decay_in[None, :, None]
    o_inner = jnp.einsum("bhnij,bhnjv->bhniv", scores, vc)
    u = jnp.einsum("bhnjd,bhnjv->nbhdv", kc * k_decay[None, :, None, :, None], vc)

    def step(s, u_n):
        s_new = chunk_decay[None, :, None, None] * s + u_n
        return s_new, s

    s_final, s_prev = lax.scan(step, state0, u)
    o_cross = jnp.einsum("bhnid,nbhdv->bhniv", qc * q_decay[None, :, None, :, None], s_prev)
    return (o_inner + o_cross).reshape(b, h, t, dv), s_final


def retention_bidir(q, k, v, lg_fwd, lg_bwd, s_fwd0, s_bwd0):
    o_f, s_f = retention_chunked(q, k, v, lg_fwd, s_fwd0)
    flip = lambda a: jnp.flip(a, axis=2)
    o_b, s_b = retention_chunked(flip(q), flip(k), flip(v), lg_bwd, s_bwd0)
    return o_f + flip(o_b), s_f, s_b


def head_group_norm(o, w, dtype):
    mu = jnp.mean(o, axis=-1, keepdims=True)
    var = jnp.mean(jnp.square(o - mu), axis=-1, keepdims=True)
    y = (o - mu) * lax.rsqrt(var + NORM_EPS)
    b, h, t, dv = o.shape
    y = y.transpose(0, 2, 1, 3).reshape(b, t, h * dv) * w.astype(jnp.float32)
    return y.astype(dtype)


def setup_inputs(seed: int = 0) -> dict:
    key = jax.random.key(seed)
    ks = jax.random.split(key, 16)
    f32 = jnp.float32
    eps = 2.0 ** (-5.0 - np.arange(RET_HEADS, dtype=np.float32))
    decay_logit = jnp.asarray(np.log((1.0 - eps) / eps), dtype=f32)
    return {
        "x": jax.random.normal(ks[0], (BATCH, SEQ, D_MODEL), f32),
        "c": jax.random.normal(ks[1], (BATCH, D_MODEL), f32),
        "ctx": jax.random.normal(ks[2], (BATCH, CTX_LEN, D_MODEL), f32),
        "c_ctx": jax.random.normal(ks[3], (D_MODEL,), f32),
        "norm_w": 1.0 + 0.02 * jax.random.normal(ks[4], (DEPTH, D_MODEL), f32),
        "w_mod": 0.5 * D_MODEL ** -0.5 * jax.random.normal(ks[5], (DEPTH, D_MODEL, 3 * D_MODEL), f32),
        "b_mod": 0.02 * jax.random.normal(ks[6], (DEPTH, 3 * D_MODEL), f32),
        "w_in": D_MODEL ** -0.5 * jax.random.normal(ks[7], (DEPTH, D_MODEL, IN_WIDTH), f32),
        "q_norm_w": 1.0 + 0.02 * jax.random.normal(ks[8], (DEPTH, ATT_HEAD_DIM), f32),
        "k_norm_w": 1.0 + 0.02 * jax.random.normal(ks[9], (DEPTH, ATT_HEAD_DIM), f32),
        "ret_decay_fwd": decay_logit + 0.05 * jax.random.normal(ks[10], (DEPTH, RET_HEADS), f32),
        "ret_decay_bwd": decay_logit + 0.05 * jax.random.normal(ks[11], (DEPTH, RET_HEADS), f32),
        "ret_gn_w": 1.0 + 0.02 * jax.random.normal(ks[12], (DEPTH, RET_WIDTH), f32),
        "w_out": MIX_WIDTH ** -0.5 * jax.random.normal(ks[13], (DEPTH, MIX_WIDTH, D_MODEL), f32),
        "final_norm_w": 1.0 + 0.02 * jax.random.normal(ks[14], (D_MODEL,), f32),
    }


def reference(x, c, ctx, c_ctx, norm_w, w_mod, b_mod, w_in, q_norm_w, k_norm_w,
              ret_decay_fwd, ret_decay_bwd, ret_gn_w, w_out, final_norm_w):
    b, seq, _ = x.shape
    n_rows = seq // GRID_W
    cos_a, sin_a = axial_rope_tables(n_rows, ATT_HEAD_DIM)
    cos_r, sin_r = axial_rope_tables(n_rows, RET_HEAD_DIM)
    zero_state = jnp.zeros((b, RET_HEADS, RET_HEAD_DIM, RET_HEAD_DIM), jnp.float32)

    for layer in range(DEPTH):
        shift, scale, gate = jnp.split(jax.nn.silu(c) @ w_mod[layer] + b_mod[layer], 3, axis=-1)
        shift_c, scale_c, gate_c = jnp.split(jax.nn.silu(c_ctx) @ w_mod[layer] + b_mod[layer], 3, axis=-1)
        h = rms_norm(x, norm_w[layer]) * (1.0 + scale[:, None, :]) + shift[:, None, :]
        hc = rms_norm(ctx, norm_w[layer]) * (1.0 + scale_c) + shift_c

        qa, ka, va, ga, qr, kr, vr, gr = split_proj(h @ w_in[layer])
        qa_c, ka_c, va_c, ga_c, qr_c, kr_c, vr_c, gr_c = split_proj(hc @ w_in[layer])

        q_l, k_l, v_l = attention_heads(qa, ka, va, q_norm_w[layer], k_norm_w[layer])
        q_l = apply_rope(q_l, cos_a, sin_a)
        k_l = apply_rope(k_l, cos_a, sin_a)
        q_c, k_c, v_c = attention_heads(qa_c, ka_c, va_c, q_norm_w[layer], k_norm_w[layer])
        att_lat = latent_attention(q_l, k_l, v_l, k_c, v_c)

        lg_f = jax.nn.log_sigmoid(ret_decay_fwd[layer].astype(jnp.float32))
        lg_b = jax.nn.log_sigmoid(ret_decay_bwd[layer].astype(jnp.float32))
        rq_c, rk_c, rv_c = retention_heads(qr_c, kr_c, vr_c)
        ret_ctx, s_fwd_ctx, s_bwd_ctx = retention_bidir(rq_c, rk_c, rv_c, lg_f, lg_b, zero_state, zero_state)
        rq_l, rk_l, rv_l = retention_heads(qr, kr, vr, cos_r, sin_r)
        ret_lat, _, _ = retention_bidir(rq_l, rk_l, rv_l, lg_f, lg_b, s_fwd_ctx, s_bwd_ctx)
        ret_lat = head_group_norm(ret_lat, ret_gn_w[layer], x.dtype)

        mixed = jnp.concatenate([att_lat * jax.nn.silu(ga), ret_lat * jax.nn.silu(gr)], axis=-1)
        x = x + gate[:, None, :] * (mixed @ w_out[layer])

        if layer < DEPTH - 1:
            att_ctx = context_attention(q_c, k_c, v_c)
            ret_ctx_n = head_group_norm(ret_ctx, ret_gn_w[layer], ctx.dtype)
            mixed_c = jnp.concatenate([att_ctx * jax.nn.silu(ga_c), ret_ctx_n * jax.nn.silu(gr_c)], axis=-1)
            ctx = ctx + gate_c * (mixed_c @ w_out[layer])

    return rms_norm(x, final_norm_w)
```

```python
import functools

import jax
import jax.numpy as jnp
import numpy as np
from jax import lax
from jax.experimental import pallas as pl
from jax.experimental.pallas import tpu as pltpu

F32 = jnp.float32
BF16 = jnp.bfloat16

D_MODEL = 1024
GRID_W = 64
ATT_HEADS = 8
ATT_KV_HEADS = 2
ATT_GROUP = ATT_HEADS // ATT_KV_HEADS
ATT_HEAD_DIM = 64
ATT_WIDTH = ATT_HEADS * ATT_HEAD_DIM
ATT_KV_WIDTH = ATT_KV_HEADS * ATT_HEAD_DIM
RET_HEADS = 4
RET_HEAD_DIM = 128
RET_WIDTH = RET_HEADS * RET_HEAD_DIM
CHUNK = 128
ROPE_THETA = 10000.0
NORM_EPS = 1e-6
LANES = 128
TILE = 512
CHUNKS_PER_TILE = TILE // CHUNK
MOD_ROWS = 8
VEXT_W = 2 * LANES

O_QA = 0
O_KA = O_QA + ATT_WIDTH
O_VA = O_KA + ATT_KV_WIDTH
O_GA = O_VA + ATT_KV_WIDTH
O_QR = O_GA + ATT_WIDTH
O_KR = O_QR + RET_WIDTH
O_VR = O_KR + RET_WIDTH
O_GR = O_VR + RET_WIDTH
IN_WIDTH = O_GR + RET_WIDTH

VMEM_LIMIT = 48 * 1024 * 1024


def _silu(x):
    return x * (1.0 / (1.0 + jnp.exp(-x)))


def _mod_kernel(cc_ref, w_ref, b_ref, dl_ref, mod_ref, dmask_ref, qdec_ref, kdec_ref, cdec_ref):
    sc = _silu(cc_ref[...])
    mod_ref[0] = jnp.dot(sc, w_ref[...], preferred_element_type=F32,
                         precision=lax.Precision.HIGHEST) + b_ref[0]

    @pl.when(pl.program_id(0) == 0)
    def _():
        x = dl_ref[...]
        lg = jnp.minimum(x, 0.0) - jnp.log1p(jnp.exp(-jnp.abs(x)))
        ii = lax.broadcasted_iota(jnp.int32, (CHUNK, CHUNK), 0).astype(F32)
        jj = lax.broadcasted_iota(jnp.int32, (CHUNK, CHUNK), 1).astype(F32)
        d = ii - jj
        for h in range(RET_HEADS):
            lf = lg[h:h + 1, :]
            lb = lg[RET_HEADS + h:RET_HEADS + h + 1, :]
            fwd = jnp.exp(lf * jnp.maximum(d, 0.0))
            bwd = jnp.exp(lb * jnp.maximum(-d, 0.0))
            dmask_ref[h] = jnp.where(d > 0, fwd, jnp.where(d < 0, bwd, 2.0))
            qdec_ref[h] = jnp.exp(lf * (ii + 1.0))
            qdec_ref[RET_HEADS + h] = jnp.exp(lb * (float(CHUNK) - ii))
        lane = lax.broadcasted_iota(jnp.int32, (2 * RET_HEADS, LANES), 1).astype(F32)
        row = lax.broadcasted_iota(jnp.int32, (2 * RET_HEADS, LANES), 0)
        kdec_ref[...] = jnp.where(row < RET_HEADS, jnp.exp(lg * (float(CHUNK) - 1.0 - lane)),
                                  jnp.exp(lg * lane))
        cdec_ref[...] = jnp.exp(lg * float(CHUNK))


def _run_mod(cc, w_mod, b_mod, dl):
    n3 = 3
    return pl.pallas_call(
        _mod_kernel,
        grid=(n3,),
        in_specs=[
            pl.BlockSpec((MOD_ROWS, D_MODEL), lambda j: (0, 0)),
            pl.BlockSpec((D_MODEL, D_MODEL), lambda j: (0, j)),
            pl.BlockSpec((1, 1, D_MODEL), lambda j: (j, 0, 0)),
            pl.BlockSpec((2 * RET_HEADS, LANES), lambda j: (0, 0)),
        ],
        out_specs=[
            pl.BlockSpec((1, MOD_ROWS, D_MODEL), lambda j: (j, 0, 0)),
            pl.BlockSpec((RET_HEADS, CHUNK, CHUNK), lambda j: (0, 0, 0)),
            pl.BlockSpec((2 * RET_HEADS, CHUNK, CHUNK), lambda j: (0, 0, 0)),
            pl.BlockSpec((2 * RET_HEADS, LANES), lambda j: (0, 0)),
            pl.BlockSpec((2 * RET_HEADS, LANES), lambda j: (0, 0)),
        ],
        out_shape=[
            jax.ShapeDtypeStruct((n3, MOD_ROWS, D_MODEL), F32),
            jax.ShapeDtypeStruct((RET_HEADS, CHUNK, CHUNK), F32),
            jax.ShapeDtypeStruct((2 * RET_HEADS, CHUNK, CHUNK), F32),
            jax.ShapeDtypeStruct((2 * RET_HEADS, LANES), F32),
            jax.ShapeDtypeStruct((2 * RET_HEADS, LANES), F32),
        ],
        compiler_params=pltpu.CompilerParams(dimension_semantics=("arbitrary",),
                                             vmem_limit_bytes=VMEM_LIMIT),
        name="mod",
    )(cc, w_mod, b_mod, dl)


def _inproj_kernel(x_ref, sc_ref, sh_ref, nw_ref, w_ref, qnw_ref, knw_ref,
                   ca_ref, sa_ref, cr_ref, sr_ref,
                   qa_o, kaT_o, vx_o, ga_o, qr_o, krT_o, vr_o, gr_o):
    tm = x_ref.shape[0]
    x = x_ref[...]
    ms = jnp.mean(x * x, axis=-1, keepdims=True)
    a = nw_ref[...] * (1.0 + sc_ref[0])
    hb = ((x * lax.rsqrt(ms + NORM_EPS)) * a + sh_ref[0]).astype(BF16)

    def proj(lo, width):
        return jnp.dot(hb, w_ref[:, lo:lo + width], preferred_element_type=F32)

    lane = lax.broadcasted_iota(jnp.int32, (tm, LANES), 1)
    head_lo = lane < ATT_HEAD_DIM
    half_lo = (lane & (ATT_HEAD_DIM // 2)) == 0
    ca, sa = ca_ref[...], sa_ref[...]
    cr, sr = cr_ref[...], sr_ref[...]
    inv_hd = 1.0 / ATT_HEAD_DIM

    def att_norm_rope(v, w):
        sq = v * v
        s_lo = jnp.sum(jnp.where(head_lo, sq, 0.0), axis=-1, keepdims=True)
        s_hi = jnp.sum(jnp.where(head_lo, 0.0, sq), axis=-1, keepdims=True)
        r = jnp.where(head_lo, lax.rsqrt(s_lo * inv_hd + NORM_EPS),
                      lax.rsqrt(s_hi * inv_hd + NORM_EPS))
        vn = v * r * w
        rot = jnp.where(half_lo, pltpu.roll(vn, LANES - ATT_HEAD_DIM // 2, 1),
                        pltpu.roll(vn, ATT_HEAD_DIM // 2, 1))
        return vn * ca + rot * sa

    def ret_rope(v):
        return v * cr + pltpu.roll(v, RET_HEAD_DIM // 2, 1) * sr

    qnw = qnw_ref[...]
    qa = proj(O_QA, ATT_WIDTH)
    for c in range(ATT_WIDTH // LANES):
        cs = slice(c * LANES, (c + 1) * LANES)
        qa_o[:, cs] = (att_norm_rope(qa[:, cs], qnw) * (ATT_HEAD_DIM ** -0.5)).astype(BF16)

    ka = att_norm_rope(proj(O_KA, ATT_KV_WIDTH), knw_ref[...])
    kaT_o[0, 0] = ka.T.astype(BF16)

    va = proj(O_VA, ATT_KV_WIDTH)
    ones_col = jnp.where(lane == ATT_HEAD_DIM, 1.0, 0.0)
    vx_o[:, 0:LANES] = jnp.where(head_lo, va, ones_col).astype(BF16)
    vx_o[:, LANES:2 * LANES] = jnp.where(head_lo, pltpu.roll(va, ATT_HEAD_DIM, 1),
                                         ones_col).astype(BF16)

    ga_o[...] = _silu(proj(O_GA, ATT_WIDTH)).astype(BF16)

    qr = proj(O_QR, RET_WIDTH)
    kr = proj(O_KR, RET_WIDTH)
    for h in range(RET_HEADS):
        hs = slice(h * RET_HEAD_DIM, (h + 1) * RET_HEAD_DIM)
        qr_o[:, hs] = ret_rope(qr[:, hs]).astype(BF16)
        krh = ret_rope(kr[:, hs]) * (RET_HEAD_DIM ** -0.5)
        krT_o[0, 0, hs, :] = krh.T.astype(BF16)

    vr_o[...] = proj(O_VR, RET_WIDTH).astype(BF16)
    gr_o[...] = _silu(proj(O_GR, RET_WIDTH)).astype(BF16)


def _run_inproj(x2, scale, shift, norm_w, w_in, qnw, knw, tabs, *, batch, seq, tm):
    nt = seq // tm
    rows = batch * seq
    row_map = lambda t, b: (b * nt + t, 0)
    tab_spec = pl.BlockSpec((tm, LANES), lambda t, b: (t, 0))
    vec_spec = pl.BlockSpec((1, 1, D_MODEL), lambda t, b: (b, 0, 0))

    def row_out(width):
        return pl.BlockSpec((tm, width), row_map)

    return pl.pallas_call(
        _inproj_kernel,
        grid=(nt, batch),
        in_specs=[
            pl.BlockSpec((tm, D_MODEL), row_map),
            vec_spec, vec_spec,
            pl.BlockSpec((1, D_MODEL), lambda t, b: (0, 0)),
            pl.BlockSpec((D_MODEL, IN_WIDTH), lambda t, b: (0, 0)),
            pl.BlockSpec((1, LANES), lambda t, b: (0, 0)),
            pl.BlockSpec((1, LANES), lambda t, b: (0, 0)),
            tab_spec, tab_spec, tab_spec, tab_spec,
        ],
        out_specs=[
            row_out(ATT_WIDTH),
            pl.BlockSpec((1, 1, ATT_KV_WIDTH, tm), lambda t, b: (b, t, 0, 0)),
            row_out(VEXT_W),
            row_out(ATT_WIDTH),
            row_out(RET_WIDTH),
            pl.BlockSpec((1, 1, RET_WIDTH, tm), lambda t, b: (b, t, 0, 0)),
            row_out(RET_WIDTH),
            row_out(RET_WIDTH),
        ],
        out_shape=[
            jax.ShapeDtypeStruct((rows, ATT_WIDTH), BF16),
            jax.ShapeDtypeStruct((batch, nt, ATT_KV_WIDTH, tm), BF16),
            jax.ShapeDtypeStruct((rows, VEXT_W), BF16),
            jax.ShapeDtypeStruct((rows, ATT_WIDTH), BF16),
            jax.ShapeDtypeStruct((rows, RET_WIDTH), BF16),
            jax.ShapeDtypeStruct((batch, nt, RET_WIDTH, tm), BF16),
            jax.ShapeDtypeStruct((rows, RET_WIDTH), BF16),
            jax.ShapeDtypeStruct((rows, RET_WIDTH), BF16),
        ],
        compiler_params=pltpu.CompilerParams(dimension_semantics=("arbitrary", "arbitrary"),
                                             vmem_limit_bytes=VMEM_LIMIT),
        name="inproj",
    )(x2, scale, shift, norm_w, w_in, qnw, knw, *tabs)


def _states_kernel(kTc_ref, vc_ref, kTf_ref, vf_ref, kTb_ref, vb_ref, kdec_ref, cdec_ref,
                   sf_o, sb_o, st):
    kdec = kdec_ref[...]
    cdec = cdec_ref[...]
    nh = RET_HEADS

    def chunk_state(kT_ref, v_ref, c, h, row):
        hs = slice(h * RET_HEAD_DIM, (h + 1) * RET_HEAD_DIM)
        cs = slice(c * CHUNK, (c + 1) * CHUNK)
        kd = (kT_ref[0, 0, hs, cs].astype(F32) * kdec[row:row + 1, :]).astype(BF16)
        return jnp.dot(kd, v_ref[cs, hs], preferred_element_type=F32)

    @pl.when(pl.program_id(1) == 0)
    def _():
        for h in range(nh):
            st[h] = (cdec[h:h + 1, :] * chunk_state(kTc_ref, vc_ref, 0, h, h)
                     + chunk_state(kTc_ref, vc_ref, 1, h, h))
            st[nh + h] = (cdec[nh + h:nh + h + 1, :] * chunk_state(kTc_ref, vc_ref, 1, h, nh + h)
                          + chunk_state(kTc_ref, vc_ref, 0, h, nh + h))

    for c in range(CHUNKS_PER_TILE):
        for h in range(nh):
            s = st[h]
            sf_o[0, c, h] = s.astype(BF16)
            st[h] = cdec[h:h + 1, :] * s + chunk_state(kTf_ref, vf_ref, c, h, h)
    for c in reversed(range(CHUNKS_PER_TILE)):
        for h in range(nh):
            s = st[nh + h]
            sb_o[0, c, h] = s.astype(BF16)
            st[nh + h] = cdec[nh + h:nh + h + 1, :] * s + chunk_state(kTb_ref, vb_ref, c, h, nh + h)


def _run_states(krT_c, vr_c, krT, vr, kdec, cdec, *, batch, seq, ctx_len):
    nt = seq // TILE
    nchunks = seq // CHUNK
    s_spec_f = pl.BlockSpec((1, CHUNKS_PER_TILE, RET_HEADS, RET_HEAD_DIM, RET_HEAD_DIM),
                            lambda b, t: (b, t, 0, 0, 0))
    s_spec_b = pl.BlockSpec((1, CHUNKS_PER_TILE, RET_HEADS, RET_HEAD_DIM, RET_HEAD_DIM),
                            lambda b, t: (b, nt - 1 - t, 0, 0, 0))
    s_shape = jax.ShapeDtypeStruct((batch, nchunks, RET_HEADS, RET_HEAD_DIM, RET_HEAD_DIM), BF16)
    small = pl.BlockSpec((2 * RET_HEADS, LANES), lambda b, t: (0, 0))
    return pl.pallas_call(
        _states_kernel,
        grid=(batch, nt),
        in_specs=[
            pl.BlockSpec((1, 1, RET_WIDTH, ctx_len), lambda b, t: (b, 0, 0, 0)),
            pl.BlockSpec((ctx_len, RET_WIDTH), lambda b, t: (b, 0)),
            pl.BlockSpec((1, 1, RET_WIDTH, TILE), lambda b, t: (b, t, 0, 0)),
            pl.BlockSpec((TILE, RET_WIDTH), lambda b, t: (b * nt + t, 0)),
            pl.BlockSpec((1, 1, RET_WIDTH, TILE), lambda b, t: (b, nt - 1 - t, 0, 0)),
            pl.BlockSpec((TILE, RET_WIDTH), lambda b, t: (b * nt + nt - 1 - t, 0)),
            small, small,
        ],
        out_specs=[s_spec_f, s_spec_b],
        out_shape=[s_shape, s_shape],
        scratch_shapes=[pltpu.VMEM((2 * RET_HEADS, RET_HEAD_DIM, RET_HEAD_DIM), F32)],
        compiler_params=pltpu.CompilerParams(dimension_semantics=("arbitrary", "arbitrary"),
                                             vmem_limit_bytes=VMEM_LIMIT),
        name="states",
    )(krT_c, vr_c, krT, vr, krT, vr, kdec, cdec)


def _attn_kernel(q_ref, kTc_ref, vc_ref, kT_ref, v_ref, ga_ref, o_ref, *, tq, tk, n_kv):
    q = q_ref[...]
    qs = jnp.concatenate([q[:, h * ATT_HEAD_DIM:(h + 1) * ATT_HEAD_DIM]
                          for h in range(ATT_GROUP)], axis=0)
    rows = ATT_GROUP * tq

    def block(carry, kT, v):
        m, acc = carry
        s = jnp.dot(qs, kT, preferred_element_type=F32)
        m_new = jnp.maximum(m, jnp.max(s, axis=-1, keepdims=True))
        p = jnp.exp(s - m_new).astype(BF16)
        acc = jnp.exp(m - m_new) * acc + jnp.dot(p, v, preferred_element_type=F32)
        return m_new, acc

    carry = (jnp.full((rows, 1), -1e30, F32), jnp.zeros((rows, LANES), F32))
    carry = block(carry, kTc_ref[0, 0], vc_ref[...])

    def body(j, carry):
        off = pl.multiple_of(j * tk, tk)
        return block(carry, kT_ref[0, j], v_ref[pl.ds(off, tk), :])

    _, acc = lax.fori_loop(0, n_kv, body, carry)
    o = acc[:, 0:ATT_HEAD_DIM] * (1.0 / acc[:, ATT_HEAD_DIM:ATT_HEAD_DIM + 1])
    out = jnp.concatenate([o[h * tq:(h + 1) * tq] for h in range(ATT_GROUP)], axis=1)
    o_ref[...] = (out * ga_ref[...].astype(F32)).astype(BF16)


def _run_attn(qa, kaT_c, vx_c, kaT, vx, ga, *, batch, seq, ctx_len, tq):
    nq = seq // tq
    n_kv = seq // TILE
    gw = ATT_GROUP * ATT_HEAD_DIM
    kern = functools.partial(_attn_kernel, tq=tq, tk=TILE, n_kv=n_kv)
    return pl.pallas_call(
        kern,
        grid=(batch, ATT_KV_HEADS, nq),
        in_specs=[
            pl.BlockSpec((tq, gw), lambda b, g, i: (b * nq + i, g)),
            pl.BlockSpec((1, 1, ATT_HEAD_DIM, ctx_len), lambda b, g, i: (b, 0, g, 0)),
            pl.BlockSpec((ctx_len, LANES), lambda b, g, i: (b, g)),
            pl.BlockSpec((1, n_kv, ATT_HEAD_DIM, TILE), lambda b, g, i: (b, 0, g, 0)),
            pl.BlockSpec((seq, LANES), lambda b, g, i: (b, g)),
            pl.BlockSpec((tq, gw), lambda b, g, i: (b * nq + i, g)),
        ],
        out_specs=pl.BlockSpec((tq, gw), lambda b, g, i: (b * nq + i, g)),
        out_shape=jax.ShapeDtypeStruct((batch * seq, ATT_WIDTH), BF16),
        compiler_params=pltpu.CompilerParams(
            dimension_semantics=("arbitrary", "arbitrary", "arbitrary"),
            vmem_limit_bytes=VMEM_LIMIT),
        name="attn",
    )(qa, kaT_c, vx_c, kaT, vx, ga)


def _epilogue_kernel(att_ref, qr_ref, krT_ref, vr_ref, gr_ref, x_ref, sf_ref, sb_ref,
                     dmask_ref, qdec_ref, gnw_ref, gate_ref, wout_ref, fnw_ref, y_ref, ret_scr):
    nh = RET_HEADS
    for c in range(CHUNKS_PER_TILE):
        rs = slice(c * CHUNK, (c + 1) * CHUNK)
        for h in range(nh):
            hs = slice(h * RET_HEAD_DIM, (h + 1) * RET_HEAD_DIM)
            q = qr_ref[rs, hs]
            a = jnp.dot(q, krT_ref[0, 0, hs, rs], preferred_element_type=F32) * dmask_ref[h]
            o = jnp.dot(a.astype(BF16), vr_ref[rs, hs], preferred_element_type=F32)
            o = o + jnp.dot(q, sf_ref[0, c, h], preferred_element_type=F32) * qdec_ref[h]
            o = o + jnp.dot(q, sb_ref[0, c, h], preferred_element_type=F32) * qdec_ref[nh + h]
            mu = jnp.mean(o, axis=-1, keepdims=True)
            d = o - mu
            var = jnp.mean(d * d, axis=-1, keepdims=True)
            yh = d * lax.rsqrt(var + NORM_EPS) * gnw_ref[:, hs] * gr_ref[rs, hs].astype(F32)
            ret_scr[rs, hs] = yh.astype(BF16)
    proj = (jnp.dot(att_ref[...], wout_ref[0:ATT_WIDTH, :], preferred_element_type=F32)
            + jnp.dot(ret_scr[...], wout_ref[ATT_WIDTH:, :], preferred_element_type=F32))
    z = x_ref[...] + gate_ref[0] * proj
    ms = jnp.mean(z * z, axis=-1, keepdims=True)
    y_ref[...] = z * lax.rsqrt(ms + NORM_EPS) * fnw_ref[...]


def _run_epilogue(att, qr, krT, vr, gr, x2, sf, sb, dmask, qdec, gnw, gate, w_out, fnw,
                  *, batch, seq):
    nt = seq // TILE
    row_map = lambda b, t: (b * nt + t, 0)
    s_spec = pl.BlockSpec((1, CHUNKS_PER_TILE, RET_HEADS, RET_HEAD_DIM, RET_HEAD_DIM),
                          lambda b, t: (b, t, 0, 0, 0))
    return pl.pallas_call(
        _epilogue_kernel,
        grid=(batch, nt),
        in_specs=[
            pl.BlockSpec((TILE, ATT_WIDTH), row_map),
            pl.BlockSpec((TILE, RET_WIDTH), row_map),
            pl.BlockSpec((1, 1, RET_WIDTH, TILE), lambda b, t: (b, t, 0, 0)),
            pl.BlockSpec((TILE, RET_WIDTH), row_map),
            pl.BlockSpec((TILE, RET_WIDTH), row_map),
            pl.BlockSpec((TILE, D_MODEL), row_map),
            s_spec, s_spec,
            pl.BlockSpec((RET_HEADS, CHUNK, CHUNK), lambda b, t: (0, 0, 0)),
            pl.BlockSpec((2 * RET_HEADS, CHUNK, CHUNK), lambda b, t: (0, 0, 0)),
            pl.BlockSpec((1, RET_WIDTH), lambda b, t: (0, 0)),
            pl.BlockSpec((1, 1, D_MODEL), lambda b, t: (b, 0, 0)),
            pl.BlockSpec((ATT_WIDTH + RET_WIDTH, D_MODEL), lambda b, t: (0, 0)),
            pl.BlockSpec((1, D_MODEL), lambda b, t: (0, 0)),
        ],
        out_specs=pl.BlockSpec((TILE, D_MODEL), row_map),
        out_shape=jax.ShapeDtypeStruct((batch * seq, D_MODEL), F32),
        scratch_shapes=[pltpu.VMEM((TILE, RET_WIDTH), BF16)],
        compiler_params=pltpu.CompilerParams(dimension_semantics=("arbitrary", "arbitrary"),
                                             vmem_limit_bytes=VMEM_LIMIT),
        name="epilogue",
    )(att, qr, krT, vr, gr, x2, sf, sb, dmask, qdec, gnw, gate, w_out, fnw)


def _rope_tables(seq):
    n_rows = seq // GRID_W
    rows, cols = jnp.meshgrid(jnp.arange(n_rows, dtype=F32), jnp.arange(GRID_W, dtype=F32),
                              indexing="ij")
    rows = rows.reshape(-1)
    cols = cols.reshape(-1)

    def angles(head_dim):
        n_axis = head_dim // 4
        inv_freq = ROPE_THETA ** (-jnp.arange(n_axis, dtype=F32) / n_axis)
        return jnp.concatenate([rows[:, None] * inv_freq, cols[:, None] * inv_freq], axis=-1)

    ang_a = angles(ATT_HEAD_DIM)
    ang_r = angles(RET_HEAD_DIM)
    cos_a = jnp.tile(jnp.cos(ang_a), (1, LANES // (ATT_HEAD_DIM // 2)))
    sin_a = jnp.tile(jnp.concatenate([-jnp.sin(ang_a), jnp.sin(ang_a)], axis=-1),
                     (1, LANES // ATT_HEAD_DIM))
    cos_r = jnp.tile(jnp.cos(ang_r), (1, 2))
    sin_r = jnp.concatenate([-jnp.sin(ang_r), jnp.sin(ang_r)], axis=-1)
    return cos_a, sin_a, cos_r, sin_r


def kernel(x, c, ctx, c_ctx, norm_w, w_mod, b_mod, w_in, q_norm_w, k_norm_w,
           ret_decay_fwd, ret_decay_bwd, ret_gn_w, w_out, final_norm_w):
    batch, seq, d = x.shape
    ctx_len = ctx.shape[1]
    depth = norm_w.shape[0]
    assert d == D_MODEL and depth == 1 and seq % TILE == 0 and ctx_len == 2 * CHUNK
    assert batch + 1 <= MOD_ROWS and w_in.shape[-1] == IN_WIDTH
    layer = 0

    cc = jnp.zeros((MOD_ROWS, D_MODEL), F32).at[:batch].set(c).at[batch].set(c_ctx)
    dl = jnp.broadcast_to(
        jnp.concatenate([ret_decay_fwd[layer], ret_decay_bwd[layer]]).astype(F32)[:, None],
        (2 * RET_HEADS, LANES))
    mod, dmask, qdec, kdec, cdec = _run_mod(cc, w_mod[layer], b_mod[layer].reshape(3, 1, D_MODEL), dl)
    shift, scale, gate = mod[0], mod[1], mod[2]

    w_in_b = w_in[layer].astype(BF16)
    nw = norm_w[layer].reshape(1, D_MODEL)
    qnw = jnp.tile(q_norm_w[layer], LANES // ATT_HEAD_DIM).reshape(1, LANES)
    knw = jnp.tile(k_norm_w[layer], LANES // ATT_HEAD_DIM).reshape(1, LANES)

    ones = jnp.ones((ctx_len, LANES), F32)
    zeros = jnp.zeros((ctx_len, LANES), F32)
    sc_c = jnp.broadcast_to(scale[batch].reshape(1, 1, D_MODEL), (batch, 1, D_MODEL))
    sh_c = jnp.broadcast_to(shift[batch].reshape(1, 1, D_MODEL), (batch, 1, D_MODEL))
    (_, kaT_c, vx_c, _, _, krT_c, vr_c, _) = _run_inproj(
        ctx.reshape(batch * ctx_len, d), sc_c, sh_c, nw, w_in_b, qnw, knw,
        (ones, zeros, ones, zeros), batch=batch, seq=ctx_len, tm=ctx_len)

    x2 = x.reshape(batch * seq, d)
    sc_l = scale[:batch].reshape(batch, 1, D_MODEL)
    sh_l = shift[:batch].reshape(batch, 1, D_MODEL)
    qa, kaT, vx, ga, qr, krT, vr, gr = _run_inproj(
        x2, sc_l, sh_l, nw, w_in_b, qnw, knw, _rope_tables(seq), batch=batch, seq=seq, tm=TILE)

    sf, sb = _run_states(krT_c, vr_c, krT, vr, kdec, cdec, batch=batch, seq=seq, ctx_len=ctx_len)
    att = _run_attn(qa, kaT_c, vx_c, kaT, vx, ga, batch=batch, seq=seq, ctx_len=ctx_len, tq=128)
    y = _run_epilogue(att, qr, krT, vr, gr, x2, sf, sb, dmask, qdec,
                      ret_gn_w[layer].reshape(1, RET_WIDTH),
                      gate[:batch].reshape(batch, 1, D_MODEL),
                      w_out[layer].astype(BF16), final_norm_w.reshape(1, D_MODEL),
                      batch=batch, seq=seq)
    return y.reshape(batch, seq, d)
```

```python
import functools

import jax
import jax.numpy as jnp
import numpy as np
from jax import lax
from jax.experimental import pallas as pl
from jax.experimental.pallas import tpu as pltpu

F32 = jnp.float32
BF16 = jnp.bfloat16

D_MODEL = 1024
GRID_W = 64
ATT_HEADS = 8
ATT_KV_HEADS = 2
ATT_GROUP = ATT_HEADS // ATT_KV_HEADS
ATT_HEAD_DIM = 64
ATT_WIDTH = ATT_HEADS * ATT_HEAD_DIM
ATT_KV_WIDTH = ATT_KV_HEADS * ATT_HEAD_DIM
RET_HEADS = 4
RET_HEAD_DIM = 128
RET_WIDTH = RET_HEADS * RET_HEAD_DIM
CHUNK = 128
ROPE_THETA = 10000.0
NORM_EPS = 1e-6
LANES = 128
TILE = 512
CHUNKS_PER_TILE = TILE // CHUNK
MOD_ROWS = 8
VEXT_W = 2 * LANES

O_QA = 0
O_KA = O_QA + ATT_WIDTH
O_VA = O_KA + ATT_KV_WIDTH
O_GA = O_VA + ATT_KV_WIDTH
O_QR = O_GA + ATT_WIDTH
O_KR = O_QR + RET_WIDTH
O_VR = O_KR + RET_WIDTH
O_GR = O_VR + RET_WIDTH
IN_WIDTH = O_GR + RET_WIDTH

VMEM_LIMIT = 48 * 1024 * 1024

LOG2_E = 1.4426950408889634
Q_SCALE_LOG2 = ATT_HEAD_DIM ** -0.5 * LOG2_E
SCORE_LOG2_LIMIT = 100.0
BF16_ROUND_UP = 1.0 + 2.0 ** -8


def _silu(x):
    return x * (1.0 / (1.0 + jnp.exp(-x)))


def _mod_kernel(cc_ref, w_ref, b_ref, dl_ref, mod_ref, dmask_ref, qdec_ref, kdec_ref, cdec_ref):
    sc = _silu(cc_ref[...])
    mod_ref[0] = jnp.dot(sc, w_ref[...], preferred_element_type=F32,
                         precision=lax.Precision.HIGHEST) + b_ref[0]

    @pl.when(pl.program_id(0) == 0)
    def _():
        x = dl_ref[...]
        lg = jnp.minimum(x, 0.0) - jnp.log1p(jnp.exp(-jnp.abs(x)))
        ii = lax.broadcasted_iota(jnp.int32, (CHUNK, CHUNK), 0).astype(F32)
        jj = lax.broadcasted_iota(jnp.int32, (CHUNK, CHUNK), 1).astype(F32)
        d = ii - jj
        for h in range(RET_HEADS):
            lf = lg[h:h + 1, :]
            lb = lg[RET_HEADS + h:RET_HEADS + h + 1, :]
            fwd = jnp.exp(lf * jnp.maximum(d, 0.0))
            bwd = jnp.exp(lb * jnp.maximum(-d, 0.0))
            dmask_ref[h] = jnp.where(d > 0, fwd, jnp.where(d < 0, bwd, 2.0))
            qdec_ref[h] = jnp.exp(lf * (ii + 1.0))
            qdec_ref[RET_HEADS + h] = jnp.exp(lb * (float(CHUNK) - ii))
        lane = lax.broadcasted_iota(jnp.int32, (2 * RET_HEADS, LANES), 1).astype(F32)
        row = lax.broadcasted_iota(jnp.int32, (2 * RET_HEADS, LANES), 0)
        kdec_ref[...] = jnp.where(row < RET_HEADS, jnp.exp(lg * (float(CHUNK) - 1.0 - lane)),
                                  jnp.exp(lg * lane))
        cdec_ref[...] = jnp.exp(lg * float(CHUNK))


def _run_mod(cc, w_mod, b_mod, dl):
    n3 = 3
    return pl.pallas_call(
        _mod_kernel,
        grid=(n3,),
        in_specs=[
            pl.BlockSpec((MOD_ROWS, D_MODEL), lambda j: (0, 0)),
            pl.BlockSpec((D_MODEL, D_MODEL), lambda j: (0, j)),
            pl.BlockSpec((1, 1, D_MODEL), lambda j: (j, 0, 0)),
            pl.BlockSpec((2 * RET_HEADS, LANES), lambda j: (0, 0)),
        ],
        out_specs=[
            pl.BlockSpec((1, MOD_ROWS, D_MODEL), lambda j: (j, 0, 0)),
            pl.BlockSpec((RET_HEADS, CHUNK, CHUNK), lambda j: (0, 0, 0)),
            pl.BlockSpec((2 * RET_HEADS, CHUNK, CHUNK), lambda j: (0, 0, 0)),
            pl.BlockSpec((2 * RET_HEADS, LANES), lambda j: (0, 0)),
            pl.BlockSpec((2 * RET_HEADS, LANES), lambda j: (0, 0)),
        ],
        out_shape=[
            jax.ShapeDtypeStruct((n3, MOD_ROWS, D_MODEL), F32),
            jax.ShapeDtypeStruct((RET_HEADS, CHUNK, CHUNK), F32),
            jax.ShapeDtypeStruct((2 * RET_HEADS, CHUNK, CHUNK), F32),
            jax.ShapeDtypeStruct((2 * RET_HEADS, LANES), F32),
            jax.ShapeDtypeStruct((2 * RET_HEADS, LANES), F32),
        ],
        compiler_params=pltpu.CompilerParams(dimension_semantics=("arbitrary",),
                                             vmem_limit_bytes=VMEM_LIMIT),
        name="mod",
    )(cc, w_mod, b_mod, dl)


def _inproj_kernel(x_ref, sc_ref, sh_ref, nw_ref, w_ref, qnw_ref, knw_ref,
                   ca_ref, sa_ref, cr_ref, sr_ref,
                   qa_o, kaT_o, vx_o, ga_o, qr_o, krT_o, vr_o, gr_o):
    tm = x_ref.shape[0]
    x = x_ref[...]
    ms = jnp.mean(x * x, axis=-1, keepdims=True)
    a = nw_ref[...] * (1.0 + sc_ref[0])
    hb = ((x * lax.rsqrt(ms + NORM_EPS)) * a + sh_ref[0]).astype(BF16)

    def proj(lo, width):
        return jnp.dot(hb, w_ref[:, lo:lo + width], preferred_element_type=F32)

    lane = lax.broadcasted_iota(jnp.int32, (tm, LANES), 1)
    head_lo = lane < ATT_HEAD_DIM
    half_lo = (lane & (ATT_HEAD_DIM // 2)) == 0
    ca, sa = ca_ref[...], sa_ref[...]
    cr, sr = cr_ref[...], sr_ref[...]
    inv_hd = 1.0 / ATT_HEAD_DIM

    def att_norm_rope(v, w):
        sq = v * v
        s_lo = jnp.sum(jnp.where(head_lo, sq, 0.0), axis=-1, keepdims=True)
        s_hi = jnp.sum(jnp.where(head_lo, 0.0, sq), axis=-1, keepdims=True)
        r = jnp.where(head_lo, lax.rsqrt(s_lo * inv_hd + NORM_EPS),
                      lax.rsqrt(s_hi * inv_hd + NORM_EPS))
        vn = v * r * w
        rot = jnp.where(half_lo, pltpu.roll(vn, LANES - ATT_HEAD_DIM // 2, 1),
                        pltpu.roll(vn, ATT_HEAD_DIM // 2, 1))
        return vn * ca + rot * sa

    def ret_rope(v):
        return v * cr + pltpu.roll(v, RET_HEAD_DIM // 2, 1) * sr

    qnw = qnw_ref[...]
    qa = proj(O_QA, ATT_WIDTH)
    for c in range(ATT_WIDTH // LANES):
        cs = slice(c * LANES, (c + 1) * LANES)
        qa_o[:, cs] = (att_norm_rope(qa[:, cs], qnw) * Q_SCALE_LOG2).astype(BF16)

    ka = att_norm_rope(proj(O_KA, ATT_KV_WIDTH), knw_ref[...])
    kaT_o[0, 0] = ka.T.astype(BF16)

    va = proj(O_VA, ATT_KV_WIDTH)
    ones_col = jnp.where(lane == ATT_HEAD_DIM, 1.0, 0.0)
    vx_o[:, 0:LANES] = jnp.where(head_lo, va, ones_col).astype(BF16)
    vx_o[:, LANES:2 * LANES] = jnp.where(head_lo, pltpu.roll(va, ATT_HEAD_DIM, 1),
                                         ones_col).astype(BF16)

    ga_o[...] = _silu(proj(O_GA, ATT_WIDTH)).astype(BF16)

    qr = proj(O_QR, RET_WIDTH)
    kr = proj(O_KR, RET_WIDTH)
    for h in range(RET_HEADS):
        hs = slice(h * RET_HEAD_DIM, (h + 1) * RET_HEAD_DIM)
        qr_o[:, hs] = ret_rope(qr[:, hs]).astype(BF16)
        krh = ret_rope(kr[:, hs]) * (RET_HEAD_DIM ** -0.5)
        krT_o[0, 0, hs, :] = krh.T.astype(BF16)

    vr_o[...] = proj(O_VR, RET_WIDTH).astype(BF16)
    gr_o[...] = _silu(proj(O_GR, RET_WIDTH)).astype(BF16)


def _run_inproj(x2, scale, shift, norm_w, w_in, qnw, knw, tabs, *, batch, seq, tm):
    nt = seq // tm
    rows = batch * seq
    row_map = lambda t, b: (b * nt + t, 0)
    tab_spec = pl.BlockSpec((tm, LANES), lambda t, b: (t, 0))
    vec_spec = pl.BlockSpec((1, 1, D_MODEL), lambda t, b: (b, 0, 0))

    def row_out(width):
        return pl.BlockSpec((tm, width), row_map)

    return pl.pallas_call(
        _inproj_kernel,
        grid=(nt, batch),
        in_specs=[
            pl.BlockSpec((tm, D_MODEL), row_map),
            vec_spec, vec_spec,
            pl.BlockSpec((1, D_MODEL), lambda t, b: (0, 0)),
            pl.BlockSpec((D_MODEL, IN_WIDTH), lambda t, b: (0, 0)),
            pl.BlockSpec((1, LANES), lambda t, b: (0, 0)),
            pl.BlockSpec((1, LANES), lambda t, b: (0, 0)),
            tab_spec, tab_spec, tab_spec, tab_spec,
        ],
        out_specs=[
            row_out(ATT_WIDTH),
            pl.BlockSpec((1, 1, ATT_KV_WIDTH, tm), lambda t, b: (b, t, 0, 0)),
            row_out(VEXT_W),
            row_out(ATT_WIDTH),
            row_out(RET_WIDTH),
            pl.BlockSpec((1, 1, RET_WIDTH, tm), lambda t, b: (b, t, 0, 0)),
            row_out(RET_WIDTH),
            row_out(RET_WIDTH),
        ],
        out_shape=[
            jax.ShapeDtypeStruct((rows, ATT_WIDTH), BF16),
            jax.ShapeDtypeStruct((batch, nt, ATT_KV_WIDTH, tm), BF16),
            jax.ShapeDtypeStruct((rows, VEXT_W), BF16),
            jax.ShapeDtypeStruct((rows, ATT_WIDTH), BF16),
            jax.ShapeDtypeStruct((rows, RET_WIDTH), BF16),
            jax.ShapeDtypeStruct((batch, nt, RET_WIDTH, tm), BF16),
            jax.ShapeDtypeStruct((rows, RET_WIDTH), BF16),
            jax.ShapeDtypeStruct((rows, RET_WIDTH), BF16),
        ],
        compiler_params=pltpu.CompilerParams(dimension_semantics=("arbitrary", "arbitrary"),
                                             vmem_limit_bytes=VMEM_LIMIT),
        name="inproj",
    )(x2, scale, shift, norm_w, w_in, qnw, knw, *tabs)


def _states_kernel(kTc_ref, vc_ref, kTf_ref, vf_ref, kTb_ref, vb_ref, kdec_ref, cdec_ref,
                   sf_o, sb_o, st):
    kdec = kdec_ref[...]
    cdec = cdec_ref[...]
    nh = RET_HEADS

    def chunk_state(kT_ref, v_ref, c, h, row):
        hs = slice(h * RET_HEAD_DIM, (h + 1) * RET_HEAD_DIM)
        cs = slice(c * CHUNK, (c + 1) * CHUNK)
        kd = (kT_ref[0, 0, hs, cs].astype(F32) * kdec[row:row + 1, :]).astype(BF16)
        return jnp.dot(kd, v_ref[cs, hs], preferred_element_type=F32)

    @pl.when(pl.program_id(1) == 0)
    def _():
        for h in range(nh):
            st[h] = (cdec[h:h + 1, :] * chunk_state(kTc_ref, vc_ref, 0, h, h)
                     + chunk_state(kTc_ref, vc_ref, 1, h, h))
            st[nh + h] = (cdec[nh + h:nh + h + 1, :] * chunk_state(kTc_ref, vc_ref, 1, h, nh + h)
                          + chunk_state(kTc_ref, vc_ref, 0, h, nh + h))

    for c in range(CHUNKS_PER_TILE):
        for h in range(nh):
            s = st[h]
            sf_o[0, c, h] = s.astype(BF16)
            st[h] = cdec[h:h + 1, :] * s + chunk_state(kTf_ref, vf_ref, c, h, h)
    for c in reversed(range(CHUNKS_PER_TILE)):
        for h in range(nh):
            s = st[nh + h]
            sb_o[0, c, h] = s.astype(BF16)
            st[nh + h] = cdec[nh + h:nh + h + 1, :] * s + chunk_state(kTb_ref, vb_ref, c, h, nh + h)


def _run_states(krT_c, vr_c, krT, vr, kdec, cdec, *, batch, seq, ctx_len):
    nt = seq // TILE
    nchunks = seq // CHUNK
    s_spec_f = pl.BlockSpec((1, CHUNKS_PER_TILE, RET_HEADS, RET_HEAD_DIM, RET_HEAD_DIM),
                            lambda b, t: (b, t, 0, 0, 0))
    s_spec_b = pl.BlockSpec((1, CHUNKS_PER_TILE, RET_HEADS, RET_HEAD_DIM, RET_HEAD_DIM),
                            lambda b, t: (b, nt - 1 - t, 0, 0, 0))
    s_shape = jax.ShapeDtypeStruct((batch, nchunks, RET_HEADS, RET_HEAD_DIM, RET_HEAD_DIM), BF16)
    small = pl.BlockSpec((2 * RET_HEADS, LANES), lambda b, t: (0, 0))
    return pl.pallas_call(
        _states_kernel,
        grid=(batch, nt),
        in_specs=[
            pl.BlockSpec((1, 1, RET_WIDTH, ctx_len), lambda b, t: (b, 0, 0, 0)),
            pl.BlockSpec((ctx_len, RET_WIDTH), lambda b, t: (b, 0)),
            pl.BlockSpec((1, 1, RET_WIDTH, TILE), lambda b, t: (b, t, 0, 0)),
            pl.BlockSpec((TILE, RET_WIDTH), lambda b, t: (b * nt + t, 0)),
            pl.BlockSpec((1, 1, RET_WIDTH, TILE), lambda b, t: (b, nt - 1 - t, 0, 0)),
            pl.BlockSpec((TILE, RET_WIDTH), lambda b, t: (b * nt + nt - 1 - t, 0)),
            small, small,
        ],
        out_specs=[s_spec_f, s_spec_b],
        out_shape=[s_shape, s_shape],
        scratch_shapes=[pltpu.VMEM((2 * RET_HEADS, RET_HEAD_DIM, RET_HEAD_DIM), F32)],
        compiler_params=pltpu.CompilerParams(dimension_semantics=("arbitrary", "arbitrary"),
                                             vmem_limit_bytes=VMEM_LIMIT),
        name="states",
    )(krT_c, vr_c, krT, vr, krT, vr, kdec, cdec)


def _attn_kernel(q_ref, kTc_ref, vc_ref, kT_ref, v_ref, ga_ref, o_ref, *, tq, tk, n_kv,
                 bounded, unroll):
    q = q_ref[...]
    qs = jnp.concatenate([q[:, h * ATT_HEAD_DIM:(h + 1) * ATT_HEAD_DIM]
                          for h in range(ATT_GROUP)], axis=0)
    rows = ATT_GROUP * tq

    if bounded:
        def block(acc, kT, v):
            s = jnp.dot(qs, kT, preferred_element_type=F32)
            return acc + jnp.dot(jnp.exp2(s).astype(BF16), v, preferred_element_type=F32)

        carry = jnp.zeros((rows, LANES), F32)
    else:
        def block(carry, kT, v):
            m, acc = carry
            s = jnp.dot(qs, kT, preferred_element_type=F32)
            m_new = jnp.maximum(m, jnp.max(s, axis=-1, keepdims=True))
            p = jnp.exp2(s - m_new).astype(BF16)
            acc = jnp.exp2(m - m_new) * acc + jnp.dot(p, v, preferred_element_type=F32)
            return m_new, acc

        carry = (jnp.full((rows, 1), -1e30, F32), jnp.zeros((rows, LANES), F32))

    carry = block(carry, kTc_ref[0, 0], vc_ref[...])

    def body(j, carry):
        off = pl.multiple_of(j * tk, tk)
        return block(carry, kT_ref[0, j], v_ref[pl.ds(off, tk), :])

    carry = lax.fori_loop(0, n_kv, body, carry, unroll=unroll)
    acc = carry if bounded else carry[1]
    o = acc[:, 0:ATT_HEAD_DIM] * (1.0 / acc[:, ATT_HEAD_DIM:ATT_HEAD_DIM + 1])
    out = jnp.concatenate([o[h * tq:(h + 1) * tq] for h in range(ATT_GROUP)], axis=1)
    o_ref[...] = (out * ga_ref[...].astype(F32)).astype(BF16)


def _run_attn(qa, kaT_c, vx_c, kaT, vx, ga, *, batch, seq, ctx_len, tq, bounded, unroll):
    nq = seq // tq
    n_kv = seq // TILE
    gw = ATT_GROUP * ATT_HEAD_DIM
    kern = functools.partial(_attn_kernel, tq=tq, tk=TILE, n_kv=n_kv, bounded=bounded,
                             unroll=unroll)
    return pl.pallas_call(
        kern,
        grid=(batch, ATT_KV_HEADS, nq),
        in_specs=[
            pl.BlockSpec((tq, gw), lambda b, g, i: (b * nq + i, g)),
            pl.BlockSpec((1, 1, ATT_HEAD_DIM, ctx_len), lambda b, g, i: (b, 0, g, 0)),
            pl.BlockSpec((ctx_len, LANES), lambda b, g, i: (b, g)),
            pl.BlockSpec((1, n_kv, ATT_HEAD_DIM, TILE), lambda b, g, i: (b, 0, g, 0)),
            pl.BlockSpec((seq, LANES), lambda b, g, i: (b, g)),
            pl.BlockSpec((tq, gw), lambda b, g, i: (b * nq + i, g)),
        ],
        out_specs=pl.BlockSpec((tq, gw), lambda b, g, i: (b * nq + i, g)),
        out_shape=jax.ShapeDtypeStruct((batch * seq, ATT_WIDTH), BF16),
        compiler_params=pltpu.CompilerParams(
            dimension_semantics=("arbitrary", "arbitrary", "arbitrary"),
            vmem_limit_bytes=VMEM_LIMIT),
        name="attn_bounded" if bounded else "attn_online",
    )(qa, kaT_c, vx_c, kaT, vx, ga)


def _epilogue_kernel(att_ref, qr_ref, krT_ref, vr_ref, gr_ref, x_ref, sf_ref, sb_ref,
                     dmask_ref, qdec_ref, gnw_ref, gate_ref, wout_ref, fnw_ref, y_ref, ret_scr):
    nh = RET_HEADS
    for c in range(CHUNKS_PER_TILE):
        rs = slice(c * CHUNK, (c + 1) * CHUNK)
        for h in range(nh):
            hs = slice(h * RET_HEAD_DIM, (h + 1) * RET_HEAD_DIM)
            q = qr_ref[rs, hs]
            a = jnp.dot(q, krT_ref[0, 0, hs, rs], preferred_element_type=F32) * dmask_ref[h]
            o = jnp.dot(a.astype(BF16), vr_ref[rs, hs], preferred_element_type=F32)
            o = o + jnp.dot(q, sf_ref[0, c, h], preferred_element_type=F32) * qdec_ref[h]
            o = o + jnp.dot(q, sb_ref[0, c, h], preferred_element_type=F32) * qdec_ref[nh + h]
            mu = jnp.mean(o, axis=-1, keepdims=True)
            d = o - mu
            var = jnp.mean(d * d, axis=-1, keepdims=True)
            yh = d * lax.rsqrt(var + NORM_EPS) * gnw_ref[:, hs] * gr_ref[rs, hs].astype(F32)
            ret_scr[rs, hs] = yh.astype(BF16)
    proj = (jnp.dot(att_ref[...], wout_ref[0:ATT_WIDTH, :], preferred_element_type=F32)
            + jnp.dot(ret_scr[...], wout_ref[ATT_WIDTH:, :], preferred_element_type=F32))
    z = x_ref[...] + gate_ref[0] * proj
    ms = jnp.mean(z * z, axis=-1, keepdims=True)
    y_ref[...] = z * lax.rsqrt(ms + NORM_EPS) * fnw_ref[...]


def _run_epilogue(att, qr, krT, vr, gr, x2, sf, sb, dmask, qdec, gnw, gate, w_out, fnw,
                  *, batch, seq):
    nt = seq // TILE
    row_map = lambda b, t: (b * nt + t, 0)
    s_spec = pl.BlockSpec((1, CHUNKS_PER_TILE, RET_HEADS, RET_HEAD_DIM, RET_HEAD_DIM),
                          lambda b, t: (b, t, 0, 0, 0))
    return pl.pallas_call(
        _epilogue_kernel,
        grid=(batch, nt),
        in_specs=[
            pl.BlockSpec((TILE, ATT_WIDTH), row_map),
            pl.BlockSpec((TILE, RET_WIDTH), row_map),
            pl.BlockSpec((1, 1, RET_WIDTH, TILE), lambda b, t: (b, t, 0, 0)),
            pl.BlockSpec((TILE, RET_WIDTH), row_map),
            pl.BlockSpec((TILE, RET_WIDTH), row_map),
            pl.BlockSpec((TILE, D_MODEL), row_map),
            s_spec, s_spec,
            pl.BlockSpec((RET_HEADS, CHUNK, CHUNK), lambda b, t: (0, 0, 0)),
            pl.BlockSpec((2 * RET_HEADS, CHUNK, CHUNK), lambda b, t: (0, 0, 0)),
            pl.BlockSpec((1, RET_WIDTH), lambda b, t: (0, 0)),
            pl.BlockSpec((1, 1, D_MODEL), lambda b, t: (b, 0, 0)),
            pl.BlockSpec((ATT_WIDTH + RET_WIDTH, D_MODEL), lambda b, t: (0, 0)),
            pl.BlockSpec((1, D_MODEL), lambda b, t: (0, 0)),
        ],
        out_specs=pl.BlockSpec((TILE, D_MODEL), row_map),
        out_shape=jax.ShapeDtypeStruct((batch * seq, D_MODEL), F32),
        scratch_shapes=[pltpu.VMEM((TILE, RET_WIDTH), BF16)],
        compiler_params=pltpu.CompilerParams(dimension_semantics=("arbitrary", "arbitrary"),
                                             vmem_limit_bytes=VMEM_LIMIT),
        name="epilogue",
    )(att, qr, krT, vr, gr, x2, sf, sb, dmask, qdec, gnw, gate, w_out, fnw)


def _rope_tables(seq):
    n_rows = seq // GRID_W
    rows, cols = jnp.meshgrid(jnp.arange(n_rows, dtype=F32), jnp.arange(GRID_W, dtype=F32),
                              indexing="ij")
    rows = rows.reshape(-1)
    cols = cols.reshape(-1)

    def angles(head_dim):
        n_axis = head_dim // 4
        inv_freq = ROPE_THETA ** (-jnp.arange(n_axis, dtype=F32) / n_axis)
        return jnp.concatenate([rows[:, None] * inv_freq, cols[:, None] * inv_freq], axis=-1)

    ang_a = angles(ATT_HEAD_DIM)
    ang_r = angles(RET_HEAD_DIM)
    cos_a = jnp.tile(jnp.cos(ang_a), (1, LANES // (ATT_HEAD_DIM // 2)))
    sin_a = jnp.tile(jnp.concatenate([-jnp.sin(ang_a), jnp.sin(ang_a)], axis=-1),
                     (1, LANES // ATT_HEAD_DIM))
    cos_r = jnp.tile(jnp.cos(ang_r), (1, 2))
    sin_r = jnp.concatenate([-jnp.sin(ang_r), jnp.sin(ang_r)], axis=-1)
    return cos_a, sin_a, cos_r, sin_r


def kernel(x, c, ctx, c_ctx, norm_w, w_mod, b_mod, w_in, q_norm_w, k_norm_w,
           ret_decay_fwd, ret_decay_bwd, ret_gn_w, w_out, final_norm_w):
    batch, seq, d = x.shape
    ctx_len = ctx.shape[1]
    depth = norm_w.shape[0]
    assert d == D_MODEL and depth == 1 and seq % TILE == 0 and ctx_len == 2 * CHUNK
    assert batch + 1 <= MOD_ROWS and w_in.shape[-1] == IN_WIDTH
    layer = 0

    cc = jnp.zeros((MOD_ROWS, D_MODEL), F32).at[:batch].set(c).at[batch].set(c_ctx)
    dl = jnp.broadcast_to(
        jnp.concatenate([ret_decay_fwd[layer], ret_decay_bwd[layer]]).astype(F32)[:, None],
        (2 * RET_HEADS, LANES))
    mod, dmask, qdec, kdec, cdec = _run_mod(cc, w_mod[layer], b_mod[layer].reshape(3, 1, D_MODEL), dl)
    shift, scale, gate = mod[0], mod[1], mod[2]

    w_in_b = w_in[layer].astype(BF16)
    nw = norm_w[layer].reshape(1, D_MODEL)
    qnw = jnp.tile(q_norm_w[layer], LANES // ATT_HEAD_DIM).reshape(1, LANES)
    knw = jnp.tile(k_norm_w[layer], LANES // ATT_HEAD_DIM).reshape(1, LANES)

    ones = jnp.ones((ctx_len, LANES), F32)
    zeros = jnp.zeros((ctx_len, LANES), F32)
    sc_c = jnp.broadcast_to(scale[batch].reshape(1, 1, D_MODEL), (batch, 1, D_MODEL))
    sh_c = jnp.broadcast_to(shift[batch].reshape(1, 1, D_MODEL), (batch, 1, D_MODEL))
    (_, kaT_c, vx_c, _, _, krT_c, vr_c, _) = _run_inproj(
        ctx.reshape(batch * ctx_len, d), sc_c, sh_c, nw, w_in_b, qnw, knw,
        (ones, zeros, ones, zeros), batch=batch, seq=ctx_len, tm=ctx_len)

    x2 = x.reshape(batch * seq, d)
    sc_l = scale[:batch].reshape(batch, 1, D_MODEL)
    sh_l = shift[:batch].reshape(batch, 1, D_MODEL)
    qa, kaT, vx, ga, qr, krT, vr, gr = _run_inproj(
        x2, sc_l, sh_l, nw, w_in_b, qnw, knw, _rope_tables(seq), batch=batch, seq=seq, tm=TILE)

    sf, sb = _run_states(krT_c, vr_c, krT, vr, kdec, cdec, batch=batch, seq=seq, ctx_len=ctx_len)
    score_bound = (ATT_HEAD_DIM * Q_SCALE_LOG2 * BF16_ROUND_UP ** 2
                   * jnp.max(jnp.abs(q_norm_w[layer])) * jnp.max(jnp.abs(k_norm_w[layer])))
    attn = functools.partial(_run_attn, batch=batch, seq=seq, ctx_len=ctx_len, tq=128)
    att = lax.cond(score_bound <= SCORE_LOG2_LIMIT,
                   functools.partial(attn, bounded=True, unroll=16),
                   functools.partial(attn, bounded=False, unroll=1),
                   qa, kaT_c, vx_c, kaT, vx, ga)
    y = _run_epilogue(att, qr, krT, vr, gr, x2, sf, sb, dmask, qdec,
                      ret_gn_w[layer].reshape(1, RET_WIDTH),
                      gate[:batch].reshape(batch, 1, D_MODEL),
                      w_out[layer].astype(BF16), final_norm_w.reshape(1, D_MODEL),
                      batch=batch, seq=seq)
    return y.reshape(batch, seq, d)
```

```python
import functools

import jax
import jax.numpy as jnp
import numpy as np
from jax import lax
from jax.experimental import pallas as pl
from jax.experimental.pallas import tpu as pltpu

F32 = jnp.float32
BF16 = jnp.bfloat16

D_MODEL = 1024
GRID_W = 64
ATT_HEADS = 8
ATT_KV_HEADS = 2
ATT_GROUP = ATT_HEADS // ATT_KV_HEADS
ATT_HEAD_DIM = 64
ATT_WIDTH = ATT_HEADS * ATT_HEAD_DIM
ATT_KV_WIDTH = ATT_KV_HEADS * ATT_HEAD_DIM
RET_HEADS = 4
RET_HEAD_DIM = 128
RET_WIDTH = RET_HEADS * RET_HEAD_DIM
CHUNK = 128
ROPE_THETA = 10000.0
NORM_EPS = 1e-6
LANES = 128
TILE = 512
CHUNKS_PER_TILE = TILE // CHUNK
MOD_ROWS = 8
VEXT_W = 2 * LANES

O_QA = 0
O_KA = O_QA + ATT_WIDTH
O_VA = O_KA + ATT_KV_WIDTH
O_GA = O_VA + ATT_KV_WIDTH
O_QR = O_GA + ATT_WIDTH
O_KR = O_QR + RET_WIDTH
O_VR = O_KR + RET_WIDTH
O_GR = O_VR + RET_WIDTH
IN_WIDTH = O_GR + RET_WIDTH

VMEM_LIMIT = 48 * 1024 * 1024

LOG2_E = 1.4426950408889634
Q_SCALE_LOG2 = ATT_HEAD_DIM ** -0.5 * LOG2_E
SCORE_LOG2_LIMIT = 100.0
BF16_ROUND_UP = 1.0 + 2.0 ** -8


def _silu(x):
    return x * (1.0 / (1.0 + jnp.exp(-x)))


def _mod_kernel(cc_ref, w_ref, b_ref, dl_ref, mod_ref, dmask_ref, qdec_ref, kdec_ref, cdec_ref):
    sc = _silu(cc_ref[...])
    mod_ref[0] = jnp.dot(sc, w_ref[...], preferred_element_type=F32,
                         precision=lax.Precision.HIGHEST) + b_ref[0]

    @pl.when(pl.program_id(0) == 0)
    def _():
        x = dl_ref[...]
        lg = jnp.minimum(x, 0.0) - jnp.log1p(jnp.exp(-jnp.abs(x)))
        ii = lax.broadcasted_iota(jnp.int32, (CHUNK, CHUNK), 0).astype(F32)
        jj = lax.broadcasted_iota(jnp.int32, (CHUNK, CHUNK), 1).astype(F32)
        d = ii - jj
        for h in range(RET_HEADS):
            lf = lg[h:h + 1, :]
            lb = lg[RET_HEADS + h:RET_HEADS + h + 1, :]
            fwd = jnp.exp(lf * jnp.maximum(d, 0.0))
            bwd = jnp.exp(lb * jnp.maximum(-d, 0.0))
            dmask_ref[h] = jnp.where(d > 0, fwd, jnp.where(d < 0, bwd, 2.0))
            qdec_ref[h] = jnp.exp(lf * (ii + 1.0))
            qdec_ref[RET_HEADS + h] = jnp.exp(lb * (float(CHUNK) - ii))
        lane = lax.broadcasted_iota(jnp.int32, (2 * RET_HEADS, LANES), 1).astype(F32)
        row = lax.broadcasted_iota(jnp.int32, (2 * RET_HEADS, LANES), 0)
        kdec_ref[...] = jnp.where(row < RET_HEADS, jnp.exp(lg * (float(CHUNK) - 1.0 - lane)),
                                  jnp.exp(lg * lane))
        cdec_ref[...] = jnp.exp(lg * float(CHUNK))


def _run_mod(cc, w_mod, b_mod, dl):
    n3 = 3
    return pl.pallas_call(
        _mod_kernel,
        grid=(n3,),
        in_specs=[
            pl.BlockSpec((MOD_ROWS, D_MODEL), lambda j: (0, 0)),
            pl.BlockSpec((D_MODEL, D_MODEL), lambda j: (0, j)),
            pl.BlockSpec((1, 1, D_MODEL), lambda j: (j, 0, 0)),
            pl.BlockSpec((2 * RET_HEADS, LANES), lambda j: (0, 0)),
        ],
        out_specs=[
            pl.BlockSpec((1, MOD_ROWS, D_MODEL), lambda j: (j, 0, 0)),
            pl.BlockSpec((RET_HEADS, CHUNK, CHUNK), lambda j: (0, 0, 0)),
            pl.BlockSpec((2 * RET_HEADS, CHUNK, CHUNK), lambda j: (0, 0, 0)),
            pl.BlockSpec((2 * RET_HEADS, LANES), lambda j: (0, 0)),
            pl.BlockSpec((2 * RET_HEADS, LANES), lambda j: (0, 0)),
        ],
        out_shape=[
            jax.ShapeDtypeStruct((n3, MOD_ROWS, D_MODEL), F32),
            jax.ShapeDtypeStruct((RET_HEADS, CHUNK, CHUNK), F32),
            jax.ShapeDtypeStruct((2 * RET_HEADS, CHUNK, CHUNK), F32),
            jax.ShapeDtypeStruct((2 * RET_HEADS, LANES), F32),
            jax.ShapeDtypeStruct((2 * RET_HEADS, LANES), F32),
        ],
        compiler_params=pltpu.CompilerParams(dimension_semantics=("arbitrary",),
                                             vmem_limit_bytes=VMEM_LIMIT),
        name="mod",
    )(cc, w_mod, b_mod, dl)


def _inproj_kernel(x_ref, sc_ref, sh_ref, nw_ref, w_ref, qnw_ref, knw_ref,
                   ca_ref, sa_ref, cr_ref, sr_ref,
                   qa_o, kaT_o, vx_o, ga_o, qr_o, krT_o, vr_o, gr_o):
    tm = x_ref.shape[0]
    x = x_ref[...]
    ms = jnp.mean(x * x, axis=-1, keepdims=True)
    a = nw_ref[...] * (1.0 + sc_ref[0])
    hb = ((x * lax.rsqrt(ms + NORM_EPS)) * a + sh_ref[0]).astype(BF16)

    def proj(lo, width):
        return jnp.dot(hb, w_ref[:, lo:lo + width], preferred_element_type=F32)

    lane = lax.broadcasted_iota(jnp.int32, (tm, LANES), 1)
    head_lo = lane < ATT_HEAD_DIM
    half_lo = (lane & (ATT_HEAD_DIM // 2)) == 0
    ca, sa = ca_ref[...], sa_ref[...]
    cr, sr = cr_ref[...], sr_ref[...]
    inv_hd = 1.0 / ATT_HEAD_DIM

    def att_norm_rope(v, w):
        sq = v * v
        s_lo = jnp.sum(jnp.where(head_lo, sq, 0.0), axis=-1, keepdims=True)
        s_hi = jnp.sum(jnp.where(head_lo, 0.0, sq), axis=-1, keepdims=True)
        r = jnp.where(head_lo, lax.rsqrt(s_lo * inv_hd + NORM_EPS),
                      lax.rsqrt(s_hi * inv_hd + NORM_EPS))
        vn = v * r * w
        rot = jnp.where(half_lo, pltpu.roll(vn, LANES - ATT_HEAD_DIM // 2, 1),
                        pltpu.roll(vn, ATT_HEAD_DIM // 2, 1))
        return vn * ca + rot * sa

    def ret_rope(v):
        return v * cr + pltpu.roll(v, RET_HEAD_DIM // 2, 1) * sr

    qnw = qnw_ref[...]
    qa = proj(O_QA, ATT_WIDTH)
    for c in range(ATT_WIDTH // LANES):
        cs = slice(c * LANES, (c + 1) * LANES)
        qa_o[:, cs] = (att_norm_rope(qa[:, cs], qnw) * Q_SCALE_LOG2).astype(BF16)

    ka = att_norm_rope(proj(O_KA, ATT_KV_WIDTH), knw_ref[...])
    kaT_o[0, 0] = ka.T.astype(BF16)

    va = proj(O_VA, ATT_KV_WIDTH)
    ones_col = jnp.where(lane == ATT_HEAD_DIM, 1.0, 0.0)
    vx_o[:, 0:LANES] = jnp.where(head_lo, va, ones_col).astype(BF16)
    vx_o[:, LANES:2 * LANES] = jnp.where(head_lo, pltpu.roll(va, ATT_HEAD_DIM, 1),
                                         ones_col).astype(BF16)

    ga_o[...] = _silu(proj(O_GA, ATT_WIDTH)).astype(BF16)

    qr = proj(O_QR, RET_WIDTH)
    kr = proj(O_KR, RET_WIDTH)
    for h in range(RET_HEADS):
        hs = slice(h * RET_HEAD_DIM, (h + 1) * RET_HEAD_DIM)
        qr_o[:, hs] = ret_rope(qr[:, hs]).astype(BF16)
        krh = ret_rope(kr[:, hs]) * (RET_HEAD_DIM ** -0.5)
        krT_o[0, 0, hs, :] = krh.T.astype(BF16)

    vr_o[...] = proj(O_VR, RET_WIDTH).astype(BF16)
    gr_o[...] = _silu(proj(O_GR, RET_WIDTH)).astype(BF16)


def _run_inproj(x2, scale, shift, norm_w, w_in, qnw, knw, tabs, *, batch, seq, tm):
    nt = seq // tm
    rows = batch * seq
    row_map = lambda t, b: (b * nt + t, 0)
    tab_spec = pl.BlockSpec((tm, LANES), lambda t, b: (t, 0))
    vec_spec = pl.BlockSpec((1, 1, D_MODEL), lambda t, b: (b, 0, 0))

    def row_out(width):
        return pl.BlockSpec((tm, width), row_map)

    return pl.pallas_call(
        _inproj_kernel,
        grid=(nt, batch),
        in_specs=[
            pl.BlockSpec((tm, D_MODEL), row_map),
            vec_spec, vec_spec,
            pl.BlockSpec((1, D_MODEL), lambda t, b: (0, 0)),
            pl.BlockSpec((D_MODEL, IN_WIDTH), lambda t, b: (0, 0)),
            pl.BlockSpec((1, LANES), lambda t, b: (0, 0)),
            pl.BlockSpec((1, LANES), lambda t, b: (0, 0)),
            tab_spec, tab_spec, tab_spec, tab_spec,
        ],
        out_specs=[
            row_out(ATT_WIDTH),
            pl.BlockSpec((1, 1, ATT_KV_WIDTH, tm), lambda t, b: (b, t, 0, 0)),
            row_out(VEXT_W),
            row_out(ATT_WIDTH),
            row_out(RET_WIDTH),
            pl.BlockSpec((1, 1, RET_WIDTH, tm), lambda t, b: (b, t, 0, 0)),
            row_out(RET_WIDTH),
            row_out(RET_WIDTH),
        ],
        out_shape=[
            jax.ShapeDtypeStruct((rows, ATT_WIDTH), BF16),
            jax.ShapeDtypeStruct((batch, nt, ATT_KV_WIDTH, tm), BF16),
            jax.ShapeDtypeStruct((rows, VEXT_W), BF16),
            jax.ShapeDtypeStruct((rows, ATT_WIDTH), BF16),
            jax.ShapeDtypeStruct((rows, RET_WIDTH), BF16),
            jax.ShapeDtypeStruct((batch, nt, RET_WIDTH, tm), BF16),
            jax.ShapeDtypeStruct((rows, RET_WIDTH), BF16),
            jax.ShapeDtypeStruct((rows, RET_WIDTH), BF16),
        ],
        compiler_params=pltpu.CompilerParams(dimension_semantics=("arbitrary", "arbitrary"),
                                             vmem_limit_bytes=VMEM_LIMIT),
        name="inproj",
    )(x2, scale, shift, norm_w, w_in, qnw, knw, *tabs)


def _states_kernel(kTc_ref, vc_ref, kTf_ref, vf_ref, kTb_ref, vb_ref, kdec_ref, cdec_ref,
                   sf_o, sb_o, st):
    kdec = kdec_ref[...]
    cdec = cdec_ref[...]
    nh = RET_HEADS

    def chunk_state(kT_ref, v_ref, c, h, row):
        hs = slice(h * RET_HEAD_DIM, (h + 1) * RET_HEAD_DIM)
        cs = slice(c * CHUNK, (c + 1) * CHUNK)
        kd = (kT_ref[0, 0, hs, cs].astype(F32) * kdec[row:row + 1, :]).astype(BF16)
        return jnp.dot(kd, v_ref[cs, hs], preferred_element_type=F32)

    @pl.when(pl.program_id(1) == 0)
    def _():
        for h in range(nh):
            st[h] = (cdec[h:h + 1, :] * chunk_state(kTc_ref, vc_ref, 0, h, h)
                     + chunk_state(kTc_ref, vc_ref, 1, h, h))
            st[nh + h] = (cdec[nh + h:nh + h + 1, :] * chunk_state(kTc_ref, vc_ref, 1, h, nh + h)
                          + chunk_state(kTc_ref, vc_ref, 0, h, nh + h))

    for c in range(CHUNKS_PER_TILE):
        for h in range(nh):
            s = st[h]
            sf_o[0, c, h] = s.astype(BF16)
            st[h] = cdec[h:h + 1, :] * s + chunk_state(kTf_ref, vf_ref, c, h, h)
    for c in reversed(range(CHUNKS_PER_TILE)):
        for h in range(nh):
            s = st[nh + h]
            sb_o[0, c, h] = s.astype(BF16)
            st[nh + h] = cdec[nh + h:nh + h + 1, :] * s + chunk_state(kTb_ref, vb_ref, c, h, nh + h)


def _run_states(krT_c, vr_c, krT, vr, kdec, cdec, *, batch, seq, ctx_len):
    nt = seq // TILE
    nchunks = seq // CHUNK
    s_spec_f = pl.BlockSpec((1, CHUNKS_PER_TILE, RET_HEADS, RET_HEAD_DIM, RET_HEAD_DIM),
                            lambda b, t: (b, t, 0, 0, 0))
    s_spec_b = pl.BlockSpec((1, CHUNKS_PER_TILE, RET_HEADS, RET_HEAD_DIM, RET_HEAD_DIM),
                            lambda b, t: (b, nt - 1 - t, 0, 0, 0))
    s_shape = jax.ShapeDtypeStruct((batch, nchunks, RET_HEADS, RET_HEAD_DIM, RET_HEAD_DIM), BF16)
    small = pl.BlockSpec((2 * RET_HEADS, LANES), lambda b, t: (0, 0))
    return pl.pallas_call(
        _states_kernel,
        grid=(batch, nt),
        in_specs=[
            pl.BlockSpec((1, 1, RET_WIDTH, ctx_len), lambda b, t: (b, 0, 0, 0)),
            pl.BlockSpec((ctx_len, RET_WIDTH), lambda b, t: (b, 0)),
            pl.BlockSpec((1, 1, RET_WIDTH, TILE), lambda b, t: (b, t, 0, 0)),
            pl.BlockSpec((TILE, RET_WIDTH), lambda b, t: (b * nt + t, 0)),
            pl.BlockSpec((1, 1, RET_WIDTH, TILE), lambda b, t: (b, nt - 1 - t, 0, 0)),
            pl.BlockSpec((TILE, RET_WIDTH), lambda b, t: (b * nt + nt - 1 - t, 0)),
            small, small,
        ],
        out_specs=[s_spec_f, s_spec_b],
        out_shape=[s_shape, s_shape],
        scratch_shapes=[pltpu.VMEM((2 * RET_HEADS, RET_HEAD_DIM, RET_HEAD_DIM), F32)],
        compiler_params=pltpu.CompilerParams(dimension_semantics=("arbitrary", "arbitrary"),
                                             vmem_limit_bytes=VMEM_LIMIT),
        name="states",
    )(krT_c, vr_c, krT, vr, krT, vr, kdec, cdec)


def _attn_kernel(q_ref, kTc_ref, vc_ref, kT_ref, v_ref, ga_ref, o_ref, *, tq, tk, n_kv,
                 bounded, unroll):
    q = q_ref[...]
    qs = jnp.concatenate([q[:, h * ATT_HEAD_DIM:(h + 1) * ATT_HEAD_DIM]
                          for h in range(ATT_GROUP)], axis=0)
    rows = ATT_GROUP * tq

    if bounded:
        def block(acc, kT, v):
            s = jnp.dot(qs, kT, preferred_element_type=F32)
            return acc + jnp.dot(jnp.exp2(s).astype(BF16), v, preferred_element_type=F32)

        carry = jnp.zeros((rows, LANES), F32)
    else:
        def block(carry, kT, v):
            m, acc = carry
            s = jnp.dot(qs, kT, preferred_element_type=F32)
            m_new = jnp.maximum(m, jnp.max(s, axis=-1, keepdims=True))
            p = jnp.exp2(s - m_new).astype(BF16)
            acc = jnp.exp2(m - m_new) * acc + jnp.dot(p, v, preferred_element_type=F32)
            return m_new, acc

        carry = (jnp.full((rows, 1), -1e30, F32), jnp.zeros((rows, LANES), F32))

    carry = block(carry, kTc_ref[0, 0], vc_ref[...])

    def body(j, carry):
        off = pl.multiple_of(j * tk, tk)
        return block(carry, kT_ref[0, j], v_ref[pl.ds(off, tk), :])

    carry = lax.fori_loop(0, n_kv, body, carry, unroll=unroll)
    acc = carry if bounded else carry[1]
    o = acc[:, 0:ATT_HEAD_DIM] * (1.0 / acc[:, ATT_HEAD_DIM:ATT_HEAD_DIM + 1])
    out = jnp.concatenate([o[h * tq:(h + 1) * tq] for h in range(ATT_GROUP)], axis=1)
    o_ref[...] = (out * ga_ref[...].astype(F32)).astype(BF16)


def _run_attn(qa, kaT_c, vx_c, kaT, vx, ga, *, batch, seq, ctx_len, tq, bounded, unroll):
    nq = seq // tq
    n_kv = seq // TILE
    gw = ATT_GROUP * ATT_HEAD_DIM
    kern = functools.partial(_attn_kernel, tq=tq, tk=TILE, n_kv=n_kv, bounded=bounded,
                             unroll=unroll)
    return pl.pallas_call(
        kern,
        grid=(batch, ATT_KV_HEADS, nq),
        in_specs=[
            pl.BlockSpec((tq, gw), lambda b, g, i: (b * nq + i, g)),
            pl.BlockSpec((1, 1, ATT_HEAD_DIM, ctx_len), lambda b, g, i: (b, 0, g, 0)),
            pl.BlockSpec((ctx_len, LANES), lambda b, g, i: (b, g)),
            pl.BlockSpec((1, n_kv, ATT_HEAD_DIM, TILE), lambda b, g, i: (b, 0, g, 0)),
            pl.BlockSpec((seq, LANES), lambda b, g, i: (b, g)),
            pl.BlockSpec((tq, gw), lambda b, g, i: (b * nq + i, g)),
        ],
        out_specs=pl.BlockSpec((tq, gw), lambda b, g, i: (b * nq + i, g)),
        out_shape=jax.ShapeDtypeStruct((batch * seq, ATT_WIDTH), BF16),
        compiler_params=pltpu.CompilerParams(
            dimension_semantics=("arbitrary", "arbitrary", "arbitrary"),
            vmem_limit_bytes=VMEM_LIMIT),
        name="attn_bounded" if bounded else "attn_online",
    )(qa, kaT_c, vx_c, kaT, vx, ga)


def _epilogue_kernel(att_ref, qr_ref, krT_ref, vr_ref, gr_ref, x_ref, sf_ref, sb_ref,
                     dmask_ref, qdec_ref, gnw_ref, gate_ref, wout_ref, fnw_ref, y_ref, ret_scr):
    nh = RET_HEADS
    for c in range(CHUNKS_PER_TILE):
        rs = slice(c * CHUNK, (c + 1) * CHUNK)
        for h in range(nh):
            hs = slice(h * RET_HEAD_DIM, (h + 1) * RET_HEAD_DIM)
            q = qr_ref[rs, hs]
            a = jnp.dot(q, krT_ref[0, 0, hs, rs], preferred_element_type=F32) * dmask_ref[h]
            o = jnp.dot(a.astype(BF16), vr_ref[rs, hs], preferred_element_type=F32)
            o = o + jnp.dot(q, sf_ref[0, c, h], preferred_element_type=F32) * qdec_ref[h]
            o = o + jnp.dot(q, sb_ref[0, c, h], preferred_element_type=F32) * qdec_ref[nh + h]
            mu = jnp.mean(o, axis=-1, keepdims=True)
            d = o - mu
            var = jnp.mean(d * d, axis=-1, keepdims=True)
            yh = d * lax.rsqrt(var + NORM_EPS) * gnw_ref[:, hs] * gr_ref[rs, hs].astype(F32)
            ret_scr[rs, hs] = yh.astype(BF16)
    proj = (jnp.dot(att_ref[...], wout_ref[0:ATT_WIDTH, :], preferred_element_type=F32)
            + jnp.dot(ret_scr[...], wout_ref[ATT_WIDTH:, :], preferred_element_type=F32))
    z = x_ref[...] + gate_ref[0] * proj
    ms = jnp.mean(z * z, axis=-1, keepdims=True)
    y_ref[...] = z * lax.rsqrt(ms + NORM_EPS) * fnw_ref[...]


def _run_epilogue(att, qr, krT, vr, gr, x2, sf, sb, dmask, qdec, gnw, gate, w_out, fnw,
                  *, batch, seq):
    nt = seq // TILE
    row_map = lambda b, t: (b * nt + t, 0)
    s_spec = pl.BlockSpec((1, CHUNKS_PER_TILE, RET_HEADS, RET_HEAD_DIM, RET_HEAD_DIM),
                          lambda b, t: (b, t, 0, 0, 0))
    return pl.pallas_call(
        _epilogue_kernel,
        grid=(batch, nt),
        in_specs=[
            pl.BlockSpec((TILE, ATT_WIDTH), row_map),
            pl.BlockSpec((TILE, RET_WIDTH), row_map),
            pl.BlockSpec((1, 1, RET_WIDTH, TILE), lambda b, t: (b, t, 0, 0)),
            pl.BlockSpec((TILE, RET_WIDTH), row_map),
            pl.BlockSpec((TILE, RET_WIDTH), row_map),
            pl.BlockSpec((TILE, D_MODEL), row_map),
            s_spec, s_spec,
            pl.BlockSpec((RET_HEADS, CHUNK, CHUNK), lambda b, t: (0, 0, 0)),
            pl.BlockSpec((2 * RET_HEADS, CHUNK, CHUNK), lambda b, t: (0, 0, 0)),
            pl.BlockSpec((1, RET_WIDTH), lambda b, t: (0, 0)),
            pl.BlockSpec((1, 1, D_MODEL), lambda b, t: (b, 0, 0)),
            pl.BlockSpec((ATT_WIDTH + RET_WIDTH, D_MODEL), lambda b, t: (0, 0)),
            pl.BlockSpec((1, D_MODEL), lambda b, t: (0, 0)),
        ],
        out_specs=pl.BlockSpec((TILE, D_MODEL), row_map),
        out_shape=jax.ShapeDtypeStruct((batch * seq, D_MODEL), F32),
        scratch_shapes=[pltpu.VMEM((TILE, RET_WIDTH), BF16)],
        compiler_params=pltpu.CompilerParams(dimension_semantics=("arbitrary", "arbitrary"),
                                             vmem_limit_bytes=VMEM_LIMIT),
        name="epilogue",
    )(att, qr, krT, vr, gr, x2, sf, sb, dmask, qdec, gnw, gate, w_out, fnw)


def _rope_tables(seq):
    n_rows = seq // GRID_W
    rows, cols = jnp.meshgrid(jnp.arange(n_rows, dtype=F32), jnp.arange(GRID_W, dtype=F32),
                              indexing="ij")
    rows = rows.reshape(-1)
    cols = cols.reshape(-1)

    def angles(head_dim):
        n_axis = head_dim // 4
        inv_freq = ROPE_THETA ** (-jnp.arange(n_axis, dtype=F32) / n_axis)
        return jnp.concatenate([rows[:, None] * inv_freq, cols[:, None] * inv_freq], axis=-1)

    ang_a = angles(ATT_HEAD_DIM)
    ang_r = angles(RET_HEAD_DIM)
    cos_a = jnp.tile(jnp.cos(ang_a), (1, LANES // (ATT_HEAD_DIM // 2)))
    sin_a = jnp.tile(jnp.concatenate([-jnp.sin(ang_a), jnp.sin(ang_a)], axis=-1),
                     (1, LANES // ATT_HEAD_DIM))
    cos_r = jnp.tile(jnp.cos(ang_r), (1, 2))
    sin_r = jnp.concatenate([-jnp.sin(ang_r), jnp.sin(ang_r)], axis=-1)
    return cos_a, sin_a, cos_r, sin_r


def kernel(x, c, ctx, c_ctx, norm_w, w_mod, b_mod, w_in, q_norm_w, k_norm_w,
           ret_decay_fwd, ret_decay_bwd, ret_gn_w, w_out, final_norm_w):
    batch, seq, d = x.shape
    ctx_len = ctx.shape[1]
    depth = norm_w.shape[0]
    assert d == D_MODEL and depth == 1 and seq % TILE == 0 and ctx_len == 2 * CHUNK
    assert batch + 1 <= MOD_ROWS and w_in.shape[-1] == IN_WIDTH
    layer = 0

    cc = jnp.zeros((MOD_ROWS, D_MODEL), F32).at[:batch].set(c).at[batch].set(c_ctx)
    dl = jnp.broadcast_to(
        jnp.concatenate([ret_decay_fwd[layer], ret_decay_bwd[layer]]).astype(F32)[:, None],
        (2 * RET_HEADS, LANES))
    mod, dmask, qdec, kdec, cdec = _run_mod(cc, w_mod[layer], b_mod[layer].reshape(3, 1, D_MODEL), dl)
    shift, scale, gate = mod[0], mod[1], mod[2]

    w_in_b = w_in[layer].astype(BF16)
    nw = norm_w[layer].reshape(1, D_MODEL)
    qnw = jnp.tile(q_norm_w[layer], LANES // ATT_HEAD_DIM).reshape(1, LANES)
    knw = jnp.tile(k_norm_w[layer], LANES // ATT_HEAD_DIM).reshape(1, LANES)

    ones = jnp.ones((ctx_len, LANES), F32)
    zeros = jnp.zeros((ctx_len, LANES), F32)
    sc_c = jnp.broadcast_to(scale[batch].reshape(1, 1, D_MODEL), (batch, 1, D_MODEL))
    sh_c = jnp.broadcast_to(shift[batch].reshape(1, 1, D_MODEL), (batch, 1, D_MODEL))
    (_, kaT_c, vx_c, _, _, krT_c, vr_c, _) = _run_inproj(
        ctx.reshape(batch * ctx_len, d), sc_c, sh_c, nw, w_in_b, qnw, knw,
        (ones, zeros, ones, zeros), batch=batch, seq=ctx_len, tm=ctx_len)

    x2 = x.reshape(batch * seq, d)
    sc_l = scale[:batch].reshape(batch, 1, D_MODEL)
    sh_l = shift[:batch].reshape(batch, 1, D_MODEL)
    qa, kaT, vx, ga, qr, krT, vr, gr = _run_inproj(
        x2, sc_l, sh_l, nw, w_in_b, qnw, knw, _rope_tables(seq), batch=batch, seq=seq, tm=TILE)

    sf, sb = _run_states(krT_c, vr_c, krT, vr, kdec, cdec, batch=batch, seq=seq, ctx_len=ctx_len)
    score_bound = (ATT_HEAD_DIM * Q_SCALE_LOG2 * BF16_ROUND_UP ** 2
                   * jnp.max(jnp.abs(q_norm_w[layer])) * jnp.max(jnp.abs(k_norm_w[layer])))
    attn = functools.partial(_run_attn, batch=batch, seq=seq, ctx_len=ctx_len, tq=256)
    att = lax.cond(score_bound <= SCORE_LOG2_LIMIT,
                   functools.partial(attn, bounded=True, unroll=16),
                   functools.partial(attn, bounded=False, unroll=1),
                   qa, kaT_c, vx_c, kaT, vx, ga)
    y = _run_epilogue(att, qr, krT, vr, gr, x2, sf, sb, dmask, qdec,
                      ret_gn_w[layer].reshape(1, RET_WIDTH),
                      gate[:batch].reshape(batch, 1, D_MODEL),
                      w_out[layer].astype(BF16), final_norm_w.reshape(1, D_MODEL),
                      batch=batch, seq=seq)
    return y.reshape(batch, seq, d)
```

```python
import functools

import jax
import jax.numpy as jnp
from jax import lax
from jax.experimental import pallas as pl
from jax.experimental.pallas import tpu as pltpu

F32 = jnp.float32
BF16 = jnp.bfloat16

D_MODEL = 1024
GRID_W = 64
ATT_HEADS = 8
ATT_KV_HEADS = 2
ATT_GROUP = ATT_HEADS // ATT_KV_HEADS
ATT_HEAD_DIM = 64
ATT_WIDTH = ATT_HEADS * ATT_HEAD_DIM
ATT_KV_WIDTH = ATT_KV_HEADS * ATT_HEAD_DIM
RET_HEADS = 4
RET_HEAD_DIM = 128
RET_WIDTH = RET_HEADS * RET_HEAD_DIM
ROPE_THETA = 10000.0
NORM_EPS = 1e-6
LANES = 128
TILE = 512
MOD_ROWS = 8
VEXT_W = 2 * LANES
N_DIR = 2 * RET_HEADS

O_QA = 0
O_KA = O_QA + ATT_WIDTH
O_VA = O_KA + ATT_KV_WIDTH
O_GA = O_VA + ATT_KV_WIDTH
O_QR = O_GA + ATT_WIDTH
O_KR = O_QR + RET_WIDTH
O_VR = O_KR + RET_WIDTH
O_GR = O_VR + RET_WIDTH
IN_WIDTH = O_GR + RET_WIDTH

VMEM_LIMIT = 48 * 1024 * 1024

LOG2_E = 1.4426950408889634
Q_SCALE_LOG2 = ATT_HEAD_DIM ** -0.5 * LOG2_E
SCORE_LOG2_LIMIT = 100.0
BF16_ROUND_UP = 1.0 + 2.0 ** -8


def _silu(x):
    return x * (1.0 / (1.0 + jnp.exp(-x)))


def _silu_tanh(x):
    hx = 0.5 * x
    return hx + hx * jnp.tanh(hx)


def _mod_kernel(cc_ref, w_ref, b_ref, dl_ref,
                mod_ref, dmask_ref, qdec_ref, kdec_ref, kdec_c_ref, cdec_ref):
    sc = _silu(cc_ref[...])
    mod_ref[0] = jnp.dot(sc, w_ref[...], preferred_element_type=F32,
                         precision=lax.Precision.HIGHEST) + b_ref[0]

    @pl.when(pl.program_id(0) == 0)
    def _():
        x = dl_ref[...]
        lg = jnp.minimum(x, 0.0) - jnp.log1p(jnp.exp(-jnp.abs(x)))
        n = float(TILE)
        ii = lax.broadcasted_iota(jnp.int32, (TILE, TILE), 0).astype(F32)
        jj = lax.broadcasted_iota(jnp.int32, (TILE, TILE), 1).astype(F32)
        d = ii - jj
        i_col = lax.broadcasted_iota(jnp.int32, (TILE, LANES), 0).astype(F32)
        for h in range(RET_HEADS):
            lf = lg[h:h + 1, :]
            lb = lg[RET_HEADS + h:RET_HEADS + h + 1, :]
            lf_t = jnp.concatenate([lf] * (TILE // LANES), axis=1)
            lb_t = jnp.concatenate([lb] * (TILE // LANES), axis=1)
            fwd = jnp.exp(lf_t * jnp.maximum(d, 0.0))
            bwd = jnp.exp(lb_t * jnp.maximum(-d, 0.0))
            dmask_ref[h] = jnp.where(d > 0, fwd, jnp.where(d < 0, bwd, 2.0))
            qdec_ref[h] = jnp.exp(lf * (i_col + 1.0))
            qdec_ref[RET_HEADS + h] = jnp.exp(lb * (n - i_col))
        for ref in (kdec_ref, kdec_c_ref):
            length = ref.shape[1]
            tok = lax.broadcasted_iota(jnp.int32, (N_DIR, length), 1).astype(F32)
            row = lax.broadcasted_iota(jnp.int32, (N_DIR, length), 0)
            lg_t = jnp.concatenate([lg] * (length // LANES), axis=1)
            ref[...] = jnp.where(row < RET_HEADS, jnp.exp(lg_t * (float(length) - 1.0 - tok)),
                                 jnp.exp(lg_t * tok))
        cdec_ref[...] = jnp.exp(lg * n)


def _run_mod(cc, w_mod, b_mod, dl, *, ctx_len):
    n3 = 3
    const2 = lambda j: (0, 0)
    const3 = lambda j: (0, 0, 0)
    return pl.pallas_call(
        _mod_kernel,
        grid=(n3,),
        in_specs=[
            pl.BlockSpec((MOD_ROWS, D_MODEL), const2),
            pl.BlockSpec((D_MODEL, D_MODEL), lambda j: (0, j)),
            pl.BlockSpec((1, 1, D_MODEL), lambda j: (j, 0, 0)),
            pl.BlockSpec((N_DIR, LANES), const2),
        ],
        out_specs=[
            pl.BlockSpec((1, MOD_ROWS, D_MODEL), lambda j: (j, 0, 0)),
            pl.BlockSpec((RET_HEADS, TILE, TILE), const3),
            pl.BlockSpec((N_DIR, TILE, LANES), const3),
            pl.BlockSpec((N_DIR, TILE), const2),
            pl.BlockSpec((N_DIR, ctx_len), const2),
            pl.BlockSpec((N_DIR, LANES), const2),
        ],
        out_shape=[
            jax.ShapeDtypeStruct((n3, MOD_ROWS, D_MODEL), F32),
            jax.ShapeDtypeStruct((RET_HEADS, TILE, TILE), F32),
            jax.ShapeDtypeStruct((N_DIR, TILE, LANES), F32),
            jax.ShapeDtypeStruct((N_DIR, TILE), F32),
            jax.ShapeDtypeStruct((N_DIR, ctx_len), F32),
            jax.ShapeDtypeStruct((N_DIR, LANES), F32),
        ],
        compiler_params=pltpu.CompilerParams(dimension_semantics=("arbitrary",),
                                             vmem_limit_bytes=VMEM_LIMIT),
        name="mod",
    )(cc, w_mod, b_mod, dl)


def _inproj_kernel(x_ref, sc_ref, sh_ref, nw_ref, w_ref, qnw_ref, knw_ref, rt_ref, ct_ref, kdec_ref,
                   qa_o, kaT_o, vx_o, ga_o, qr_o, krT_o, vr_o, gr_o, u_o):
    tm = x_ref.shape[0]
    x = x_ref[...]
    ms = jnp.mean(x * x, axis=-1, keepdims=True)
    a = nw_ref[...] * (1.0 + sc_ref[0])
    hb = ((x * lax.rsqrt(ms + NORM_EPS)) * a + sh_ref[0]).astype(BF16)

    def proj(lo, width):
        return jnp.dot(hb, w_ref[:, lo:lo + width], preferred_element_type=F32)

    def rope_table(k):
        rt = rt_ref[k]
        rows = [jnp.broadcast_to(rt[i:i + 1, :], (GRID_W, LANES)) for i in range(tm // GRID_W)]
        return jnp.concatenate(rows, axis=0) + ct_ref[k]

    ca, sa, cr, sr = (rope_table(k) for k in range(4))
    lane = lax.broadcasted_iota(jnp.int32, (tm, LANES), 1)
    head_lo = lane < ATT_HEAD_DIM
    half_lo = (lane & (ATT_HEAD_DIM // 2)) == 0
    inv_hd = 1.0 / ATT_HEAD_DIM

    def att_norm_rope(v, w):
        sq = v * v
        s_lo = jnp.sum(jnp.where(head_lo, sq, 0.0), axis=-1, keepdims=True)
        s_hi = jnp.sum(jnp.where(head_lo, 0.0, sq), axis=-1, keepdims=True)
        r = jnp.where(head_lo, lax.rsqrt(s_lo * inv_hd + NORM_EPS),
                      lax.rsqrt(s_hi * inv_hd + NORM_EPS))
        vn = v * r * w
        rot = jnp.where(half_lo, pltpu.roll(vn, LANES - ATT_HEAD_DIM // 2, 1),
                        pltpu.roll(vn, ATT_HEAD_DIM // 2, 1))
        return vn * ca + rot * sa

    def ret_rope(v):
        return v * cr + pltpu.roll(v, RET_HEAD_DIM // 2, 1) * sr

    qnw = qnw_ref[...]
    qa = proj(O_QA, ATT_WIDTH)
    for c in range(ATT_WIDTH // LANES):
        cs = slice(c * LANES, (c + 1) * LANES)
        qa_o[:, cs] = (att_norm_rope(qa[:, cs], qnw) * Q_SCALE_LOG2).astype(BF16)

    ka = att_norm_rope(proj(O_KA, ATT_KV_WIDTH), knw_ref[...])
    kaT_o[0, 0] = ka.T.astype(BF16)

    va = proj(O_VA, ATT_KV_WIDTH)
    ones_col = jnp.where(lane == ATT_HEAD_DIM, 1.0, 0.0)
    vx_o[:, 0:LANES] = jnp.where(head_lo, va, ones_col).astype(BF16)
    vx_o[:, LANES:2 * LANES] = jnp.where(head_lo, pltpu.roll(va, ATT_HEAD_DIM, 1),
                                         ones_col).astype(BF16)

    ga_o[...] = _silu_tanh(proj(O_GA, ATT_WIDTH)).astype(BF16)

    vr = proj(O_VR, RET_WIDTH).astype(BF16)
    vr_o[...] = vr
    qr = proj(O_QR, RET_WIDTH)
    kr = proj(O_KR, RET_WIDTH)
    kdec = kdec_ref[...]
    for h in range(RET_HEADS):
        hs = slice(h * RET_HEAD_DIM, (h + 1) * RET_HEAD_DIM)
        qr_o[:, hs] = ret_rope(qr[:, hs]).astype(BF16)
        krT = (ret_rope(kr[:, hs]) * (RET_HEAD_DIM ** -0.5)).T
        krT_o[0, 0, hs, :] = krT.astype(BF16)
        for row in (h, RET_HEADS + h):
            kd = (krT * kdec[row:row + 1, :]).astype(BF16)
            u_o[0, 0, row] = jnp.dot(kd, vr[:, hs], preferred_element_type=F32)

    gr_o[...] = _silu_tanh(proj(O_GR, RET_WIDTH)).astype(BF16)


def _run_inproj(x2, scale, shift, norm_w, w_in, qnw, knw, row_tab, col_tab, kdec,
                *, batch, seq, tm):
    nt = seq // tm
    rows = batch * seq
    row_map = lambda t, b: (b * nt + t, 0)
    const2 = lambda t, b: (0, 0)
    vec_spec = pl.BlockSpec((1, 1, D_MODEL), lambda t, b: (b, 0, 0))
    t_map = lambda t, b: (b, t, 0, 0)

    def row_out(width):
        return pl.BlockSpec((tm, width), row_map)

    return pl.pallas_call(
        _inproj_kernel,
        grid=(nt, batch),
        in_specs=[
            pl.BlockSpec((tm, D_MODEL), row_map),
            vec_spec, vec_spec,
            pl.BlockSpec((1, D_MODEL), const2),
            pl.BlockSpec((D_MODEL, IN_WIDTH), const2),
            pl.BlockSpec((1, LANES), const2),
            pl.BlockSpec((1, LANES), const2),
            pl.BlockSpec((4, tm // GRID_W, LANES), lambda t, b: (0, t, 0)),
            pl.BlockSpec((4, tm, LANES), lambda t, b: (0, 0, 0)),
            pl.BlockSpec((N_DIR, tm), const2),
        ],
        out_specs=[
            row_out(ATT_WIDTH),
            pl.BlockSpec((1, 1, ATT_KV_WIDTH, tm), t_map),
            row_out(VEXT_W),
            row_out(ATT_WIDTH),
            row_out(RET_WIDTH),
            pl.BlockSpec((1, 1, RET_WIDTH, tm), t_map),
            row_out(RET_WIDTH),
            row_out(RET_WIDTH),
            pl.BlockSpec((1, 1, N_DIR, RET_HEAD_DIM, RET_HEAD_DIM), lambda t, b: (b, t, 0, 0, 0)),
        ],
        out_shape=[
            jax.ShapeDtypeStruct((rows, ATT_WIDTH), BF16),
            jax.ShapeDtypeStruct((batch, nt, ATT_KV_WIDTH, tm), BF16),
            jax.ShapeDtypeStruct((rows, VEXT_W), BF16),
            jax.ShapeDtypeStruct((rows, ATT_WIDTH), BF16),
            jax.ShapeDtypeStruct((rows, RET_WIDTH), BF16),
            jax.ShapeDtypeStruct((batch, nt, RET_WIDTH, tm), BF16),
            jax.ShapeDtypeStruct((rows, RET_WIDTH), BF16),
            jax.ShapeDtypeStruct((rows, RET_WIDTH), BF16),
            jax.ShapeDtypeStruct((batch, nt, N_DIR, RET_HEAD_DIM, RET_HEAD_DIM), F32),
        ],
        compiler_params=pltpu.CompilerParams(dimension_semantics=("arbitrary", "arbitrary"),
                                             vmem_limit_bytes=VMEM_LIMIT),
        name="inproj",
    )(x2, scale, shift, norm_w, w_in, qnw, knw, row_tab, col_tab, kdec)


def _scan_kernel(uc_ref, uf_ref, ub_ref, cdec_ref, sf_o, sb_o, st):
    nh = RET_HEADS
    cdec = cdec_ref[...]

    @pl.when(pl.program_id(1) == 0)
    def _():
        st[...] = uc_ref[0, 0]

    for h in range(nh):
        s = st[h]
        sf_o[0, 0, h] = s.astype(BF16)
        st[h] = cdec[h:h + 1, :] * s + uf_ref[0, 0, h]
        s = st[nh + h]
        sb_o[0, 0, h] = s.astype(BF16)
        st[nh + h] = cdec[nh + h:nh + h + 1, :] * s + ub_ref[0, 0, h]


def _run_scan(u_c, u, cdec, *, batch, seq):
    nt = seq // TILE
    hd = RET_HEAD_DIM
    blk = (1, 1, RET_HEADS, hd, hd)
    s_shape = jax.ShapeDtypeStruct((batch, nt, RET_HEADS, hd, hd), BF16)
    return pl.pallas_call(
        _scan_kernel,
        grid=(batch, nt),
        in_specs=[
            pl.BlockSpec((1, 1, N_DIR, hd, hd), lambda b, t: (b, 0, 0, 0, 0)),
            pl.BlockSpec(blk, lambda b, t: (b, t, 0, 0, 0)),
            pl.BlockSpec(blk, lambda b, t: (b, nt - 1 - t, 1, 0, 0)),
            pl.BlockSpec((N_DIR, LANES), lambda b, t: (0, 0)),
        ],
        out_specs=[pl.BlockSpec(blk, lambda b, t: (b, t, 0, 0, 0)),
                   pl.BlockSpec(blk, lambda b, t: (b, nt - 1 - t, 0, 0, 0))],
        out_shape=[s_shape, s_shape],
        scratch_shapes=[pltpu.VMEM((N_DIR, hd, hd), F32)],
        compiler_params=pltpu.CompilerParams(dimension_semantics=("arbitrary", "arbitrary"),
                                             vmem_limit_bytes=VMEM_LIMIT),
        name="scan",
    )(u_c, u, u, cdec)


def _attn_kernel(q_ref, kTc_ref, vc_ref, kT_ref, v_ref, ga_ref, o_ref, *, tq, tk, n_kv,
                 bounded, unroll):
    q = q_ref[...]
    qs = jnp.concatenate([q[:, h * ATT_HEAD_DIM:(h + 1) * ATT_HEAD_DIM]
                          for h in range(ATT_GROUP)], axis=0)
    rows = ATT_GROUP * tq

    if bounded:
        def block(acc, kT, v):
            s = jnp.dot(qs, kT, preferred_element_type=F32)
            return acc + jnp.dot(jnp.exp2(s).astype(BF16), v, preferred_element_type=F32)

        carry = jnp.zeros((rows, LANES), F32)
    else:
        def block(carry, kT, v):
            m, acc = carry
            s = jnp.dot(qs, kT, preferred_element_type=F32)
            m_new = jnp.maximum(m, jnp.max(s, axis=-1, keepdims=True))
            p = jnp.exp2(s - m_new).astype(BF16)
            acc = jnp.exp2(m - m_new) * acc + jnp.dot(p, v, preferred_element_type=F32)
            return m_new, acc

        carry = (jnp.full((rows, 1), -1e30, F32), jnp.zeros((rows, LANES), F32))

    carry = block(carry, kTc_ref[0, 0], vc_ref[...])

    def body(j, carry):
        off = pl.multiple_of(j * tk, tk)
        return block(carry, kT_ref[0, j], v_ref[pl.ds(off, tk), :])

    carry = lax.fori_loop(0, n_kv, body, carry, unroll=unroll)
    acc = carry if bounded else carry[1]
    o = acc[:, 0:ATT_HEAD_DIM] * (1.0 / acc[:, ATT_HEAD_DIM:ATT_HEAD_DIM + 1])
    out = jnp.concatenate([o[h * tq:(h + 1) * tq] for h in range(ATT_GROUP)], axis=1)
    o_ref[...] = (out * ga_ref[...].astype(F32)).astype(BF16)


def _run_attn(qa, kaT_c, vx_c, kaT, vx, ga, *, batch, seq, ctx_len, tq, bounded, unroll):
    nq = seq // tq
    n_kv = seq // TILE
    gw = ATT_GROUP * ATT_HEAD_DIM
    kern = functools.partial(_attn_kernel, tq=tq, tk=TILE, n_kv=n_kv, bounded=bounded,
                             unroll=unroll)
    return pl.pallas_call(
        kern,
        grid=(batch, ATT_KV_HEADS, nq),
        in_specs=[
            pl.BlockSpec((tq, gw), lambda b, g, i: (b * nq + i, g)),
            pl.BlockSpec((1, 1, ATT_HEAD_DIM, ctx_len), lambda b, g, i: (b, 0, g, 0)),
            pl.BlockSpec((ctx_len, LANES), lambda b, g, i: (b, g)),
            pl.BlockSpec((1, n_kv, ATT_HEAD_DIM, TILE), lambda b, g, i: (b, 0, g, 0)),
            pl.BlockSpec((seq, LANES), lambda b, g, i: (b, g)),
            pl.BlockSpec((tq, gw), lambda b, g, i: (b * nq + i, g)),
        ],
        out_specs=pl.BlockSpec((tq, gw), lambda b, g, i: (b * nq + i, g)),
        out_shape=jax.ShapeDtypeStruct((batch * seq, ATT_WIDTH), BF16),
        compiler_params=pltpu.CompilerParams(
            dimension_semantics=("arbitrary", "arbitrary", "arbitrary"),
            vmem_limit_bytes=VMEM_LIMIT),
        name="attn_bounded" if bounded else "attn_online",
    )(qa, kaT_c, vx_c, kaT, vx, ga)


def _epilogue_kernel(att_ref, qr_ref, krT_ref, vr_ref, gr_ref, x_ref, sf_ref, sb_ref,
                     dmask_ref, qdec_ref, gnw_ref, gate_ref, wout_ref, fnw_ref, y_ref, ret_scr):
    nh = RET_HEADS
    for h in range(nh):
        hs = slice(h * RET_HEAD_DIM, (h + 1) * RET_HEAD_DIM)
        q = qr_ref[:, hs]
        qf = q.astype(F32)
        a = jnp.dot(q, krT_ref[0, 0, hs, :], preferred_element_type=F32) * dmask_ref[h]
        o = (jnp.dot(a.astype(BF16), vr_ref[:, hs], preferred_element_type=F32)
             + jnp.dot((qf * qdec_ref[h]).astype(BF16), sf_ref[0, 0, h],
                       preferred_element_type=F32)
             + jnp.dot((qf * qdec_ref[nh + h]).astype(BF16), sb_ref[0, 0, h],
                       preferred_element_type=F32))
        mu = jnp.mean(o, axis=-1, keepdims=True)
        d = o - mu
        var = jnp.mean(d * d, axis=-1, keepdims=True)
        yh = d * lax.rsqrt(var + NORM_EPS) * gnw_ref[:, hs] * gr_ref[:, hs].astype(F32)
        ret_scr[:, hs] = yh.astype(BF16)
    proj = (jnp.dot(att_ref[...], wout_ref[0:ATT_WIDTH, :], preferred_element_type=F32)
            + jnp.dot(ret_scr[...], wout_ref[ATT_WIDTH:, :], preferred_element_type=F32))
    z = x_ref[...] + gate_ref[0] * proj
    ms = jnp.mean(z * z, axis=-1, keepdims=True)
    y_ref[...] = z * lax.rsqrt(ms + NORM_EPS) * fnw_ref[...]


def _run_epilogue(att, qr, krT, vr, gr, x2, sf, sb, dmask, qdec, gnw, gate, w_out, fnw,
                  *, batch, seq):
    nt = seq // TILE
    row_map = lambda b, t: (b * nt + t, 0)
    const2 = lambda b, t: (0, 0)
    const3 = lambda b, t: (0, 0, 0)
    s_spec = pl.BlockSpec((1, 1, RET_HEADS, RET_HEAD_DIM, RET_HEAD_DIM),
                          lambda b, t: (b, t, 0, 0, 0))
    return pl.pallas_call(
        _epilogue_kernel,
        grid=(batch, nt),
        in_specs=[
            pl.BlockSpec((TILE, ATT_WIDTH), row_map),
            pl.BlockSpec((TILE, RET_WIDTH), row_map),
            pl.BlockSpec((1, 1, RET_WIDTH, TILE), lambda b, t: (b, t, 0, 0)),
            pl.BlockSpec((TILE, RET_WIDTH), row_map),
            pl.BlockSpec((TILE, RET_WIDTH), row_map),
            pl.BlockSpec((TILE, D_MODEL), row_map),
            s_spec, s_spec,
            pl.BlockSpec((RET_HEADS, TILE, TILE), const3),
            pl.BlockSpec((N_DIR, TILE, LANES), const3),
            pl.BlockSpec((1, RET_WIDTH), const2),
            pl.BlockSpec((1, 1, D_MODEL), lambda b, t: (b, 0, 0)),
            pl.BlockSpec((ATT_WIDTH + RET_WIDTH, D_MODEL), const2),
            pl.BlockSpec((1, D_MODEL), const2),
        ],
        out_specs=pl.BlockSpec((TILE, D_MODEL), row_map),
        out_shape=jax.ShapeDtypeStruct((batch * seq, D_MODEL), F32),
        scratch_shapes=[pltpu.VMEM((TILE, RET_WIDTH), BF16)],
        compiler_params=pltpu.CompilerParams(dimension_semantics=("arbitrary", "arbitrary"),
                                             vmem_limit_bytes=VMEM_LIMIT),
        name="epilogue",
    )(att, qr, krT, vr, gr, x2, sf, sb, dmask, qdec, gnw, gate, w_out, fnw)


def _rope_tables(seq):
    rows = jnp.arange(seq // GRID_W, dtype=F32)
    cols = jnp.arange(GRID_W, dtype=F32)

    def parts(head_dim):
        n_axis = head_dim // 4
        inv_freq = ROPE_THETA ** (-jnp.arange(n_axis, dtype=F32) / n_axis)
        reps = LANES // head_dim

        def lanes(pos, first):
            ang = pos[:, None] * inv_freq
            z = jnp.zeros_like(ang)
            c = jnp.concatenate([jnp.cos(ang), z] if first else [z, jnp.cos(ang)], axis=-1)
            s = jnp.concatenate([jnp.sin(ang), z] if first else [z, jnp.sin(ang)], axis=-1)
            return (jnp.tile(jnp.concatenate([c, c], axis=-1), (1, reps)),
                    jnp.tile(jnp.concatenate([-s, s], axis=-1), (1, reps)))

        return lanes(rows, True), lanes(cols, False)

    (rca, rsa), (cca, csa) = parts(ATT_HEAD_DIM)
    (rcr, rsr), (ccr, csr) = parts(RET_HEAD_DIM)
    return jnp.stack([rca, rsa, rcr, rsr]), jnp.stack([cca, csa, ccr, csr])


def kernel(x, c, ctx, c_ctx, norm_w, w_mod, b_mod, w_in, q_norm_w, k_norm_w,
           ret_decay_fwd, ret_decay_bwd, ret_gn_w, w_out, final_norm_w):
    batch, seq, d = x.shape
    ctx_len = ctx.shape[1]
    depth = norm_w.shape[0]
    assert d == D_MODEL and depth == 1 and seq % TILE == 0
    assert ctx_len % LANES == 0 and ctx_len <= TILE
    assert batch + 1 <= MOD_ROWS and w_in.shape[-1] == IN_WIDTH
    layer = 0

    cc = jnp.zeros((MOD_ROWS, D_MODEL), F32).at[:batch].set(c).at[batch].set(c_ctx)
    dl = jnp.broadcast_to(
        jnp.concatenate([ret_decay_fwd[layer], ret_decay_bwd[layer]]).astype(F32)[:, None],
        (N_DIR, LANES))
    mod, dmask, qdec, kdec, kdec_c, cdec = _run_mod(
        cc, w_mod[layer], b_mod[layer].reshape(3, 1, D_MODEL), dl, ctx_len=ctx_len)
    shift, scale, gate = mod[0], mod[1], mod[2]

    w_in_b = w_in[layer].astype(BF16)
    nw = norm_w[layer].reshape(1, D_MODEL)
    qnw = jnp.tile(q_norm_w[layer], LANES // ATT_HEAD_DIM).reshape(1, LANES)
    knw = jnp.tile(k_norm_w[layer], LANES // ATT_HEAD_DIM).reshape(1, LANES)

    ctx_rows = ctx_len // GRID_W
    one = jnp.ones((ctx_rows, LANES), F32)
    zero = jnp.zeros((ctx_rows, LANES), F32)
    row_tab_c = jnp.stack([one, zero, one, zero])
    col_tab_c = jnp.zeros((4, ctx_len, LANES), F32)
    sc_c = jnp.broadcast_to(scale[batch].reshape(1, 1, D_MODEL), (batch, 1, D_MODEL))
    sh_c = jnp.broadcast_to(shift[batch].reshape(1, 1, D_MODEL), (batch, 1, D_MODEL))
    (_, kaT_c, vx_c, _, _, _, _, _, u_c) = _run_inproj(
        ctx.reshape(batch * ctx_len, d), sc_c, sh_c, nw, w_in_b, qnw, knw,
        row_tab_c, col_tab_c, kdec_c, batch=batch, seq=ctx_len, tm=ctx_len)

    x2 = x.reshape(batch * seq, d)
    sc_l = scale[:batch].reshape(batch, 1, D_MODEL)
    sh_l = shift[:batch].reshape(batch, 1, D_MODEL)
    row_tab, col_tab = _rope_tables(seq)
    col_tab = jnp.tile(col_tab, (1, TILE // GRID_W, 1))
    qa, kaT, vx, ga, qr, krT, vr, gr, u = _run_inproj(
        x2, sc_l, sh_l, nw, w_in_b, qnw, knw, row_tab, col_tab, kdec,
        batch=batch, seq=seq, tm=TILE)

    sf, sb = _run_scan(u_c, u, cdec, batch=batch, seq=seq)
    score_bound = (ATT_HEAD_DIM * Q_SCALE_LOG2 * BF16_ROUND_UP ** 2
                   * jnp.max(jnp.abs(q_norm_w[layer])) * jnp.max(jnp.abs(k_norm_w[layer])))
    attn = functools.partial(_run_attn, batch=batch, seq=seq, ctx_len=ctx_len, tq=512)
    att = lax.cond(score_bound <= SCORE_LOG2_LIMIT,
                   functools.partial(attn, bounded=True, unroll=seq // TILE),
                   functools.partial(attn, bounded=False, unroll=1),
                   qa, kaT_c, vx_c, kaT, vx, ga)
    y = _run_epilogue(att, qr, krT, vr, gr, x2, sf, sb, dmask, qdec,
                      ret_gn_w[layer].reshape(1, RET_WIDTH),
                      gate[:batch].reshape(batch, 1, D_MODEL),
                      w_out[layer].astype(BF16), final_norm_w.reshape(1, D_MODEL),
                      batch=batch, seq=seq)
    return y.reshape(batch, seq, d)
```

```python
import functools

import jax
import jax.numpy as jnp
from jax import lax
from jax.experimental import pallas as pl
from jax.experimental.pallas import tpu as pltpu

F32 = jnp.float32
BF16 = jnp.bfloat16

D_MODEL = 1024
GRID_W = 64
ATT_HEADS = 8
ATT_KV_HEADS = 2
ATT_GROUP = ATT_HEADS // ATT_KV_HEADS
ATT_HEAD_DIM = 64
ATT_WIDTH = ATT_HEADS * ATT_HEAD_DIM
ATT_KV_WIDTH = ATT_KV_HEADS * ATT_HEAD_DIM
RET_HEADS = 4
RET_HEAD_DIM = 128
RET_WIDTH = RET_HEADS * RET_HEAD_DIM
ROPE_THETA = 10000.0
NORM_EPS = 1e-6
LANES = 128
TILE = 512
MOD_ROWS = 8
VT_ROWS = 80
N_DIR = 2 * RET_HEADS

O_QA = 0
O_KA = O_QA + ATT_WIDTH
O_VA = O_KA + ATT_KV_WIDTH
O_GA = O_VA + ATT_KV_WIDTH
O_QR = O_GA + ATT_WIDTH
O_KR = O_QR + RET_WIDTH
O_VR = O_KR + RET_WIDTH
O_GR = O_VR + RET_WIDTH
IN_WIDTH = O_GR + RET_WIDTH

VMEM_LIMIT = 48 * 1024 * 1024

LOG2_E = 1.4426950408889634
Q_SCALE_LOG2 = ATT_HEAD_DIM ** -0.5 * LOG2_E
SCORE_LOG2_LIMIT = 100.0
BF16_ROUND_UP = 1.0 + 2.0 ** -8


def _silu(x):
    return x * (1.0 / (1.0 + jnp.exp(-x)))


def _silu_tanh(x):
    hx = 0.5 * x
    return hx + hx * jnp.tanh(hx)


def _mod_kernel(cc_ref, w_ref, b_ref, dl_ref,
                mod_ref, dmask_ref, qdec_ref, kdec_ref, kdec_c_ref, cdec_ref):
    sc = _silu(cc_ref[...])
    mod_ref[0] = jnp.dot(sc, w_ref[...], preferred_element_type=F32,
                         precision=lax.Precision.HIGHEST) + b_ref[0]

    @pl.when(pl.program_id(0) == 0)
    def _():
        x = dl_ref[...]
        lg = jnp.minimum(x, 0.0) - jnp.log1p(jnp.exp(-jnp.abs(x)))
        n = float(TILE)
        ii = lax.broadcasted_iota(jnp.int32, (TILE, TILE), 0).astype(F32)
        jj = lax.broadcasted_iota(jnp.int32, (TILE, TILE), 1).astype(F32)
        d = ii - jj
        i_col = lax.broadcasted_iota(jnp.int32, (TILE, LANES), 0).astype(F32)
        for h in range(RET_HEADS):
            lf = lg[h:h + 1, :]
            lb = lg[RET_HEADS + h:RET_HEADS + h + 1, :]
            lf_t = jnp.concatenate([lf] * (TILE // LANES), axis=1)
            lb_t = jnp.concatenate([lb] * (TILE // LANES), axis=1)
            fwd = jnp.exp(lf_t * jnp.maximum(d, 0.0))
            bwd = jnp.exp(lb_t * jnp.maximum(-d, 0.0))
            dmask_ref[h] = jnp.where(d > 0, fwd, jnp.where(d < 0, bwd, 2.0))
            qdec_ref[h] = jnp.exp(lf * (i_col + 1.0))
            qdec_ref[RET_HEADS + h] = jnp.exp(lb * (n - i_col))
        for ref in (kdec_ref, kdec_c_ref):
            length = ref.shape[1]
            tok = lax.broadcasted_iota(jnp.int32, (N_DIR, length), 1).astype(F32)
            row = lax.broadcasted_iota(jnp.int32, (N_DIR, length), 0)
            lg_t = jnp.concatenate([lg] * (length // LANES), axis=1)
            ref[...] = jnp.where(row < RET_HEADS, jnp.exp(lg_t * (float(length) - 1.0 - tok)),
                                 jnp.exp(lg_t * tok))
        cdec_ref[...] = jnp.exp(lg * n)


def _run_mod(cc, w_mod, b_mod, dl, *, ctx_len):
    n3 = 3
    const2 = lambda j: (0, 0)
    const3 = lambda j: (0, 0, 0)
    return pl.pallas_call(
        _mod_kernel,
        grid=(n3,),
        in_specs=[
            pl.BlockSpec((MOD_ROWS, D_MODEL), const2),
            pl.BlockSpec((D_MODEL, D_MODEL), lambda j: (0, j)),
            pl.BlockSpec((1, 1, D_MODEL), lambda j: (j, 0, 0)),
            pl.BlockSpec((N_DIR, LANES), const2),
        ],
        out_specs=[
            pl.BlockSpec((1, MOD_ROWS, D_MODEL), lambda j: (j, 0, 0)),
            pl.BlockSpec((RET_HEADS, TILE, TILE), const3),
            pl.BlockSpec((N_DIR, TILE, LANES), const3),
            pl.BlockSpec((N_DIR, TILE), const2),
            pl.BlockSpec((N_DIR, ctx_len), const2),
            pl.BlockSpec((N_DIR, LANES), const2),
        ],
        out_shape=[
            jax.ShapeDtypeStruct((n3, MOD_ROWS, D_MODEL), F32),
            jax.ShapeDtypeStruct((RET_HEADS, TILE, TILE), F32),
            jax.ShapeDtypeStruct((N_DIR, TILE, LANES), F32),
            jax.ShapeDtypeStruct((N_DIR, TILE), F32),
            jax.ShapeDtypeStruct((N_DIR, ctx_len), F32),
            jax.ShapeDtypeStruct((N_DIR, LANES), F32),
        ],
        compiler_params=pltpu.CompilerParams(dimension_semantics=("arbitrary",),
                                             vmem_limit_bytes=VMEM_LIMIT),
        name="mod",
    )(cc, w_mod, b_mod, dl)


def _inproj_kernel(x_ref, sc_ref, sh_ref, nw_ref, w_ref, qnw_ref, knw_ref, rt_ref, ct_ref, kdec_ref,
                   qT_o, kz_o, vT_o, ga_o, qr_o, krT_o, vr_o, gr_o, u_o):
    tm = x_ref.shape[0]
    x = x_ref[...]
    ms = jnp.mean(x * x, axis=-1, keepdims=True)
    a = nw_ref[...] * (1.0 + sc_ref[0])
    hb = ((x * lax.rsqrt(ms + NORM_EPS)) * a + sh_ref[0]).astype(BF16)

    def proj(lo, width):
        return jnp.dot(hb, w_ref[:, lo:lo + width], preferred_element_type=F32)

    def rope_table(k):
        rt = rt_ref[k]
        rows = [jnp.broadcast_to(rt[i:i + 1, :], (GRID_W, LANES)) for i in range(tm // GRID_W)]
        return jnp.concatenate(rows, axis=0) + ct_ref[k]

    ca, sa, cr, sr = (rope_table(k) for k in range(4))
    lane = lax.broadcasted_iota(jnp.int32, (tm, LANES), 1)
    head_lo = lane < ATT_HEAD_DIM
    half_lo = (lane & (ATT_HEAD_DIM // 2)) == 0
    inv_hd = 1.0 / ATT_HEAD_DIM

    def att_norm_rope(v, w):
        sq = v * v
        s_lo = jnp.sum(jnp.where(head_lo, sq, 0.0), axis=-1, keepdims=True)
        s_hi = jnp.sum(jnp.where(head_lo, 0.0, sq), axis=-1, keepdims=True)
        r = jnp.where(head_lo, lax.rsqrt(s_lo * inv_hd + NORM_EPS),
                      lax.rsqrt(s_hi * inv_hd + NORM_EPS))
        vn = v * r * w
        rot = jnp.where(half_lo, pltpu.roll(vn, LANES - ATT_HEAD_DIM // 2, 1),
                        pltpu.roll(vn, ATT_HEAD_DIM // 2, 1))
        return vn * ca + rot * sa

    def ret_rope(v):
        return v * cr + pltpu.roll(v, RET_HEAD_DIM // 2, 1) * sr

    qnw = qnw_ref[...]
    qa = proj(O_QA, ATT_WIDTH)
    for c in range(ATT_WIDTH // LANES):
        cs = slice(c * LANES, (c + 1) * LANES)
        qT = (att_norm_rope(qa[:, cs], qnw) * Q_SCALE_LOG2).T
        qT_o[0, 0, cs, :] = qT.astype(BF16)

    ka = att_norm_rope(proj(O_KA, ATT_KV_WIDTH), knw_ref[...])
    kz_o[0] = jnp.where(head_lo, ka, 0.0).astype(BF16)
    kz_o[1] = jnp.where(head_lo, 0.0, ka).astype(BF16)

    va = proj(O_VA, ATT_KV_WIDTH)
    ones_col = jnp.where(lane == ATT_HEAD_DIM, 1.0, 0.0)
    for g in range(ATT_KV_HEADS):
        vg = va if g == 0 else pltpu.roll(va, ATT_HEAD_DIM, 1)
        vT_o[0, 0, g] = jnp.where(head_lo, vg, ones_col).T[0:VT_ROWS, :].astype(BF16)

    ga_o[...] = _silu_tanh(proj(O_GA, ATT_WIDTH)).astype(BF16)

    vr = proj(O_VR, RET_WIDTH).astype(BF16)
    vr_o[...] = vr
    qr = proj(O_QR, RET_WIDTH)
    kr = proj(O_KR, RET_WIDTH)
    kdec = kdec_ref[...]
    for h in range(RET_HEADS):
        hs = slice(h * RET_HEAD_DIM, (h + 1) * RET_HEAD_DIM)
        qr_o[:, hs] = ret_rope(qr[:, hs]).astype(BF16)
        krT = (ret_rope(kr[:, hs]) * (RET_HEAD_DIM ** -0.5)).T
        krT_o[0, 0, hs, :] = krT.astype(BF16)
        for row in (h, RET_HEADS + h):
            kd = (krT * kdec[row:row + 1, :]).astype(BF16)
            u_o[0, 0, row] = jnp.dot(kd, vr[:, hs], preferred_element_type=F32)

    gr_o[...] = _silu_tanh(proj(O_GR, RET_WIDTH)).astype(BF16)


def _run_inproj(x2, scale, shift, norm_w, w_in, qnw, knw, row_tab, col_tab, kdec,
                *, batch, seq, tm):
    nt = seq // tm
    rows = batch * seq
    row_map = lambda t, b: (b * nt + t, 0)
    const2 = lambda t, b: (0, 0)
    vec_spec = pl.BlockSpec((1, 1, D_MODEL), lambda t, b: (b, 0, 0))
    t_map = lambda t, b: (b, t, 0, 0)

    def row_out(width):
        return pl.BlockSpec((tm, width), row_map)

    return pl.pallas_call(
        _inproj_kernel,
        grid=(nt, batch),
        in_specs=[
            pl.BlockSpec((tm, D_MODEL), row_map),
            vec_spec, vec_spec,
            pl.BlockSpec((1, D_MODEL), const2),
            pl.BlockSpec((D_MODEL, IN_WIDTH), const2),
            pl.BlockSpec((1, LANES), const2),
            pl.BlockSpec((1, LANES), const2),
            pl.BlockSpec((4, tm // GRID_W, LANES), lambda t, b: (0, t, 0)),
            pl.BlockSpec((4, tm, LANES), lambda t, b: (0, 0, 0)),
            pl.BlockSpec((N_DIR, tm), const2),
        ],
        out_specs=[
            pl.BlockSpec((1, 1, ATT_WIDTH, tm), t_map),
            pl.BlockSpec((ATT_KV_HEADS, tm, LANES), lambda t, b: (0, b * nt + t, 0)),
            pl.BlockSpec((1, 1, ATT_KV_HEADS, VT_ROWS, tm), lambda t, b: (b, t, 0, 0, 0)),
            row_out(ATT_WIDTH),
            row_out(RET_WIDTH),
            pl.BlockSpec((1, 1, RET_WIDTH, tm), t_map),
            row_out(RET_WIDTH),
            row_out(RET_WIDTH),
            pl.BlockSpec((1, 1, N_DIR, RET_HEAD_DIM, RET_HEAD_DIM), lambda t, b: (b, t, 0, 0, 0)),
        ],
        out_shape=[
            jax.ShapeDtypeStruct((batch, nt, ATT_WIDTH, tm), BF16),
            jax.ShapeDtypeStruct((ATT_KV_HEADS, rows, LANES), BF16),
            jax.ShapeDtypeStruct((batch, nt, ATT_KV_HEADS, VT_ROWS, tm), BF16),
            jax.ShapeDtypeStruct((rows, ATT_WIDTH), BF16),
            jax.ShapeDtypeStruct((rows, RET_WIDTH), BF16),
            jax.ShapeDtypeStruct((batch, nt, RET_WIDTH, tm), BF16),
            jax.ShapeDtypeStruct((rows, RET_WIDTH), BF16),
            jax.ShapeDtypeStruct((rows, RET_WIDTH), BF16),
            jax.ShapeDtypeStruct((batch, nt, N_DIR, RET_HEAD_DIM, RET_HEAD_DIM), F32),
        ],
        compiler_params=pltpu.CompilerParams(dimension_semantics=("arbitrary", "arbitrary"),
                                             vmem_limit_bytes=VMEM_LIMIT),
        name="inproj",
    )(x2, scale, shift, norm_w, w_in, qnw, knw, row_tab, col_tab, kdec)


def _scan_kernel(uc_ref, uf_ref, ub_ref, cdec_ref, sf_o, sb_o, st):
    nh = RET_HEADS
    cdec = cdec_ref[...]

    @pl.when(pl.program_id(1) == 0)
    def _():
        st[...] = uc_ref[0, 0]

    for h in range(nh):
        s = st[h]
        sf_o[0, 0, h] = s.astype(BF16)
        st[h] = cdec[h:h + 1, :] * s + uf_ref[0, 0, h]
        s = st[nh + h]
        sb_o[0, 0, h] = s.astype(BF16)
        st[nh + h] = cdec[nh + h:nh + h + 1, :] * s + ub_ref[0, 0, h]


def _run_scan(u_c, u, cdec, *, batch, seq):
    nt = seq // TILE
    hd = RET_HEAD_DIM
    blk = (1, 1, RET_HEADS, hd, hd)
    s_shape = jax.ShapeDtypeStruct((batch, nt, RET_HEADS, hd, hd), BF16)
    return pl.pallas_call(
        _scan_kernel,
        grid=(batch, nt),
        in_specs=[
            pl.BlockSpec((1, 1, N_DIR, hd, hd), lambda b, t: (b, 0, 0, 0, 0)),
            pl.BlockSpec(blk, lambda b, t: (b, t, 0, 0, 0)),
            pl.BlockSpec(blk, lambda b, t: (b, nt - 1 - t, 1, 0, 0)),
            pl.BlockSpec((N_DIR, LANES), lambda b, t: (0, 0)),
        ],
        out_specs=[pl.BlockSpec(blk, lambda b, t: (b, t, 0, 0, 0)),
                   pl.BlockSpec(blk, lambda b, t: (b, nt - 1 - t, 0, 0, 0))],
        out_shape=[s_shape, s_shape],
        scratch_shapes=[pltpu.VMEM((N_DIR, hd, hd), F32)],
        compiler_params=pltpu.CompilerParams(dimension_semantics=("arbitrary", "arbitrary"),
                                             vmem_limit_bytes=VMEM_LIMIT),
        name="scan",
    )(u_c, u, u, cdec)


def _attn_kernel(qT_ref, kc_ref, vTc_ref, k_ref, vT_ref, ga_ref, o_ref, *, tq, tk, n_kv,
                 bounded, unroll):
    hd = ATT_HEAD_DIM
    qT = qT_ref[0, 0]
    wq = jnp.concatenate(
        [jnp.concatenate([qT[h * hd:(h + 1) * hd, :]] * ATT_KV_HEADS, axis=0)
         for h in range(ATT_GROUP)], axis=1)
    cols = ATT_GROUP * tq

    if bounded:
        def block(acc, k, vT):
            sT = jnp.dot(k, wq, preferred_element_type=F32)
            return acc + jnp.dot(vT, jnp.exp2(sT).astype(BF16), preferred_element_type=F32)

        carry = jnp.zeros((VT_ROWS, cols), F32)
    else:
        def block(carry, k, vT):
            m, acc = carry
            sT = jnp.dot(k, wq, preferred_element_type=F32)
            m_new = jnp.maximum(m, jnp.max(sT, axis=0, keepdims=True))
            p = jnp.exp2(sT - m_new).astype(BF16)
            acc = jnp.exp2(m - m_new) * acc + jnp.dot(vT, p, preferred_element_type=F32)
            return m_new, acc

        carry = (jnp.full((1, cols), -1e30, F32), jnp.zeros((VT_ROWS, cols), F32))

    carry = block(carry, kc_ref[0], vTc_ref[0, 0, 0])

    def body(j, carry):
        off = pl.multiple_of(j * tk, tk)
        return block(carry, k_ref[0, pl.ds(off, tk), :], vT_ref[0, j, 0])

    carry = lax.fori_loop(0, n_kv, body, carry, unroll=unroll)
    acc = carry if bounded else carry[1]
    oT = acc[0:hd, :] * (1.0 / acc[hd:hd + 1, :])
    oT = jnp.concatenate([oT, jnp.zeros((LANES - hd, cols), F32)], axis=0)
    out = jnp.concatenate([oT[:, h * tq:(h + 1) * tq].T[:, 0:hd] for h in range(ATT_GROUP)], axis=1)
    o_ref[...] = (out * ga_ref[...].astype(F32)).astype(BF16)


def _run_attn(qT, kz_c, vT_c, kz, vT, ga, *, batch, seq, ctx_len, bounded, unroll):
    tq = TILE
    nq = seq // tq
    n_kv = seq // TILE
    gw = ATT_GROUP * ATT_HEAD_DIM
    kern = functools.partial(_attn_kernel, tq=tq, tk=TILE, n_kv=n_kv, bounded=bounded,
                             unroll=unroll)
    return pl.pallas_call(
        kern,
        grid=(batch, ATT_KV_HEADS, nq),
        in_specs=[
            pl.BlockSpec((1, 1, gw, tq), lambda b, g, i: (b, i, g, 0)),
            pl.BlockSpec((1, ctx_len, LANES), lambda b, g, i: (g, b, 0)),
            pl.BlockSpec((1, 1, 1, VT_ROWS, ctx_len), lambda b, g, i: (b, 0, g, 0, 0)),
            pl.BlockSpec((1, seq, LANES), lambda b, g, i: (g, b, 0)),
            pl.BlockSpec((1, n_kv, 1, VT_ROWS, TILE), lambda b, g, i: (b, 0, g, 0, 0)),
            pl.BlockSpec((tq, gw), lambda b, g, i: (b * nq + i, g)),
        ],
        out_specs=pl.BlockSpec((tq, gw), lambda b, g, i: (b * nq + i, g)),
        out_shape=jax.ShapeDtypeStruct((batch * seq, ATT_WIDTH), BF16),
        compiler_params=pltpu.CompilerParams(
            dimension_semantics=("arbitrary", "arbitrary", "arbitrary"),
            vmem_limit_bytes=VMEM_LIMIT),
        name="attn_bounded" if bounded else "attn_online",
    )(qT, kz_c, vT_c, kz, vT, ga)


def _epilogue_kernel(att_ref, qr_ref, krT_ref, vr_ref, gr_ref, x_ref, sf_ref, sb_ref,
                     dmask_ref, qdec_ref, gnw_ref, gate_ref, wout_ref, fnw_ref, y_ref, ret_scr):
    nh = RET_HEADS
    for h in range(nh):
        hs = slice(h * RET_HEAD_DIM, (h + 1) * RET_HEAD_DIM)
        q = qr_ref[:, hs]
        qf = q.astype(F32)
        a = jnp.dot(q, krT_ref[0, 0, hs, :], preferred_element_type=F32) * dmask_ref[h]
        o = (jnp.dot(a.astype(BF16), vr_ref[:, hs], preferred_element_type=F32)
             + jnp.dot((qf * qdec_ref[h]).astype(BF16), sf_ref[0, 0, h],
                       preferred_element_type=F32)
             + jnp.dot((qf * qdec_ref[nh + h]).astype(BF16), sb_ref[0, 0, h],
                       preferred_element_type=F32))
        mu = jnp.mean(o, axis=-1, keepdims=True)
        d = o - mu
        var = jnp.mean(d * d, axis=-1, keepdims=True)
        yh = d * lax.rsqrt(var + NORM_EPS) * gnw_ref[:, hs] * gr_ref[:, hs].astype(F32)
        ret_scr[:, hs] = yh.astype(BF16)
    proj = (jnp.dot(att_ref[...], wout_ref[0:ATT_WIDTH, :], preferred_element_type=F32)
            + jnp.dot(ret_scr[...], wout_ref[ATT_WIDTH:, :], preferred_element_type=F32))
    z = x_ref[...] + gate_ref[0] * proj
    ms = jnp.mean(z * z, axis=-1, keepdims=True)
    y_ref[...] = z * lax.rsqrt(ms + NORM_EPS) * fnw_ref[...]


def _run_epilogue(att, qr, krT, vr, gr, x2, sf, sb, dmask, qdec, gnw, gate, w_out, fnw,
                  *, batch, seq):
    nt = seq // TILE
    row_map = lambda b, t: (b * nt + t, 0)
    const2 = lambda b, t: (0, 0)
    const3 = lambda b, t: (0, 0, 0)
    s_spec = pl.BlockSpec((1, 1, RET_HEADS, RET_HEAD_DIM, RET_HEAD_DIM),
                          lambda b, t: (b, t, 0, 0, 0))
    return pl.pallas_call(
        _epilogue_kernel,
        grid=(batch, nt),
        in_specs=[
            pl.BlockSpec((TILE, ATT_WIDTH), row_map),
            pl.BlockSpec((TILE, RET_WIDTH), row_map),
            pl.BlockSpec((1, 1, RET_WIDTH, TILE), lambda b, t: (b, t, 0, 0)),
            pl.BlockSpec((TILE, RET_WIDTH), row_map),
            pl.BlockSpec((TILE, RET_WIDTH), row_map),
            pl.BlockSpec((TILE, D_MODEL), row_map),
            s_spec, s_spec,
            pl.BlockSpec((RET_HEADS, TILE, TILE), const3),
            pl.BlockSpec((N_DIR, TILE, LANES), const3),
            pl.BlockSpec((1, RET_WIDTH), const2),
            pl.BlockSpec((1, 1, D_MODEL), lambda b, t: (b, 0, 0)),
            pl.BlockSpec((ATT_WIDTH + RET_WIDTH, D_MODEL), const2),
            pl.BlockSpec((1, D_MODEL), const2),
        ],
        out_specs=pl.BlockSpec((TILE, D_MODEL), row_map),
        out_shape=jax.ShapeDtypeStruct((batch * seq, D_MODEL), F32),
        scratch_shapes=[pltpu.VMEM((TILE, RET_WIDTH), BF16)],
        compiler_params=pltpu.CompilerParams(dimension_semantics=("arbitrary", "arbitrary"),
                                             vmem_limit_bytes=VMEM_LIMIT),
        name="epilogue",
    )(att, qr, krT, vr, gr, x2, sf, sb, dmask, qdec, gnw, gate, w_out, fnw)


def _rope_tables(seq):
    rows = jnp.arange(seq // GRID_W, dtype=F32)
    cols = jnp.arange(GRID_W, dtype=F32)

    def parts(head_dim):
        n_axis = head_dim // 4
        inv_freq = ROPE_THETA ** (-jnp.arange(n_axis, dtype=F32) / n_axis)
        reps = LANES // head_dim

        def lanes(pos, first):
            ang = pos[:, None] * inv_freq
            z = jnp.zeros_like(ang)
            c = jnp.concatenate([jnp.cos(ang), z] if first else [z, jnp.cos(ang)], axis=-1)
            s = jnp.concatenate([jnp.sin(ang), z] if first else [z, jnp.sin(ang)], axis=-1)
            return (jnp.tile(jnp.concatenate([c, c], axis=-1), (1, reps)),
                    jnp.tile(jnp.concatenate([-s, s], axis=-1), (1, reps)))

        return lanes(rows, True), lanes(cols, False)

    (rca, rsa), (cca, csa) = parts(ATT_HEAD_DIM)
    (rcr, rsr), (ccr, csr) = parts(RET_HEAD_DIM)
    return jnp.stack([rca, rsa, rcr, rsr]), jnp.stack([cca, csa, ccr, csr])


def kernel(x, c, ctx, c_ctx, norm_w, w_mod, b_mod, w_in, q_norm_w, k_norm_w,
           ret_decay_fwd, ret_decay_bwd, ret_gn_w, w_out, final_norm_w):
    batch, seq, d = x.shape
    ctx_len = ctx.shape[1]
    depth = norm_w.shape[0]
    assert d == D_MODEL and depth == 1 and seq % TILE == 0
    assert ctx_len % LANES == 0 and ctx_len <= TILE
    assert batch + 1 <= MOD_ROWS and w_in.shape[-1] == IN_WIDTH
    layer = 0

    cc = jnp.zeros((MOD_ROWS, D_MODEL), F32).at[:batch].set(c).at[batch].set(c_ctx)
    dl = jnp.broadcast_to(
        jnp.concatenate([ret_decay_fwd[layer], ret_decay_bwd[layer]]).astype(F32)[:, None],
        (N_DIR, LANES))
    mod, dmask, qdec, kdec, kdec_c, cdec = _run_mod(
        cc, w_mod[layer], b_mod[layer].reshape(3, 1, D_MODEL), dl, ctx_len=ctx_len)
    shift, scale, gate = mod[0], mod[1], mod[2]

    w_in_b = w_in[layer].astype(BF16)
    nw = norm_w[layer].reshape(1, D_MODEL)
    qnw = jnp.tile(q_norm_w[layer], LANES // ATT_HEAD_DIM).reshape(1, LANES)
    knw = jnp.tile(k_norm_w[layer], LANES // ATT_HEAD_DIM).reshape(1, LANES)

    ctx_rows = ctx_len // GRID_W
    one = jnp.ones((ctx_rows, LANES), F32)
    zero = jnp.zeros((ctx_rows, LANES), F32)
    row_tab_c = jnp.stack([one, zero, one, zero])
    col_tab_c = jnp.zeros((4, ctx_len, LANES), F32)
    sc_c = jnp.broadcast_to(scale[batch].reshape(1, 1, D_MODEL), (batch, 1, D_MODEL))
    sh_c = jnp.broadcast_to(shift[batch].reshape(1, 1, D_MODEL), (batch, 1, D_MODEL))
    (_, kz_c, vT_c, _, _, _, _, _, u_c) = _run_inproj(
        ctx.reshape(batch * ctx_len, d), sc_c, sh_c, nw, w_in_b, qnw, knw,
        row_tab_c, col_tab_c, kdec_c, batch=batch, seq=ctx_len, tm=ctx_len)

    x2 = x.reshape(batch * seq, d)
    sc_l = scale[:batch].reshape(batch, 1, D_MODEL)
    sh_l = shift[:batch].reshape(batch, 1, D_MODEL)
    row_tab, col_tab = _rope_tables(seq)
    col_tab = jnp.tile(col_tab, (1, TILE // GRID_W, 1))
    qT, kz, vT, ga, qr, krT, vr, gr, u = _run_inproj(
        x2, sc_l, sh_l, nw, w_in_b, qnw, knw, row_tab, col_tab, kdec,
        batch=batch, seq=seq, tm=TILE)

    sf, sb = _run_scan(u_c, u, cdec, batch=batch, seq=seq)
    score_bound = (ATT_HEAD_DIM * Q_SCALE_LOG2 * BF16_ROUND_UP ** 2
                   * jnp.max(jnp.abs(q_norm_w[layer])) * jnp.max(jnp.abs(k_norm_w[layer])))
    attn = functools.partial(_run_attn, batch=batch, seq=seq, ctx_len=ctx_len)
    att = lax.cond(score_bound <= SCORE_LOG2_LIMIT,
                   functools.partial(attn, bounded=True, unroll=seq // TILE),
                   functools.partial(attn, bounded=False, unroll=1),
                   qT, kz_c, vT_c, kz, vT, ga)
    y = _run_epilogue(att, qr, krT, vr, gr, x2, sf, sb, dmask, qdec,
                      ret_gn_w[layer].reshape(1, RET_WIDTH),
                      gate[:batch].reshape(batch, 1, D_MODEL),
                      w_out[layer].astype(BF16), final_norm_w.reshape(1, D_MODEL),
                      batch=batch, seq=seq)
    return y.reshape(batch, seq, d)
```

```python
import functools

import jax
import jax.numpy as jnp
from jax import lax
from jax.experimental import pallas as pl
from jax.experimental.pallas import tpu as pltpu

F32 = jnp.float32
BF16 = jnp.bfloat16

D_MODEL = 1024
GRID_W = 64
ATT_HEADS = 8
ATT_KV_HEADS = 2
ATT_GROUP = ATT_HEADS // ATT_KV_HEADS
ATT_HEAD_DIM = 64
ATT_WIDTH = ATT_HEADS * ATT_HEAD_DIM
ATT_KV_WIDTH = ATT_KV_HEADS * ATT_HEAD_DIM
RET_HEADS = 4
RET_HEAD_DIM = 128
RET_WIDTH = RET_HEADS * RET_HEAD_DIM
ROPE_THETA = 10000.0
NORM_EPS = 1e-6
LANES = 128
TILE = 512
MOD_ROWS = 8
VT_ROWS = 80
N_DIR = 2 * RET_HEADS
MXU_TILE = 256
PV_LAG = 2

O_QA = 0
O_KA = O_QA + ATT_WIDTH
O_VA = O_KA + ATT_KV_WIDTH
O_GA = O_VA + ATT_KV_WIDTH
O_QR = O_GA + ATT_WIDTH
O_KR = O_QR + RET_WIDTH
O_VR = O_KR + RET_WIDTH
O_GR = O_VR + RET_WIDTH
IN_WIDTH = O_GR + RET_WIDTH

VMEM_LIMIT = 48 * 1024 * 1024

LOG2_E = 1.4426950408889634
Q_SCALE_LOG2 = ATT_HEAD_DIM ** -0.5 * LOG2_E
SCORE_LOG2_LIMIT = 100.0
BF16_ROUND_UP = 1.0 + 2.0 ** -8


def _silu(x):
    return x * (1.0 / (1.0 + jnp.exp(-x)))


def _silu_tanh(x):
    hx = 0.5 * x
    return hx + hx * jnp.tanh(hx)


def _mod_kernel(cc_ref, w_ref, b_ref, dl_ref,
                mod_ref, dmask_ref, qdec_ref, kdec_ref, kdec_c_ref, cdec_ref):
    sc = _silu(cc_ref[...])
    mod_ref[0] = jnp.dot(sc, w_ref[...], preferred_element_type=F32,
                         precision=lax.Precision.HIGHEST) + b_ref[0]

    @pl.when(pl.program_id(0) == 0)
    def _():
        x = dl_ref[...]
        lg = jnp.minimum(x, 0.0) - jnp.log1p(jnp.exp(-jnp.abs(x)))
        n = float(TILE)
        ii = lax.broadcasted_iota(jnp.int32, (TILE, TILE), 0).astype(F32)
        jj = lax.broadcasted_iota(jnp.int32, (TILE, TILE), 1).astype(F32)
        d = ii - jj
        i_col = lax.broadcasted_iota(jnp.int32, (TILE, LANES), 0).astype(F32)
        for h in range(RET_HEADS):
            lf = lg[h:h + 1, :]
            lb = lg[RET_HEADS + h:RET_HEADS + h + 1, :]
            lf_t = jnp.concatenate([lf] * (TILE // LANES), axis=1)
            lb_t = jnp.concatenate([lb] * (TILE // LANES), axis=1)
            fwd = jnp.exp(lf_t * jnp.maximum(d, 0.0))
            bwd = jnp.exp(lb_t * jnp.maximum(-d, 0.0))
            dmask_ref[h] = jnp.where(d > 0, fwd, jnp.where(d < 0, bwd, 2.0))
            qdec_ref[h] = jnp.exp(lf * (i_col + 1.0))
            qdec_ref[RET_HEADS + h] = jnp.exp(lb * (n - i_col))
        for ref in (kdec_ref, kdec_c_ref):
            length = ref.shape[1]
            tok = lax.broadcasted_iota(jnp.int32, (N_DIR, length), 1).astype(F32)
            row = lax.broadcasted_iota(jnp.int32, (N_DIR, length), 0)
            lg_t = jnp.concatenate([lg] * (length // LANES), axis=1)
            ref[...] = jnp.where(row < RET_HEADS, jnp.exp(lg_t * (float(length) - 1.0 - tok)),
                                 jnp.exp(lg_t * tok))
        cdec_ref[...] = jnp.exp(lg * n)


def _run_mod(cc, w_mod, b_mod, dl, *, ctx_len):
    n3 = 3
    const2 = lambda j: (0, 0)
    const3 = lambda j: (0, 0, 0)
    return pl.pallas_call(
        _mod_kernel,
        grid=(n3,),
        in_specs=[
            pl.BlockSpec((MOD_ROWS, D_MODEL), const2),
            pl.BlockSpec((D_MODEL, D_MODEL), lambda j: (0, j)),
            pl.BlockSpec((1, 1, D_MODEL), lambda j: (j, 0, 0)),
            pl.BlockSpec((N_DIR, LANES), const2),
        ],
        out_specs=[
            pl.BlockSpec((1, MOD_ROWS, D_MODEL), lambda j: (j, 0, 0)),
            pl.BlockSpec((RET_HEADS, TILE, TILE), const3),
            pl.BlockSpec((N_DIR, TILE, LANES), const3),
            pl.BlockSpec((N_DIR, TILE), const2),
            pl.BlockSpec((N_DIR, ctx_len), const2),
            pl.BlockSpec((N_DIR, LANES), const2),
        ],
        out_shape=[
            jax.ShapeDtypeStruct((n3, MOD_ROWS, D_MODEL), F32),
            jax.ShapeDtypeStruct((RET_HEADS, TILE, TILE), F32),
            jax.ShapeDtypeStruct((N_DIR, TILE, LANES), F32),
            jax.ShapeDtypeStruct((N_DIR, TILE), F32),
            jax.ShapeDtypeStruct((N_DIR, ctx_len), F32),
            jax.ShapeDtypeStruct((N_DIR, LANES), F32),
        ],
        compiler_params=pltpu.CompilerParams(dimension_semantics=("arbitrary",),
                                             vmem_limit_bytes=VMEM_LIMIT),
        name="mod",
    )(cc, w_mod, b_mod, dl)


def _inproj_kernel(x_ref, sc_ref, sh_ref, nw_ref, w_ref, qnw_ref, knw_ref, rt_ref, ct_ref, kdec_ref,
                   qT_o, kz_o, vT_o, ga_o, qr_o, krT_o, vr_o, gr_o, u_o):
    tm = x_ref.shape[0]
    x = x_ref[...]
    ms = jnp.mean(x * x, axis=-1, keepdims=True)
    a = nw_ref[...] * (1.0 + sc_ref[0])
    hb = ((x * lax.rsqrt(ms + NORM_EPS)) * a + sh_ref[0]).astype(BF16)

    def proj(lo, width):
        return jnp.dot(hb, w_ref[:, lo:lo + width], preferred_element_type=F32)

    def rope_table(k):
        rt = rt_ref[k]
        rows = [jnp.broadcast_to(rt[i:i + 1, :], (GRID_W, LANES)) for i in range(tm // GRID_W)]
        return jnp.concatenate(rows, axis=0) + ct_ref[k]

    ca, sa, cr, sr = (rope_table(k) for k in range(4))
    lane = lax.broadcasted_iota(jnp.int32, (tm, LANES), 1)
    head_lo = lane < ATT_HEAD_DIM
    half_lo = (lane & (ATT_HEAD_DIM // 2)) == 0
    inv_hd = 1.0 / ATT_HEAD_DIM

    def att_norm_rope(v, w):
        sq = v * v
        s_lo = jnp.sum(jnp.where(head_lo, sq, 0.0), axis=-1, keepdims=True)
        s_hi = jnp.sum(jnp.where(head_lo, 0.0, sq), axis=-1, keepdims=True)
        r = jnp.where(head_lo, lax.rsqrt(s_lo * inv_hd + NORM_EPS),
                      lax.rsqrt(s_hi * inv_hd + NORM_EPS))
        vn = v * r * w
        rot = jnp.where(half_lo, pltpu.roll(vn, LANES - ATT_HEAD_DIM // 2, 1),
                        pltpu.roll(vn, ATT_HEAD_DIM // 2, 1))
        return vn * ca + rot * sa

    def ret_rope(v):
        return v * cr + pltpu.roll(v, RET_HEAD_DIM // 2, 1) * sr

    qnw = qnw_ref[...]
    qa = proj(O_QA, ATT_WIDTH)
    for c in range(ATT_WIDTH // LANES):
        cs = slice(c * LANES, (c + 1) * LANES)
        qT = (att_norm_rope(qa[:, cs], qnw) * Q_SCALE_LOG2).T
        qT_o[0, 0, cs, :] = qT.astype(BF16)

    ka = att_norm_rope(proj(O_KA, ATT_KV_WIDTH), knw_ref[...])
    kz_o[0] = jnp.where(head_lo, ka, 0.0).astype(BF16)
    kz_o[1] = jnp.where(head_lo, 0.0, ka).astype(BF16)

    va = proj(O_VA, ATT_KV_WIDTH)
    ones_col = jnp.where(lane == ATT_HEAD_DIM, 1.0, 0.0)
    for g in range(ATT_KV_HEADS):
        vg = va if g == 0 else pltpu.roll(va, ATT_HEAD_DIM, 1)
        vT_o[0, 0, g] = jnp.where(head_lo, vg, ones_col).T[0:VT_ROWS, :].astype(BF16)

    ga_o[...] = _silu_tanh(proj(O_GA, ATT_WIDTH)).astype(BF16)

    vr = proj(O_VR, RET_WIDTH).astype(BF16)
    vr_o[...] = vr
    qr = proj(O_QR, RET_WIDTH)
    kr = proj(O_KR, RET_WIDTH)
    kdec = kdec_ref[...]
    for h in range(RET_HEADS):
        hs = slice(h * RET_HEAD_DIM, (h + 1) * RET_HEAD_DIM)
        qr_o[:, hs] = ret_rope(qr[:, hs]).astype(BF16)
        krT = (ret_rope(kr[:, hs]) * (RET_HEAD_DIM ** -0.5)).T
        krT_o[0, 0, hs, :] = krT.astype(BF16)
        for row in (h, RET_HEADS + h):
            kd = (krT * kdec[row:row + 1, :]).astype(BF16)
            u_o[0, 0, row] = jnp.dot(kd, vr[:, hs], preferred_element_type=F32)

    gr_o[...] = _silu_tanh(proj(O_GR, RET_WIDTH)).astype(BF16)


def _run_inproj(x2, scale, shift, norm_w, w_in, qnw, knw, row_tab, col_tab, kdec,
                *, batch, seq, tm):
    nt = seq // tm
    rows = batch * seq
    row_map = lambda t, b: (b * nt + t, 0)
    const2 = lambda t, b: (0, 0)
    vec_spec = pl.BlockSpec((1, 1, D_MODEL), lambda t, b: (b, 0, 0))
    t_map = lambda t, b: (b, t, 0, 0)

    def row_out(width):
        return pl.BlockSpec((tm, width), row_map)

    return pl.pallas_call(
        _inproj_kernel,
        grid=(nt, batch),
        in_specs=[
            pl.BlockSpec((tm, D_MODEL), row_map),
            vec_spec, vec_spec,
            pl.BlockSpec((1, D_MODEL), const2),
            pl.BlockSpec((D_MODEL, IN_WIDTH), const2),
            pl.BlockSpec((1, LANES), const2),
            pl.BlockSpec((1, LANES), const2),
            pl.BlockSpec((4, tm // GRID_W, LANES), lambda t, b: (0, t, 0)),
            pl.BlockSpec((4, tm, LANES), lambda t, b: (0, 0, 0)),
            pl.BlockSpec((N_DIR, tm), const2),
        ],
        out_specs=[
            pl.BlockSpec((1, 1, ATT_WIDTH, tm), t_map),
            pl.BlockSpec((ATT_KV_HEADS, tm, LANES), lambda t, b: (0, b * nt + t, 0)),
            pl.BlockSpec((1, 1, ATT_KV_HEADS, VT_ROWS, tm), lambda t, b: (b, t, 0, 0, 0)),
            row_out(ATT_WIDTH),
            row_out(RET_WIDTH),
            pl.BlockSpec((1, 1, RET_WIDTH, tm), t_map),
            row_out(RET_WIDTH),
            row_out(RET_WIDTH),
            pl.BlockSpec((1, 1, N_DIR, RET_HEAD_DIM, RET_HEAD_DIM), lambda t, b: (b, t, 0, 0, 0)),
        ],
        out_shape=[
            jax.ShapeDtypeStruct((batch, nt, ATT_WIDTH, tm), BF16),
            jax.ShapeDtypeStruct((ATT_KV_HEADS, rows, LANES), BF16),
            jax.ShapeDtypeStruct((batch, nt, ATT_KV_HEADS, VT_ROWS, tm), BF16),
            jax.ShapeDtypeStruct((rows, ATT_WIDTH), BF16),
            jax.ShapeDtypeStruct((rows, RET_WIDTH), BF16),
            jax.ShapeDtypeStruct((batch, nt, RET_WIDTH, tm), BF16),
            jax.ShapeDtypeStruct((rows, RET_WIDTH), BF16),
            jax.ShapeDtypeStruct((rows, RET_WIDTH), BF16),
            jax.ShapeDtypeStruct((batch, nt, N_DIR, RET_HEAD_DIM, RET_HEAD_DIM), F32),
        ],
        compiler_params=pltpu.CompilerParams(dimension_semantics=("arbitrary", "arbitrary"),
                                             vmem_limit_bytes=VMEM_LIMIT),
        name="inproj",
    )(x2, scale, shift, norm_w, w_in, qnw, knw, row_tab, col_tab, kdec)


def _scan_kernel(uc_ref, uf_ref, ub_ref, cdec_ref, sf_o, sb_o, st):
    nh = RET_HEADS
    cdec = cdec_ref[...]

    @pl.when(pl.program_id(1) == 0)
    def _():
        st[...] = uc_ref[0, 0]

    for h in range(nh):
        s = st[h]
        sf_o[0, 0, h] = s.astype(BF16)
        st[h] = cdec[h:h + 1, :] * s + uf_ref[0, 0, h]
        s = st[nh + h]
        sb_o[0, 0, h] = s.astype(BF16)
        st[nh + h] = cdec[nh + h:nh + h + 1, :] * s + ub_ref[0, 0, h]


def _run_scan(u_c, u, cdec, *, batch, seq):
    nt = seq // TILE
    hd = RET_HEAD_DIM
    blk = (1, 1, RET_HEADS, hd, hd)
    s_shape = jax.ShapeDtypeStruct((batch, nt, RET_HEADS, hd, hd), BF16)
    return pl.pallas_call(
        _scan_kernel,
        grid=(batch, nt),
        in_specs=[
            pl.BlockSpec((1, 1, N_DIR, hd, hd), lambda b, t: (b, 0, 0, 0, 0)),
            pl.BlockSpec(blk, lambda b, t: (b, t, 0, 0, 0)),
            pl.BlockSpec(blk, lambda b, t: (b, nt - 1 - t, 1, 0, 0)),
            pl.BlockSpec((N_DIR, LANES), lambda b, t: (0, 0)),
        ],
        out_specs=[pl.BlockSpec(blk, lambda b, t: (b, t, 0, 0, 0)),
                   pl.BlockSpec(blk, lambda b, t: (b, nt - 1 - t, 0, 0, 0))],
        out_shape=[s_shape, s_shape],
        scratch_shapes=[pltpu.VMEM((N_DIR, hd, hd), F32)],
        compiler_params=pltpu.CompilerParams(dimension_semantics=("arbitrary", "arbitrary"),
                                             vmem_limit_bytes=VMEM_LIMIT),
        name="scan",
    )(u_c, u, u, cdec)


def _attn_kernel(qT_ref, kc_ref, vTc_ref, k_ref, vT_ref, ga_ref, o_ref, *, tq, tk, n_kv, bounded):
    hd = ATT_HEAD_DIM
    qT = qT_ref[0, 0]
    wq = jnp.concatenate(
        [jnp.concatenate([qT[h * hd:(h + 1) * hd, :]] * ATT_KV_HEADS, axis=0)
         for h in range(ATT_GROUP)], axis=1)
    cols = ATT_GROUP * tq

    if bounded:
        n_ct = cols // MXU_TILE
        kv = [(kc_ref[0], vTc_ref[0, 0, 0])]
        kv += [(k_ref[0, j * tk:(j + 1) * tk, :], vT_ref[0, j, 0]) for j in range(n_kv)]
        units = [(b, c) for b in range(len(kv)) for c in range(n_ct)]
        accs = [jnp.zeros((VT_ROWS, MXU_TILE), F32) for _ in range(n_ct)]
        pending = []

        def value_matmul(b, c, sT):
            accs[c] = accs[c] + jnp.dot(kv[b][1], jnp.exp2(sT).astype(BF16),
                                        preferred_element_type=F32)

        for b, c in units:
            sT = jnp.dot(kv[b][0], wq[:, c * MXU_TILE:(c + 1) * MXU_TILE],
                         preferred_element_type=F32)
            pending.append((b, c, sT))
            if len(pending) > PV_LAG:
                value_matmul(*pending.pop(0))
        for item in pending:
            value_matmul(*item)
        acc = jnp.concatenate(accs, axis=1)
    else:
        def block(carry, k, vT):
            m, acc = carry
            sT = jnp.dot(k, wq, preferred_element_type=F32)
            m_new = jnp.maximum(m, jnp.max(sT, axis=0, keepdims=True))
            p = jnp.exp2(sT - m_new).astype(BF16)
            acc = jnp.exp2(m - m_new) * acc + jnp.dot(vT, p, preferred_element_type=F32)
            return m_new, acc

        carry = (jnp.full((1, cols), -1e30, F32), jnp.zeros((VT_ROWS, cols), F32))
        carry = block(carry, kc_ref[0], vTc_ref[0, 0, 0])

        def body(j, carry):
            off = pl.multiple_of(j * tk, tk)
            return block(carry, k_ref[0, pl.ds(off, tk), :], vT_ref[0, j, 0])

        acc = lax.fori_loop(0, n_kv, body, carry)[1]
    oT = acc[0:hd, :] * (1.0 / acc[hd:hd + 1, :])
    oT = jnp.concatenate([oT, jnp.zeros((LANES - hd, cols), F32)], axis=0)
    out = jnp.concatenate([oT[:, h * tq:(h + 1) * tq].T[:, 0:hd] for h in range(ATT_GROUP)], axis=1)
    o_ref[...] = (out * ga_ref[...].astype(F32)).astype(BF16)


def _run_attn(qT, kz_c, vT_c, kz, vT, ga, *, batch, seq, ctx_len, bounded):
    tq = TILE
    nq = seq // tq
    n_kv = seq // TILE
    gw = ATT_GROUP * ATT_HEAD_DIM
    kern = functools.partial(_attn_kernel, tq=tq, tk=TILE, n_kv=n_kv, bounded=bounded)
    return pl.pallas_call(
        kern,
        grid=(batch, ATT_KV_HEADS, nq),
        in_specs=[
            pl.BlockSpec((1, 1, gw, tq), lambda b, g, i: (b, i, g, 0)),
            pl.BlockSpec((1, ctx_len, LANES), lambda b, g, i: (g, b, 0)),
            pl.BlockSpec((1, 1, 1, VT_ROWS, ctx_len), lambda b, g, i: (b, 0, g, 0, 0)),
            pl.BlockSpec((1, seq, LANES), lambda b, g, i: (g, b, 0)),
            pl.BlockSpec((1, n_kv, 1, VT_ROWS, TILE), lambda b, g, i: (b, 0, g, 0, 0)),
            pl.BlockSpec((tq, gw), lambda b, g, i: (b * nq + i, g)),
        ],
        out_specs=pl.BlockSpec((tq, gw), lambda b, g, i: (b * nq + i, g)),
        out_shape=jax.ShapeDtypeStruct((batch * seq, ATT_WIDTH), BF16),
        compiler_params=pltpu.CompilerParams(
            dimension_semantics=("arbitrary", "arbitrary", "arbitrary"),
            vmem_limit_bytes=VMEM_LIMIT),
        name="attn_bounded" if bounded else "attn_online",
    )(qT, kz_c, vT_c, kz, vT, ga)


def _epilogue_kernel(att_ref, qr_ref, krT_ref, vr_ref, gr_ref, x_ref, sf_ref, sb_ref,
                     dmask_ref, qdec_ref, gnw_ref, gate_ref, wout_ref, fnw_ref, y_ref, ret_scr):
    nh = RET_HEADS
    for h in range(nh):
        hs = slice(h * RET_HEAD_DIM, (h + 1) * RET_HEAD_DIM)
        q = qr_ref[:, hs]
        qf = q.astype(F32)
        a = jnp.dot(q, krT_ref[0, 0, hs, :], preferred_element_type=F32) * dmask_ref[h]
        o = (jnp.dot(a.astype(BF16), vr_ref[:, hs], preferred_element_type=F32)
             + jnp.dot((qf * qdec_ref[h]).astype(BF16), sf_ref[0, 0, h],
                       preferred_element_type=F32)
             + jnp.dot((qf * qdec_ref[nh + h]).astype(BF16), sb_ref[0, 0, h],
                       preferred_element_type=F32))
        mu = jnp.mean(o, axis=-1, keepdims=True)
        d = o - mu
        var = jnp.mean(d * d, axis=-1, keepdims=True)
        yh = d * lax.rsqrt(var + NORM_EPS) * gnw_ref[:, hs] * gr_ref[:, hs].astype(F32)
        ret_scr[:, hs] = yh.astype(BF16)
    proj = (jnp.dot(att_ref[...], wout_ref[0:ATT_WIDTH, :], preferred_element_type=F32)
            + jnp.dot(ret_scr[...], wout_ref[ATT_WIDTH:, :], preferred_element_type=F32))
    z = x_ref[...] + gate_ref[0] * proj
    ms = jnp.mean(z * z, axis=-1, keepdims=True)
    y_ref[...] = z * lax.rsqrt(ms + NORM_EPS) * fnw_ref[...]


def _run_epilogue(att, qr, krT, vr, gr, x2, sf, sb, dmask, qdec, gnw, gate, w_out, fnw,
                  *, batch, seq):
    nt = seq // TILE
    row_map = lambda b, t: (b * nt + t, 0)
    const2 = lambda b, t: (0, 0)
    const3 = lambda b, t: (0, 0, 0)
    s_spec = pl.BlockSpec((1, 1, RET_HEADS, RET_HEAD_DIM, RET_HEAD_DIM),
                          lambda b, t: (b, t, 0, 0, 0))
    return pl.pallas_call(
        _epilogue_kernel,
        grid=(batch, nt),
        in_specs=[
            pl.BlockSpec((TILE, ATT_WIDTH), row_map),
            pl.BlockSpec((TILE, RET_WIDTH), row_map),
            pl.BlockSpec((1, 1, RET_WIDTH, TILE), lambda b, t: (b, t, 0, 0)),
            pl.BlockSpec((TILE, RET_WIDTH), row_map),
            pl.BlockSpec((TILE, RET_WIDTH), row_map),
            pl.BlockSpec((TILE, D_MODEL), row_map),
            s_spec, s_spec,
            pl.BlockSpec((RET_HEADS, TILE, TILE), const3),
            pl.BlockSpec((N_DIR, TILE, LANES), const3),
            pl.BlockSpec((1, RET_WIDTH), const2),
            pl.BlockSpec((1, 1, D_MODEL), lambda b, t: (b, 0, 0)),
            pl.BlockSpec((ATT_WIDTH + RET_WIDTH, D_MODEL), const2),
            pl.BlockSpec((1, D_MODEL), const2),
        ],
        out_specs=pl.BlockSpec((TILE, D_MODEL), row_map),
        out_shape=jax.ShapeDtypeStruct((batch * seq, D_MODEL), F32),
        scratch_shapes=[pltpu.VMEM((TILE, RET_WIDTH), BF16)],
        compiler_params=pltpu.CompilerParams(dimension_semantics=("arbitrary", "arbitrary"),
                                             vmem_limit_bytes=VMEM_LIMIT),
        name="epilogue",
    )(att, qr, krT, vr, gr, x2, sf, sb, dmask, qdec, gnw, gate, w_out, fnw)


def _rope_tables(seq):
    rows = jnp.arange(seq // GRID_W, dtype=F32)
    cols = jnp.arange(GRID_W, dtype=F32)

    def parts(head_dim):
        n_axis = head_dim // 4
        inv_freq = ROPE_THETA ** (-jnp.arange(n_axis, dtype=F32) / n_axis)
        reps = LANES // head_dim

        def lanes(pos, first):
            ang = pos[:, None] * inv_freq
            z = jnp.zeros_like(ang)
            c = jnp.concatenate([jnp.cos(ang), z] if first else [z, jnp.cos(ang)], axis=-1)
            s = jnp.concatenate([jnp.sin(ang), z] if first else [z, jnp.sin(ang)], axis=-1)
            return (jnp.tile(jnp.concatenate([c, c], axis=-1), (1, reps)),
                    jnp.tile(jnp.concatenate([-s, s], axis=-1), (1, reps)))

        return lanes(rows, True), lanes(cols, False)

    (rca, rsa), (cca, csa) = parts(ATT_HEAD_DIM)
    (rcr, rsr), (ccr, csr) = parts(RET_HEAD_DIM)
    return jnp.stack([rca, rsa, rcr, rsr]), jnp.stack([cca, csa, ccr, csr])


def kernel(x, c, ctx, c_ctx, norm_w, w_mod, b_mod, w_in, q_norm_w, k_norm_w,
           ret_decay_fwd, ret_decay_bwd, ret_gn_w, w_out, final_norm_w):
    batch, seq, d = x.shape
    ctx_len = ctx.shape[1]
    depth = norm_w.shape[0]
    assert d == D_MODEL and depth == 1 and seq % TILE == 0
    assert ctx_len % LANES == 0 and ctx_len <= TILE
    assert batch + 1 <= MOD_ROWS and w_in.shape[-1] == IN_WIDTH
    layer = 0

    cc = jnp.zeros((MOD_ROWS, D_MODEL), F32).at[:batch].set(c).at[batch].set(c_ctx)
    dl = jnp.broadcast_to(
        jnp.concatenate([ret_decay_fwd[layer], ret_decay_bwd[layer]]).astype(F32)[:, None],
        (N_DIR, LANES))
    mod, dmask, qdec, kdec, kdec_c, cdec = _run_mod(
        cc, w_mod[layer], b_mod[layer].reshape(3, 1, D_MODEL), dl, ctx_len=ctx_len)
    shift, scale, gate = mod[0], mod[1], mod[2]

    w_in_b = w_in[layer].astype(BF16)
    nw = norm_w[layer].reshape(1, D_MODEL)
    qnw = jnp.tile(q_norm_w[layer], LANES // ATT_HEAD_DIM).reshape(1, LANES)
    knw = jnp.tile(k_norm_w[layer], LANES // ATT_HEAD_DIM).reshape(1, LANES)

    ctx_rows = ctx_len // GRID_W
    one = jnp.ones((ctx_rows, LANES), F32)
    zero = jnp.zeros((ctx_rows, LANES), F32)
    row_tab_c = jnp.stack([one, zero, one, zero])
    col_tab_c = jnp.zeros((4, ctx_len, LANES), F32)
    sc_c = jnp.broadcast_to(scale[batch].reshape(1, 1, D_MODEL), (batch, 1, D_MODEL))
    sh_c = jnp.broadcast_to(shift[batch].reshape(1, 1, D_MODEL), (batch, 1, D_MODEL))
    (_, kz_c, vT_c, _, _, _, _, _, u_c) = _run_inproj(
        ctx.reshape(batch * ctx_len, d), sc_c, sh_c, nw, w_in_b, qnw, knw,
        row_tab_c, col_tab_c, kdec_c, batch=batch, seq=ctx_len, tm=ctx_len)

    x2 = x.reshape(batch * seq, d)
    sc_l = scale[:batch].reshape(batch, 1, D_MODEL)
    sh_l = shift[:batch].reshape(batch, 1, D_MODEL)
    row_tab, col_tab = _rope_tables(seq)
    col_tab = jnp.tile(col_tab, (1, TILE // GRID_W, 1))
    qT, kz, vT, ga, qr, krT, vr, gr, u = _run_inproj(
        x2, sc_l, sh_l, nw, w_in_b, qnw, knw, row_tab, col_tab, kdec,
        batch=batch, seq=seq, tm=TILE)

    sf, sb = _run_scan(u_c, u, cdec, batch=batch, seq=seq)
    score_bound = (ATT_HEAD_DIM * Q_SCALE_LOG2 * BF16_ROUND_UP ** 2
                   * jnp.max(jnp.abs(q_norm_w[layer])) * jnp.max(jnp.abs(k_norm_w[layer])))
    attn = functools.partial(_run_attn, batch=batch, seq=seq, ctx_len=ctx_len)
    att = lax.cond(score_bound <= SCORE_LOG2_LIMIT,
                   functools.partial(attn, bounded=True),
                   functools.partial(attn, bounded=False),
                   qT, kz_c, vT_c, kz, vT, ga)
    y = _run_epilogue(att, qr, krT, vr, gr, x2, sf, sb, dmask, qdec,
                      ret_gn_w[layer].reshape(1, RET_WIDTH),
                      gate[:batch].reshape(batch, 1, D_MODEL),
                      w_out[layer].astype(BF16), final_norm_w.reshape(1, D_MODEL),
                      batch=batch, seq=seq)
    return y.reshape(batch, seq, d)
```

```python
import functools

import jax
import jax.numpy as jnp
from jax import lax
from jax.experimental import pallas as pl
from jax.experimental.pallas import tpu as pltpu

F32 = jnp.float32
BF16 = jnp.bfloat16

D_MODEL = 1024
GRID_W = 64
ATT_HEADS = 8
ATT_KV_HEADS = 2
ATT_GROUP = ATT_HEADS // ATT_KV_HEADS
ATT_HEAD_DIM = 64
ATT_WIDTH = ATT_HEADS * ATT_HEAD_DIM
ATT_KV_WIDTH = ATT_KV_HEADS * ATT_HEAD_DIM
RET_HEADS = 4
RET_HEAD_DIM = 128
RET_WIDTH = RET_HEADS * RET_HEAD_DIM
ROPE_THETA = 10000.0
NORM_EPS = 1e-6
LANES = 128
TILE = 512
MOD_ROWS = 8
VT_ROWS = 80
N_DIR = 2 * RET_HEADS
MXU_TILE = 256
PV_LAG = 2

O_QA = 0
O_KA = O_QA + ATT_WIDTH
O_VA = O_KA + ATT_KV_WIDTH
O_GA = O_VA + ATT_KV_WIDTH
O_QR = O_GA + ATT_WIDTH
O_KR = O_QR + RET_WIDTH
O_VR = O_KR + RET_WIDTH
O_GR = O_VR + RET_WIDTH
IN_WIDTH = O_GR + RET_WIDTH

VMEM_LIMIT = 48 * 1024 * 1024

LOG2_E = 1.4426950408889634
Q_SCALE_LOG2 = ATT_HEAD_DIM ** -0.5 * LOG2_E
SCORE_LOG2_LIMIT = 100.0
BF16_ROUND_UP = 1.0 + 2.0 ** -8


def _silu(x):
    return x * (1.0 / (1.0 + jnp.exp(-x)))


def _silu_tanh(x):
    hx = 0.5 * x
    return hx + hx * jnp.tanh(hx)


def _mod_kernel(cc_ref, w_ref, b_ref, dl_ref,
                mod_ref, dmask_ref, qdec_ref, kdec_ref, kdec_c_ref, cdec_ref):
    sc = _silu(cc_ref[...])
    mod_ref[0] = jnp.dot(sc, w_ref[...], preferred_element_type=F32,
                         precision=lax.Precision.HIGHEST) + b_ref[0]

    @pl.when(pl.program_id(0) == 0)
    def _():
        x = dl_ref[...]
        lg = jnp.minimum(x, 0.0) - jnp.log1p(jnp.exp(-jnp.abs(x)))
        n = float(TILE)
        ii = lax.broadcasted_iota(jnp.int32, (TILE, TILE), 0).astype(F32)
        jj = lax.broadcasted_iota(jnp.int32, (TILE, TILE), 1).astype(F32)
        d = ii - jj
        i_col = lax.broadcasted_iota(jnp.int32, (TILE, LANES), 0).astype(F32)
        for h in range(RET_HEADS):
            lf = lg[h:h + 1, :]
            lb = lg[RET_HEADS + h:RET_HEADS + h + 1, :]
            lf_t = jnp.concatenate([lf] * (TILE // LANES), axis=1)
            lb_t = jnp.concatenate([lb] * (TILE // LANES), axis=1)
            fwd = jnp.exp(lf_t * jnp.maximum(d, 0.0))
            bwd = jnp.exp(lb_t * jnp.maximum(-d, 0.0))
            dmask_ref[h] = jnp.where(d > 0, fwd, jnp.where(d < 0, bwd, 2.0))
            qdec_ref[h] = jnp.exp(lf * (i_col + 1.0))
            qdec_ref[RET_HEADS + h] = jnp.exp(lb * (n - i_col))
        for ref in (kdec_ref, kdec_c_ref):
            length = ref.shape[1]
            tok = lax.broadcasted_iota(jnp.int32, (N_DIR, length), 1).astype(F32)
            row = lax.broadcasted_iota(jnp.int32, (N_DIR, length), 0)
            lg_t = jnp.concatenate([lg] * (length // LANES), axis=1)
            ref[...] = jnp.where(row < RET_HEADS, jnp.exp(lg_t * (float(length) - 1.0 - tok)),
                                 jnp.exp(lg_t * tok))
        cdec_ref[...] = jnp.exp(lg * n)


def _run_mod(cc, w_mod, b_mod, dl, *, ctx_len):
    n3 = 3
    const2 = lambda j: (0, 0)
    const3 = lambda j: (0, 0, 0)
    return pl.pallas_call(
        _mod_kernel,
        grid=(n3,),
        in_specs=[
            pl.BlockSpec((MOD_ROWS, D_MODEL), const2),
            pl.BlockSpec((D_MODEL, D_MODEL), lambda j: (0, j)),
            pl.BlockSpec((1, 1, D_MODEL), lambda j: (j, 0, 0)),
            pl.BlockSpec((N_DIR, LANES), const2),
        ],
        out_specs=[
            pl.BlockSpec((1, MOD_ROWS, D_MODEL), lambda j: (j, 0, 0)),
            pl.BlockSpec((RET_HEADS, TILE, TILE), const3),
            pl.BlockSpec((N_DIR, TILE, LANES), const3),
            pl.BlockSpec((N_DIR, TILE), const2),
            pl.BlockSpec((N_DIR, ctx_len), const2),
            pl.BlockSpec((N_DIR, LANES), const2),
        ],
        out_shape=[
            jax.ShapeDtypeStruct((n3, MOD_ROWS, D_MODEL), F32),
            jax.ShapeDtypeStruct((RET_HEADS, TILE, TILE), F32),
            jax.ShapeDtypeStruct((N_DIR, TILE, LANES), F32),
            jax.ShapeDtypeStruct((N_DIR, TILE), F32),
            jax.ShapeDtypeStruct((N_DIR, ctx_len), F32),
            jax.ShapeDtypeStruct((N_DIR, LANES), F32),
        ],
        compiler_params=pltpu.CompilerParams(dimension_semantics=("arbitrary",),
                                             vmem_limit_bytes=VMEM_LIMIT),
        name="mod",
    )(cc, w_mod, b_mod, dl)


def _inproj_kernel(x_ref, sc_ref, sh_ref, nw_ref, w_ref, qnw_ref, knw_ref, rt_ref, ct_ref, kdec_ref,
                   qT_o, kz_o, vT_o, ga_o, qr_o, krT_o, vr_o, gr_o, u_o):
    tm = x_ref.shape[0]
    x = x_ref[...]
    ms = jnp.mean(x * x, axis=-1, keepdims=True)
    a = nw_ref[...] * (1.0 + sc_ref[0])
    hb = ((x * lax.rsqrt(ms + NORM_EPS)) * a + sh_ref[0]).astype(BF16)

    def proj(lo, width):
        return jnp.dot(hb, w_ref[:, lo:lo + width], preferred_element_type=F32)

    def rope_table(k):
        rt = rt_ref[k]
        rows = [jnp.broadcast_to(rt[i:i + 1, :], (GRID_W, LANES)) for i in range(tm // GRID_W)]
        return jnp.concatenate(rows, axis=0) + ct_ref[k]

    ca, sa, cr, sr = (rope_table(k) for k in range(4))
    lane = lax.broadcasted_iota(jnp.int32, (tm, LANES), 1)
    head_lo = lane < ATT_HEAD_DIM
    half_lo = (lane & (ATT_HEAD_DIM // 2)) == 0
    inv_hd = 1.0 / ATT_HEAD_DIM

    def att_norm_rope(v, w):
        sq = v * v
        s_lo = jnp.sum(jnp.where(head_lo, sq, 0.0), axis=-1, keepdims=True)
        s_hi = jnp.sum(jnp.where(head_lo, 0.0, sq), axis=-1, keepdims=True)
        r = jnp.where(head_lo, lax.rsqrt(s_lo * inv_hd + NORM_EPS),
                      lax.rsqrt(s_hi * inv_hd + NORM_EPS))
        vn = v * r * w
        rot = jnp.where(half_lo, pltpu.roll(vn, LANES - ATT_HEAD_DIM // 2, 1),
                        pltpu.roll(vn, ATT_HEAD_DIM // 2, 1))
        return vn * ca + rot * sa

    def ret_rope(v):
        return v * cr + pltpu.roll(v, RET_HEAD_DIM // 2, 1) * sr

    qnw = qnw_ref[...]
    qa = proj(O_QA, ATT_WIDTH)
    for c in range(ATT_WIDTH // LANES):
        cs = slice(c * LANES, (c + 1) * LANES)
        qT = (att_norm_rope(qa[:, cs], qnw) * Q_SCALE_LOG2).T
        qT_o[0, 0, cs, :] = qT.astype(BF16)

    ka = att_norm_rope(proj(O_KA, ATT_KV_WIDTH), knw_ref[...])
    kz_o[0] = jnp.where(head_lo, ka, 0.0).astype(BF16)
    kz_o[1] = jnp.where(head_lo, 0.0, ka).astype(BF16)

    va = proj(O_VA, ATT_KV_WIDTH)
    ones_col = jnp.where(lane == ATT_HEAD_DIM, 1.0, 0.0)
    for g in range(ATT_KV_HEADS):
        vg = va if g == 0 else pltpu.roll(va, ATT_HEAD_DIM, 1)
        vT_o[0, 0, g] = jnp.where(head_lo, vg, ones_col).T[0:VT_ROWS, :].astype(BF16)

    ga_o[...] = _silu_tanh(proj(O_GA, ATT_WIDTH)).astype(BF16)

    vr = proj(O_VR, RET_WIDTH).astype(BF16)
    vr_o[...] = vr
    qr = proj(O_QR, RET_WIDTH)
    kr = proj(O_KR, RET_WIDTH)
    kdec = kdec_ref[...]
    for h in range(RET_HEADS):
        hs = slice(h * RET_HEAD_DIM, (h + 1) * RET_HEAD_DIM)
        qr_o[:, hs] = ret_rope(qr[:, hs]).astype(BF16)
        krT = (ret_rope(kr[:, hs]) * (RET_HEAD_DIM ** -0.5)).T
        krT_o[0, 0, hs, :] = krT.astype(BF16)
        for row in (h, RET_HEADS + h):
            kd = (krT * kdec[row:row + 1, :]).astype(BF16)
            u_o[0, 0, row] = jnp.dot(kd, vr[:, hs], preferred_element_type=F32)

    gr_o[...] = _silu_tanh(proj(O_GR, RET_WIDTH)).astype(BF16)


def _run_inproj(x2, scale, shift, norm_w, w_in, qnw, knw, row_tab, col_tab, kdec,
                *, batch, seq, tm):
    nt = seq // tm
    rows = batch * seq
    row_map = lambda t, b: (b * nt + t, 0)
    const2 = lambda t, b: (0, 0)
    vec_spec = pl.BlockSpec((1, 1, D_MODEL), lambda t, b: (b, 0, 0))
    t_map = lambda t, b: (b, t, 0, 0)

    def row_out(width):
        return pl.BlockSpec((tm, width), row_map)

    return pl.pallas_call(
        _inproj_kernel,
        grid=(nt, batch),
        in_specs=[
            pl.BlockSpec((tm, D_MODEL), row_map),
            vec_spec, vec_spec,
            pl.BlockSpec((1, D_MODEL), const2),
            pl.BlockSpec((D_MODEL, IN_WIDTH), const2),
            pl.BlockSpec((1, LANES), const2),
            pl.BlockSpec((1, LANES), const2),
            pl.BlockSpec((4, tm // GRID_W, LANES), lambda t, b: (0, t, 0)),
            pl.BlockSpec((4, tm, LANES), lambda t, b: (0, 0, 0)),
            pl.BlockSpec((N_DIR, tm), const2),
        ],
        out_specs=[
            pl.BlockSpec((1, 1, ATT_WIDTH, tm), t_map),
            pl.BlockSpec((ATT_KV_HEADS, tm, LANES), lambda t, b: (0, b * nt + t, 0)),
            pl.BlockSpec((1, 1, ATT_KV_HEADS, VT_ROWS, tm), lambda t, b: (b, t, 0, 0, 0)),
            row_out(ATT_WIDTH),
            row_out(RET_WIDTH),
            pl.BlockSpec((1, 1, RET_WIDTH, tm), t_map),
            row_out(RET_WIDTH),
            row_out(RET_WIDTH),
            pl.BlockSpec((1, 1, N_DIR, RET_HEAD_DIM, RET_HEAD_DIM), lambda t, b: (b, t, 0, 0, 0)),
        ],
        out_shape=[
            jax.ShapeDtypeStruct((batch, nt, ATT_WIDTH, tm), BF16),
            jax.ShapeDtypeStruct((ATT_KV_HEADS, rows, LANES), BF16),
            jax.ShapeDtypeStruct((batch, nt, ATT_KV_HEADS, VT_ROWS, tm), BF16),
            jax.ShapeDtypeStruct((rows, ATT_WIDTH), BF16),
            jax.ShapeDtypeStruct((rows, RET_WIDTH), BF16),
            jax.ShapeDtypeStruct((batch, nt, RET_WIDTH, tm), BF16),
            jax.ShapeDtypeStruct((rows, RET_WIDTH), BF16),
            jax.ShapeDtypeStruct((rows, RET_WIDTH), BF16),
            jax.ShapeDtypeStruct((batch, nt, N_DIR, RET_HEAD_DIM, RET_HEAD_DIM), F32),
        ],
        compiler_params=pltpu.CompilerParams(dimension_semantics=("arbitrary", "arbitrary"),
                                             vmem_limit_bytes=VMEM_LIMIT),
        name="inproj",
    )(x2, scale, shift, norm_w, w_in, qnw, knw, row_tab, col_tab, kdec)


def _scan_step(first, uc_ref, uf_ref, ub_ref, cdec_ref, sf_o, sb_o, st):
    nh = RET_HEADS
    cdec = cdec_ref[...]

    @pl.when(first)
    def _():
        st[...] = uc_ref[0, 0]

    for h in range(nh):
        s = st[h]
        sf_o[0, 0, 0, h] = s.astype(BF16)
        st[h] = cdec[h:h + 1, :] * s + uf_ref[0, 0, h]
        s = st[nh + h]
        sb_o[0, 0, 0, h] = s.astype(BF16)
        st[nh + h] = cdec[nh + h:nh + h + 1, :] * s + ub_ref[0, 0, h]


def _attn_kernel(qT_ref, kc_ref, vTc_ref, k_ref, vT_ref, ga_ref, uc_ref, uf_ref, ub_ref, cdec_ref,
                 o_ref, sf_o, sb_o, st, *, tq, tk, n_kv, bounded):
    _scan_step(pl.program_id(2) == 0, uc_ref, uf_ref, ub_ref, cdec_ref, sf_o, sb_o, st)
    hd = ATT_HEAD_DIM
    qT = qT_ref[0, 0]
    wq = jnp.concatenate(
        [jnp.concatenate([qT[h * hd:(h + 1) * hd, :]] * ATT_KV_HEADS, axis=0)
         for h in range(ATT_GROUP)], axis=1)
    cols = ATT_GROUP * tq

    if bounded:
        n_ct = cols // MXU_TILE
        kv = [(kc_ref[0], vTc_ref[0, 0, 0])]
        kv += [(k_ref[0, j * tk:(j + 1) * tk, :], vT_ref[0, j, 0]) for j in range(n_kv)]
        units = [(b, c) for b in range(len(kv)) for c in range(n_ct)]
        accs = [jnp.zeros((VT_ROWS, MXU_TILE), F32) for _ in range(n_ct)]
        pending = []

        def value_matmul(b, c, sT):
            accs[c] = accs[c] + jnp.dot(kv[b][1], jnp.exp2(sT).astype(BF16),
                                        preferred_element_type=F32)

        for b, c in units:
            sT = jnp.dot(kv[b][0], wq[:, c * MXU_TILE:(c + 1) * MXU_TILE],
                         preferred_element_type=F32)
            pending.append((b, c, sT))
            if len(pending) > PV_LAG:
                value_matmul(*pending.pop(0))
        for item in pending:
            value_matmul(*item)
        acc = jnp.concatenate(accs, axis=1)
    else:
        def block(carry, k, vT):
            m, acc = carry
            sT = jnp.dot(k, wq, preferred_element_type=F32)
            m_new = jnp.maximum(m, jnp.max(sT, axis=0, keepdims=True))
            p = jnp.exp2(sT - m_new).astype(BF16)
            acc = jnp.exp2(m - m_new) * acc + jnp.dot(vT, p, preferred_element_type=F32)
            return m_new, acc

        carry = (jnp.full((1, cols), -1e30, F32), jnp.zeros((VT_ROWS, cols), F32))
        carry = block(carry, kc_ref[0], vTc_ref[0, 0, 0])

        def body(j, carry):
            off = pl.multiple_of(j * tk, tk)
            return block(carry, k_ref[0, pl.ds(off, tk), :], vT_ref[0, j, 0])

        acc = lax.fori_loop(0, n_kv, body, carry)[1]
    oT = acc[0:hd, :] * (1.0 / acc[hd:hd + 1, :])
    oT = jnp.concatenate([oT, jnp.zeros((LANES - hd, cols), F32)], axis=0)
    out = jnp.concatenate([oT[:, h * tq:(h + 1) * tq].T[:, 0:hd] for h in range(ATT_GROUP)], axis=1)
    o_ref[...] = (out * ga_ref[...].astype(F32)).astype(BF16)


def _run_attn(qT, kz_c, vT_c, kz, vT, ga, u_c, u, cdec, *, batch, seq, ctx_len, bounded):
    tq = TILE
    nq = seq // tq
    n_kv = seq // TILE
    gw = ATT_GROUP * ATT_HEAD_DIM
    hd = RET_HEAD_DIM
    s_blk = (1, 1, RET_HEADS, hd, hd)
    s_out_blk = (1,) + s_blk
    s_shape = jax.ShapeDtypeStruct((ATT_KV_HEADS, batch, nq, RET_HEADS, hd, hd), BF16)
    kern = functools.partial(_attn_kernel, tq=tq, tk=TILE, n_kv=n_kv, bounded=bounded)
    return pl.pallas_call(
        kern,
        grid=(batch, ATT_KV_HEADS, nq),
        in_specs=[
            pl.BlockSpec((1, 1, gw, tq), lambda b, g, i: (b, i, g, 0)),
            pl.BlockSpec((1, ctx_len, LANES), lambda b, g, i: (g, b, 0)),
            pl.BlockSpec((1, 1, 1, VT_ROWS, ctx_len), lambda b, g, i: (b, 0, g, 0, 0)),
            pl.BlockSpec((1, seq, LANES), lambda b, g, i: (g, b, 0)),
            pl.BlockSpec((1, n_kv, 1, VT_ROWS, TILE), lambda b, g, i: (b, 0, g, 0, 0)),
            pl.BlockSpec((tq, gw), lambda b, g, i: (b * nq + i, g)),
            pl.BlockSpec((1, 1, N_DIR, hd, hd), lambda b, g, i: (b, 0, 0, 0, 0)),
            pl.BlockSpec(s_blk, lambda b, g, i: (b, i, 0, 0, 0)),
            pl.BlockSpec(s_blk, lambda b, g, i: (b, nq - 1 - i, 1, 0, 0)),
            pl.BlockSpec((N_DIR, LANES), lambda b, g, i: (0, 0)),
        ],
        out_specs=[
            pl.BlockSpec((tq, gw), lambda b, g, i: (b * nq + i, g)),
            pl.BlockSpec(s_out_blk, lambda b, g, i: (g, b, i, 0, 0, 0)),
            pl.BlockSpec(s_out_blk, lambda b, g, i: (g, b, nq - 1 - i, 0, 0, 0)),
        ],
        out_shape=[jax.ShapeDtypeStruct((batch * seq, ATT_WIDTH), BF16), s_shape, s_shape],
        scratch_shapes=[pltpu.VMEM((N_DIR, hd, hd), F32)],
        compiler_params=pltpu.CompilerParams(
            dimension_semantics=("arbitrary", "arbitrary", "arbitrary"),
            vmem_limit_bytes=VMEM_LIMIT),
        name="attn_bounded" if bounded else "attn_online",
    )(qT, kz_c, vT_c, kz, vT, ga, u_c, u, u, cdec)


def _epilogue_kernel(att_ref, qr_ref, krT_ref, vr_ref, gr_ref, x_ref, sf_ref, sb_ref,
                     dmask_ref, qdec_ref, gnw_ref, gate_ref, wout_ref, fnw_ref, y_ref, ret_scr):
    nh = RET_HEADS
    for h in range(nh):
        hs = slice(h * RET_HEAD_DIM, (h + 1) * RET_HEAD_DIM)
        q = qr_ref[:, hs]
        qf = q.astype(F32)
        a = jnp.dot(q, krT_ref[0, 0, hs, :], preferred_element_type=F32) * dmask_ref[h]
        o = (jnp.dot(a.astype(BF16), vr_ref[:, hs], preferred_element_type=F32)
             + jnp.dot((qf * qdec_ref[h]).astype(BF16), sf_ref[0, 0, 0, h],
                       preferred_element_type=F32)
             + jnp.dot((qf * qdec_ref[nh + h]).astype(BF16), sb_ref[0, 0, 0, h],
                       preferred_element_type=F32))
        mu = jnp.mean(o, axis=-1, keepdims=True)
        d = o - mu
        var = jnp.mean(d * d, axis=-1, keepdims=True)
        yh = d * lax.rsqrt(var + NORM_EPS) * gnw_ref[:, hs] * gr_ref[:, hs].astype(F32)
        ret_scr[:, hs] = yh.astype(BF16)
    proj = (jnp.dot(att_ref[...], wout_ref[0:ATT_WIDTH, :], preferred_element_type=F32)
            + jnp.dot(ret_scr[...], wout_ref[ATT_WIDTH:, :], preferred_element_type=F32))
    z = x_ref[...] + gate_ref[0] * proj
    ms = jnp.mean(z * z, axis=-1, keepdims=True)
    y_ref[...] = z * lax.rsqrt(ms + NORM_EPS) * fnw_ref[...]


def _run_epilogue(att, qr, krT, vr, gr, x2, sf, sb, dmask, qdec, gnw, gate, w_out, fnw,
                  *, batch, seq):
    nt = seq // TILE
    row_map = lambda b, t: (b * nt + t, 0)
    const2 = lambda b, t: (0, 0)
    const3 = lambda b, t: (0, 0, 0)
    s_spec = pl.BlockSpec((1, 1, 1, RET_HEADS, RET_HEAD_DIM, RET_HEAD_DIM),
                          lambda b, t: (0, b, t, 0, 0, 0))
    return pl.pallas_call(
        _epilogue_kernel,
        grid=(batch, nt),
        in_specs=[
            pl.BlockSpec((TILE, ATT_WIDTH), row_map),
            pl.BlockSpec((TILE, RET_WIDTH), row_map),
            pl.BlockSpec((1, 1, RET_WIDTH, TILE), lambda b, t: (b, t, 0, 0)),
            pl.BlockSpec((TILE, RET_WIDTH), row_map),
            pl.BlockSpec((TILE, RET_WIDTH), row_map),
            pl.BlockSpec((TILE, D_MODEL), row_map),
            s_spec, s_spec,
            pl.BlockSpec((RET_HEADS, TILE, TILE), const3),
            pl.BlockSpec((N_DIR, TILE, LANES), const3),
            pl.BlockSpec((1, RET_WIDTH), const2),
            pl.BlockSpec((1, 1, D_MODEL), lambda b, t: (b, 0, 0)),
            pl.BlockSpec((ATT_WIDTH + RET_WIDTH, D_MODEL), const2),
            pl.BlockSpec((1, D_MODEL), const2),
        ],
        out_specs=pl.BlockSpec((TILE, D_MODEL), row_map),
        out_shape=jax.ShapeDtypeStruct((batch * seq, D_MODEL), F32),
        scratch_shapes=[pltpu.VMEM((TILE, RET_WIDTH), BF16)],
        compiler_params=pltpu.CompilerParams(dimension_semantics=("arbitrary", "arbitrary"),
                                             vmem_limit_bytes=VMEM_LIMIT),
        name="epilogue",
    )(att, qr, krT, vr, gr, x2, sf, sb, dmask, qdec, gnw, gate, w_out, fnw)


def _rope_tables(seq):
    rows = jnp.arange(seq // GRID_W, dtype=F32)
    cols = jnp.arange(GRID_W, dtype=F32)

    def parts(head_dim):
        n_axis = head_dim // 4
        inv_freq = ROPE_THETA ** (-jnp.arange(n_axis, dtype=F32) / n_axis)
        reps = LANES // head_dim

        def lanes(pos, first):
            ang = pos[:, None] * inv_freq
            z = jnp.zeros_like(ang)
            c = jnp.concatenate([jnp.cos(ang), z] if first else [z, jnp.cos(ang)], axis=-1)
            s = jnp.concatenate([jnp.sin(ang), z] if first else [z, jnp.sin(ang)], axis=-1)
            return (jnp.tile(jnp.concatenate([c, c], axis=-1), (1, reps)),
                    jnp.tile(jnp.concatenate([-s, s], axis=-1), (1, reps)))

        return lanes(rows, True), lanes(cols, False)

    (rca, rsa), (cca, csa) = parts(ATT_HEAD_DIM)
    (rcr, rsr), (ccr, csr) = parts(RET_HEAD_DIM)
    return jnp.stack([rca, rsa, rcr, rsr]), jnp.stack([cca, csa, ccr, csr])


def kernel(x, c, ctx, c_ctx, norm_w, w_mod, b_mod, w_in, q_norm_w, k_norm_w,
           ret_decay_fwd, ret_decay_bwd, ret_gn_w, w_out, final_norm_w):
    batch, seq, d = x.shape
    ctx_len = ctx.shape[1]
    depth = norm_w.shape[0]
    assert d == D_MODEL and depth == 1 and seq % TILE == 0
    assert ctx_len % LANES == 0 and ctx_len <= TILE
    assert batch + 1 <= MOD_ROWS and w_in.shape[-1] == IN_WIDTH
    layer = 0

    cc = jnp.zeros((MOD_ROWS, D_MODEL), F32).at[:batch].set(c).at[batch].set(c_ctx)
    dl = jnp.broadcast_to(
        jnp.concatenate([ret_decay_fwd[layer], ret_decay_bwd[layer]]).astype(F32)[:, None],
        (N_DIR, LANES))
    mod, dmask, qdec, kdec, kdec_c, cdec = _run_mod(
        cc, w_mod[layer], b_mod[layer].reshape(3, 1, D_MODEL), dl, ctx_len=ctx_len)
    shift, scale, gate = mod[0], mod[1], mod[2]

    w_in_b = w_in[layer].astype(BF16)
    nw = norm_w[layer].reshape(1, D_MODEL)
    qnw = jnp.tile(q_norm_w[layer], LANES // ATT_HEAD_DIM).reshape(1, LANES)
    knw = jnp.tile(k_norm_w[layer], LANES // ATT_HEAD_DIM).reshape(1, LANES)

    ctx_rows = ctx_len // GRID_W
    one = jnp.ones((ctx_rows, LANES), F32)
    zero = jnp.zeros((ctx_rows, LANES), F32)
    row_tab_c = jnp.stack([one, zero, one, zero])
    col_tab_c = jnp.zeros((4, ctx_len, LANES), F32)
    sc_c = jnp.broadcast_to(scale[batch].reshape(1, 1, D_MODEL), (batch, 1, D_MODEL))
    sh_c = jnp.broadcast_to(shift[batch].reshape(1, 1, D_MODEL), (batch, 1, D_MODEL))
    (_, kz_c, vT_c, _, _, _, _, _, u_c) = _run_inproj(
        ctx.reshape(batch * ctx_len, d), sc_c, sh_c, nw, w_in_b, qnw, knw,
        row_tab_c, col_tab_c, kdec_c, batch=batch, seq=ctx_len, tm=ctx_len)

    x2 = x.reshape(batch * seq, d)
    sc_l = scale[:batch].reshape(batch, 1, D_MODEL)
    sh_l = shift[:batch].reshape(batch, 1, D_MODEL)
    row_tab, col_tab = _rope_tables(seq)
    col_tab = jnp.tile(col_tab, (1, TILE // GRID_W, 1))
    qT, kz, vT, ga, qr, krT, vr, gr, u = _run_inproj(
        x2, sc_l, sh_l, nw, w_in_b, qnw, knw, row_tab, col_tab, kdec,
        batch=batch, seq=seq, tm=TILE)

    score_bound = (ATT_HEAD_DIM * Q_SCALE_LOG2 * BF16_ROUND_UP ** 2
                   * jnp.max(jnp.abs(q_norm_w[layer])) * jnp.max(jnp.abs(k_norm_w[layer])))
    attn = functools.partial(_run_attn, batch=batch, seq=seq, ctx_len=ctx_len)
    att, sf, sb = lax.cond(score_bound <= SCORE_LOG2_LIMIT,
                           functools.partial(attn, bounded=True),
                           functools.partial(attn, bounded=False),
                           qT, kz_c, vT_c, kz, vT, ga, u_c, u, cdec)
    y = _run_epilogue(att, qr, krT, vr, gr, x2, sf, sb, dmask, qdec,
                      ret_gn_w[layer].reshape(1, RET_WIDTH),
                      gate[:batch].reshape(batch, 1, D_MODEL),
                      w_out[layer].astype(BF16), final_norm_w.reshape(1, D_MODEL),
                      batch=batch, seq=seq)
    return y.reshape(batch, seq, d)
```

```python
import functools

import jax
import jax.numpy as jnp
from jax import lax
from jax.experimental import pallas as pl
from jax.experimental.pallas import tpu as pltpu

F32 = jnp.float32
BF16 = jnp.bfloat16

D_MODEL = 1024
GRID_W = 64
ATT_HEADS = 8
ATT_KV_HEADS = 2
ATT_GROUP = ATT_HEADS // ATT_KV_HEADS
ATT_HEAD_DIM = 64
ATT_WIDTH = ATT_HEADS * ATT_HEAD_DIM
ATT_KV_WIDTH = ATT_KV_HEADS * ATT_HEAD_DIM
RET_HEADS = 4
RET_HEAD_DIM = 128
RET_WIDTH = RET_HEADS * RET_HEAD_DIM
ROPE_THETA = 10000.0
NORM_EPS = 1e-6
LANES = 128
TILE = 512
MOD_ROWS = 8
VT_ROWS = 80
N_DIR = 2 * RET_HEADS
MXU_TILE = 256
PV_LAG = 2

O_QA = 0
O_KA = O_QA + ATT_WIDTH
O_VA = O_KA + ATT_KV_WIDTH
O_GA = O_VA + ATT_KV_WIDTH
O_QR = O_GA + ATT_WIDTH
O_KR = O_QR + RET_WIDTH
O_VR = O_KR + RET_WIDTH
O_GR = O_VR + RET_WIDTH
IN_WIDTH = O_GR + RET_WIDTH

VMEM_LIMIT = 48 * 1024 * 1024

LOG2_E = 1.4426950408889634
Q_SCALE_LOG2 = ATT_HEAD_DIM ** -0.5 * LOG2_E
SCORE_LOG2_LIMIT = 100.0
BF16_ROUND_UP = 1.0 + 2.0 ** -8


def _silu(x):
    return x * (1.0 / (1.0 + jnp.exp(-x)))


def _silu_tanh(x):
    hx = 0.5 * x
    return hx + hx * jnp.tanh(hx)


def _mod_kernel(cc_ref, w_ref, b_ref, dl_ref,
                mod_ref, dmask_ref, qdec_ref, kdec_ref, kdec_c_ref, cdec_ref):
    sc = _silu(cc_ref[...])
    mod_ref[0] = jnp.dot(sc, w_ref[...], preferred_element_type=F32,
                         precision=lax.Precision.HIGHEST) + b_ref[0]

    @pl.when(pl.program_id(0) == 0)
    def _():
        x = dl_ref[...]
        lg = jnp.minimum(x, 0.0) - jnp.log1p(jnp.exp(-jnp.abs(x)))
        n = float(TILE)
        ii = lax.broadcasted_iota(jnp.int32, (TILE, TILE), 0).astype(F32)
        jj = lax.broadcasted_iota(jnp.int32, (TILE, TILE), 1).astype(F32)
        d = ii - jj
        i_col = lax.broadcasted_iota(jnp.int32, (TILE, LANES), 0).astype(F32)
        for h in range(RET_HEADS):
            lf = lg[h:h + 1, :]
            lb = lg[RET_HEADS + h:RET_HEADS + h + 1, :]
            lf_t = jnp.concatenate([lf] * (TILE // LANES), axis=1)
            lb_t = jnp.concatenate([lb] * (TILE // LANES), axis=1)
            fwd = jnp.exp(lf_t * jnp.maximum(d, 0.0))
            bwd = jnp.exp(lb_t * jnp.maximum(-d, 0.0))
            dmask_ref[h] = jnp.where(d > 0, fwd, jnp.where(d < 0, bwd, 2.0))
            qdec_ref[h] = jnp.exp(lf * (i_col + 1.0))
            qdec_ref[RET_HEADS + h] = jnp.exp(lb * (n - i_col))
        for ref in (kdec_ref, kdec_c_ref):
            length = ref.shape[1]
            tok = lax.broadcasted_iota(jnp.int32, (N_DIR, length), 1).astype(F32)
            row = lax.broadcasted_iota(jnp.int32, (N_DIR, length), 0)
            lg_t = jnp.concatenate([lg] * (length // LANES), axis=1)
            ref[...] = jnp.where(row < RET_HEADS, jnp.exp(lg_t * (float(length) - 1.0 - tok)),
                                 jnp.exp(lg_t * tok))
        cdec_ref[...] = jnp.exp(lg * n)


def _run_mod(cc, w_mod, b_mod, dl, *, ctx_len):
    n3 = 3
    const2 = lambda j: (0, 0)
    const3 = lambda j: (0, 0, 0)
    return pl.pallas_call(
        _mod_kernel,
        grid=(n3,),
        in_specs=[
            pl.BlockSpec((MOD_ROWS, D_MODEL), const2),
            pl.BlockSpec((D_MODEL, D_MODEL), lambda j: (0, j)),
            pl.BlockSpec((1, 1, D_MODEL), lambda j: (j, 0, 0)),
            pl.BlockSpec((N_DIR, LANES), const2),
        ],
        out_specs=[
            pl.BlockSpec((1, MOD_ROWS, D_MODEL), lambda j: (j, 0, 0)),
            pl.BlockSpec((RET_HEADS, TILE, TILE), const3),
            pl.BlockSpec((N_DIR, TILE, LANES), const3),
            pl.BlockSpec((N_DIR, TILE), const2),
            pl.BlockSpec((N_DIR, ctx_len), const2),
            pl.BlockSpec((N_DIR, LANES), const2),
        ],
        out_shape=[
            jax.ShapeDtypeStruct((n3, MOD_ROWS, D_MODEL), F32),
            jax.ShapeDtypeStruct((RET_HEADS, TILE, TILE), F32),
            jax.ShapeDtypeStruct((N_DIR, TILE, LANES), F32),
            jax.ShapeDtypeStruct((N_DIR, TILE), F32),
            jax.ShapeDtypeStruct((N_DIR, ctx_len), F32),
            jax.ShapeDtypeStruct((N_DIR, LANES), F32),
        ],
        compiler_params=pltpu.CompilerParams(dimension_semantics=("arbitrary",),
                                             vmem_limit_bytes=VMEM_LIMIT),
        name="mod",
    )(cc, w_mod, b_mod, dl)


def _inproj_kernel(x_ref, sc_ref, sh_ref, nw_ref, w_ref, qnw_ref, knw_ref, rt_ref, ct_ref, kdec_ref,
                   qT_o, kz_o, vT_o, ga_o, qr_o, krT_o, vr_o, gr_o, u_o):
    tm = x_ref.shape[0]
    x = x_ref[...]
    ms = jnp.mean(x * x, axis=-1, keepdims=True)
    a = nw_ref[...] * (1.0 + sc_ref[0])
    hb = ((x * lax.rsqrt(ms + NORM_EPS)) * a + sh_ref[0]).astype(BF16)

    def proj(lo, width):
        return jnp.dot(hb, w_ref[:, lo:lo + width], preferred_element_type=F32)

    def rope_table(k):
        rt = rt_ref[k]
        rows = [jnp.broadcast_to(rt[i:i + 1, :], (GRID_W, LANES)) for i in range(tm // GRID_W)]
        return jnp.concatenate(rows, axis=0) + ct_ref[k]

    ca, sa, cr, sr = (rope_table(k) for k in range(4))
    lane = lax.broadcasted_iota(jnp.int32, (tm, LANES), 1)
    head_lo = lane < ATT_HEAD_DIM
    half_lo = (lane & (ATT_HEAD_DIM // 2)) == 0
    inv_hd = 1.0 / ATT_HEAD_DIM

    def att_norm_rope(v, w):
        sq = v * v
        s_lo = jnp.sum(jnp.where(head_lo, sq, 0.0), axis=-1, keepdims=True)
        s_hi = jnp.sum(jnp.where(head_lo, 0.0, sq), axis=-1, keepdims=True)
        r = jnp.where(head_lo, lax.rsqrt(s_lo * inv_hd + NORM_EPS),
                      lax.rsqrt(s_hi * inv_hd + NORM_EPS))
        vn = v * r * w
        rot = jnp.where(half_lo, pltpu.roll(vn, LANES - ATT_HEAD_DIM // 2, 1),
                        pltpu.roll(vn, ATT_HEAD_DIM // 2, 1))
        return vn * ca + rot * sa

    def ret_rope(v):
        return v * cr + pltpu.roll(v, RET_HEAD_DIM // 2, 1) * sr

    qnw = qnw_ref[...]
    qa = proj(O_QA, ATT_WIDTH)
    for c in range(ATT_WIDTH // LANES):
        cs = slice(c * LANES, (c + 1) * LANES)
        qT = (att_norm_rope(qa[:, cs], qnw) * Q_SCALE_LOG2).T
        qT_o[0, 0, cs, :] = qT.astype(BF16)

    ka = att_norm_rope(proj(O_KA, ATT_KV_WIDTH), knw_ref[...])
    kz_o[0] = jnp.where(head_lo, ka, 0.0).astype(BF16)
    kz_o[1] = jnp.where(head_lo, 0.0, ka).astype(BF16)

    va = proj(O_VA, ATT_KV_WIDTH)
    ones_col = jnp.where(lane == ATT_HEAD_DIM, 1.0, 0.0)
    for g in range(ATT_KV_HEADS):
        vg = va if g == 0 else pltpu.roll(va, ATT_HEAD_DIM, 1)
        vT_o[0, 0, g] = jnp.where(head_lo, vg, ones_col).T[0:VT_ROWS, :].astype(BF16)

    ga_o[...] = _silu_tanh(proj(O_GA, ATT_WIDTH)).astype(BF16)

    vr = proj(O_VR, RET_WIDTH).astype(BF16)
    vr_o[...] = vr
    qr = proj(O_QR, RET_WIDTH)
    kr = proj(O_KR, RET_WIDTH)
    kdec = kdec_ref[...]
    krT = []
    for h in range(RET_HEADS):
        hs = slice(h * RET_HEAD_DIM, (h + 1) * RET_HEAD_DIM)
        qr_o[:, hs] = ret_rope(qr[:, hs]).astype(BF16)
        krT.append((ret_rope(kr[:, hs]) * (RET_HEAD_DIM ** -0.5)).T)
        krT_o[0, 0, hs, :] = krT[h].astype(BF16)

    gr_o[...] = _silu_tanh(proj(O_GR, RET_WIDTH)).astype(BF16)

    for h in range(RET_HEADS):
        hs = slice(h * RET_HEAD_DIM, (h + 1) * RET_HEAD_DIM)
        for row in (h, RET_HEADS + h):
            kd = (krT[h] * kdec[row:row + 1, :]).astype(BF16)
            u_o[0, 0, row] = jnp.dot(kd, vr[:, hs], preferred_element_type=F32)


def _run_inproj(x2, scale, shift, norm_w, w_in, qnw, knw, row_tab, col_tab, kdec,
                *, batch, seq, tm):
    nt = seq // tm
    rows = batch * seq
    row_map = lambda t, b: (b * nt + t, 0)
    const2 = lambda t, b: (0, 0)
    vec_spec = pl.BlockSpec((1, 1, D_MODEL), lambda t, b: (b, 0, 0))
    t_map = lambda t, b: (b, t, 0, 0)

    def row_out(width):
        return pl.BlockSpec((tm, width), row_map)

    return pl.pallas_call(
        _inproj_kernel,
        grid=(nt, batch),
        in_specs=[
            pl.BlockSpec((tm, D_MODEL), row_map),
            vec_spec, vec_spec,
            pl.BlockSpec((1, D_MODEL), const2),
            pl.BlockSpec((D_MODEL, IN_WIDTH), const2),
            pl.BlockSpec((1, LANES), const2),
            pl.BlockSpec((1, LANES), const2),
            pl.BlockSpec((4, tm // GRID_W, LANES), lambda t, b: (0, t, 0)),
            pl.BlockSpec((4, tm, LANES), lambda t, b: (0, 0, 0)),
            pl.BlockSpec((N_DIR, tm), const2),
        ],
        out_specs=[
            pl.BlockSpec((1, 1, ATT_WIDTH, tm), t_map),
            pl.BlockSpec((ATT_KV_HEADS, tm, LANES), lambda t, b: (0, b * nt + t, 0)),
            pl.BlockSpec((1, 1, ATT_KV_HEADS, VT_ROWS, tm), lambda t, b: (b, t, 0, 0, 0)),
            row_out(ATT_WIDTH),
            row_out(RET_WIDTH),
            pl.BlockSpec((1, 1, RET_WIDTH, tm), t_map),
            row_out(RET_WIDTH),
            row_out(RET_WIDTH),
            pl.BlockSpec((1, 1, N_DIR, RET_HEAD_DIM, RET_HEAD_DIM), lambda t, b: (b, t, 0, 0, 0)),
        ],
        out_shape=[
            jax.ShapeDtypeStruct((batch, nt, ATT_WIDTH, tm), BF16),
            jax.ShapeDtypeStruct((ATT_KV_HEADS, rows, LANES), BF16),
            jax.ShapeDtypeStruct((batch, nt, ATT_KV_HEADS, VT_ROWS, tm), BF16),
            jax.ShapeDtypeStruct((rows, ATT_WIDTH), BF16),
            jax.ShapeDtypeStruct((rows, RET_WIDTH), BF16),
            jax.ShapeDtypeStruct((batch, nt, RET_WIDTH, tm), BF16),
            jax.ShapeDtypeStruct((rows, RET_WIDTH), BF16),
            jax.ShapeDtypeStruct((rows, RET_WIDTH), BF16),
            jax.ShapeDtypeStruct((batch, nt, N_DIR, RET_HEAD_DIM, RET_HEAD_DIM), F32),
        ],
        compiler_params=pltpu.CompilerParams(dimension_semantics=("arbitrary", "arbitrary"),
                                             vmem_limit_bytes=VMEM_LIMIT),
        name="inproj",
    )(x2, scale, shift, norm_w, w_in, qnw, knw, row_tab, col_tab, kdec)


def _scan_step(first, uc_ref, uf_ref, ub_ref, cdec_ref, sf_o, sb_o, st):
    nh = RET_HEADS
    cdec = cdec_ref[...]

    @pl.when(first)
    def _():
        st[...] = uc_ref[0, 0]

    for h in range(nh):
        s = st[h]
        sf_o[0, 0, 0, h] = s.astype(BF16)
        st[h] = cdec[h:h + 1, :] * s + uf_ref[0, 0, h]
        s = st[nh + h]
        sb_o[0, 0, 0, h] = s.astype(BF16)
        st[nh + h] = cdec[nh + h:nh + h + 1, :] * s + ub_ref[0, 0, h]


def _attn_kernel(qT_ref, kc_ref, vTc_ref, k_ref, vT_ref, ga_ref, uc_ref, uf_ref, ub_ref, cdec_ref,
                 o_ref, sf_o, sb_o, st, *, tq, tk, n_kv, bounded):
    _scan_step(pl.program_id(2) == 0, uc_ref, uf_ref, ub_ref, cdec_ref, sf_o, sb_o, st)
    hd = ATT_HEAD_DIM
    qT = qT_ref[0, 0]
    wq = jnp.concatenate(
        [jnp.concatenate([qT[h * hd:(h + 1) * hd, :]] * ATT_KV_HEADS, axis=0)
         for h in range(ATT_GROUP)], axis=1)
    cols = ATT_GROUP * tq

    if bounded:
        n_ct = cols // MXU_TILE
        kv = [(kc_ref[0], vTc_ref[0, 0, 0])]
        kv += [(k_ref[0, j * tk:(j + 1) * tk, :], vT_ref[0, j, 0]) for j in range(n_kv)]
        units = [(b, c) for b in range(len(kv)) for c in range(n_ct)]
        accs = [jnp.zeros((VT_ROWS, MXU_TILE), F32) for _ in range(n_ct)]
        pending = []

        def value_matmul(b, c, sT):
            accs[c] = accs[c] + jnp.dot(kv[b][1], jnp.exp2(sT).astype(BF16),
                                        preferred_element_type=F32)

        for b, c in units:
            sT = jnp.dot(kv[b][0], wq[:, c * MXU_TILE:(c + 1) * MXU_TILE],
                         preferred_element_type=F32)
            pending.append((b, c, sT))
            if len(pending) > PV_LAG:
                value_matmul(*pending.pop(0))
        for item in pending:
            value_matmul(*item)
        acc = jnp.concatenate(accs, axis=1)
    else:
        def block(carry, k, vT):
            m, acc = carry
            sT = jnp.dot(k, wq, preferred_element_type=F32)
            m_new = jnp.maximum(m, jnp.max(sT, axis=0, keepdims=True))
            p = jnp.exp2(sT - m_new).astype(BF16)
            acc = jnp.exp2(m - m_new) * acc + jnp.dot(vT, p, preferred_element_type=F32)
            return m_new, acc

        carry = (jnp.full((1, cols), -1e30, F32), jnp.zeros((VT_ROWS, cols), F32))
        carry = block(carry, kc_ref[0], vTc_ref[0, 0, 0])

        def body(j, carry):
            off = pl.multiple_of(j * tk, tk)
            return block(carry, k_ref[0, pl.ds(off, tk), :], vT_ref[0, j, 0])

        acc = lax.fori_loop(0, n_kv, body, carry)[1]
    oT = acc[0:hd, :] * (1.0 / acc[hd:hd + 1, :])
    oT = jnp.concatenate([oT, jnp.zeros((LANES - hd, cols), F32)], axis=0)
    out = jnp.concatenate([oT[:, h * tq:(h + 1) * tq].T[:, 0:hd] for h in range(ATT_GROUP)], axis=1)
    o_ref[...] = (out * ga_ref[...].astype(F32)).astype(BF16)


def _run_attn(qT, kz_c, vT_c, kz, vT, ga, u_c, u, cdec, *, batch, seq, ctx_len, bounded):
    tq = TILE
    nq = seq // tq
    n_kv = seq // TILE
    gw = ATT_GROUP * ATT_HEAD_DIM
    hd = RET_HEAD_DIM
    s_blk = (1, 1, RET_HEADS, hd, hd)
    s_out_blk = (1,) + s_blk
    s_shape = jax.ShapeDtypeStruct((ATT_KV_HEADS, batch, nq, RET_HEADS, hd, hd), BF16)
    kern = functools.partial(_attn_kernel, tq=tq, tk=TILE, n_kv=n_kv, bounded=bounded)
    return pl.pallas_call(
        kern,
        grid=(batch, ATT_KV_HEADS, nq),
        in_specs=[
            pl.BlockSpec((1, 1, gw, tq), lambda b, g, i: (b, i, g, 0)),
            pl.BlockSpec((1, ctx_len, LANES), lambda b, g, i: (g, b, 0)),
            pl.BlockSpec((1, 1, 1, VT_ROWS, ctx_len), lambda b, g, i: (b, 0, g, 0, 0)),
            pl.BlockSpec((1, seq, LANES), lambda b, g, i: (g, b, 0)),
            pl.BlockSpec((1, n_kv, 1, VT_ROWS, TILE), lambda b, g, i: (b, 0, g, 0, 0)),
            pl.BlockSpec((tq, gw), lambda b, g, i: (b * nq + i, g)),
            pl.BlockSpec((1, 1, N_DIR, hd, hd), lambda b, g, i: (b, 0, 0, 0, 0)),
            pl.BlockSpec(s_blk, lambda b, g, i: (b, i, 0, 0, 0)),
            pl.BlockSpec(s_blk, lambda b, g, i: (b, nq - 1 - i, 1, 0, 0)),
            pl.BlockSpec((N_DIR, LANES), lambda b, g, i: (0, 0)),
        ],
        out_specs=[
            pl.BlockSpec((tq, gw), lambda b, g, i: (b * nq + i, g)),
            pl.BlockSpec(s_out_blk, lambda b, g, i: (g, b, i, 0, 0, 0)),
            pl.BlockSpec(s_out_blk, lambda b, g, i: (g, b, nq - 1 - i, 0, 0, 0)),
        ],
        out_shape=[jax.ShapeDtypeStruct((batch * seq, ATT_WIDTH), BF16), s_shape, s_shape],
        scratch_shapes=[pltpu.VMEM((N_DIR, hd, hd), F32)],
        compiler_params=pltpu.CompilerParams(
            dimension_semantics=("arbitrary", "arbitrary", "arbitrary"),
            vmem_limit_bytes=VMEM_LIMIT),
        name="attn_bounded" if bounded else "attn_online",
    )(qT, kz_c, vT_c, kz, vT, ga, u_c, u, u, cdec)


def _epilogue_kernel(att_ref, qr_ref, krT_ref, vr_ref, gr_ref, x_ref, sf_ref, sb_ref,
                     dmask_ref, qdec_ref, gnw_ref, gate_ref, wout_ref, fnw_ref, y_ref, ret_scr):
    nh = RET_HEADS
    heads = [slice(h * RET_HEAD_DIM, (h + 1) * RET_HEAD_DIM) for h in range(nh)]
    a = [jnp.dot(qr_ref[:, hs], krT_ref[0, 0, hs, :], preferred_element_type=F32) * dmask_ref[h]
         for h, hs in enumerate(heads)]
    proj_att = jnp.dot(att_ref[...], wout_ref[0:ATT_WIDTH, :], preferred_element_type=F32)
    for h, hs in enumerate(heads):
        qf = qr_ref[:, hs].astype(F32)
        o = (jnp.dot((qf * qdec_ref[h]).astype(BF16), sf_ref[0, 0, 0, h],
                     preferred_element_type=F32)
             + jnp.dot((qf * qdec_ref[nh + h]).astype(BF16), sb_ref[0, 0, 0, h],
                       preferred_element_type=F32)
             + jnp.dot(a[h].astype(BF16), vr_ref[:, hs], preferred_element_type=F32))
        mu = jnp.mean(o, axis=-1, keepdims=True)
        d = o - mu
        var = jnp.mean(d * d, axis=-1, keepdims=True)
        yh = d * lax.rsqrt(var + NORM_EPS) * gnw_ref[:, hs] * gr_ref[:, hs].astype(F32)
        ret_scr[:, hs] = yh.astype(BF16)
    proj = proj_att + jnp.dot(ret_scr[...], wout_ref[ATT_WIDTH:, :], preferred_element_type=F32)
    z = x_ref[...] + gate_ref[0] * proj
    ms = jnp.mean(z * z, axis=-1, keepdims=True)
    y_ref[...] = z * lax.rsqrt(ms + NORM_EPS) * fnw_ref[...]


def _run_epilogue(att, qr, krT, vr, gr, x2, sf, sb, dmask, qdec, gnw, gate, w_out, fnw,
                  *, batch, seq):
    nt = seq // TILE
    row_map = lambda b, t: (b * nt + t, 0)
    const2 = lambda b, t: (0, 0)
    const3 = lambda b, t: (0, 0, 0)
    s_spec = pl.BlockSpec((1, 1, 1, RET_HEADS, RET_HEAD_DIM, RET_HEAD_DIM),
                          lambda b, t: (0, b, t, 0, 0, 0))
    return pl.pallas_call(
        _epilogue_kernel,
        grid=(batch, nt),
        in_specs=[
            pl.BlockSpec((TILE, ATT_WIDTH), row_map),
            pl.BlockSpec((TILE, RET_WIDTH), row_map),
            pl.BlockSpec((1, 1, RET_WIDTH, TILE), lambda b, t: (b, t, 0, 0)),
            pl.BlockSpec((TILE, RET_WIDTH), row_map),
            pl.BlockSpec((TILE, RET_WIDTH), row_map),
            pl.BlockSpec((TILE, D_MODEL), row_map),
            s_spec, s_spec,
            pl.BlockSpec((RET_HEADS, TILE, TILE), const3),
            pl.BlockSpec((N_DIR, TILE, LANES), const3),
            pl.BlockSpec((1, RET_WIDTH), const2),
            pl.BlockSpec((1, 1, D_MODEL), lambda b, t: (b, 0, 0)),
            pl.BlockSpec((ATT_WIDTH + RET_WIDTH, D_MODEL), const2),
            pl.BlockSpec((1, D_MODEL), const2),
        ],
        out_specs=pl.BlockSpec((TILE, D_MODEL), row_map),
        out_shape=jax.ShapeDtypeStruct((batch * seq, D_MODEL), F32),
        scratch_shapes=[pltpu.VMEM((TILE, RET_WIDTH), BF16)],
        compiler_params=pltpu.CompilerParams(dimension_semantics=("arbitrary", "arbitrary"),
                                             vmem_limit_bytes=VMEM_LIMIT),
        name="epilogue",
    )(att, qr, krT, vr, gr, x2, sf, sb, dmask, qdec, gnw, gate, w_out, fnw)


def _rope_tables(seq):
    rows = jnp.arange(seq // GRID_W, dtype=F32)
    cols = jnp.arange(GRID_W, dtype=F32)

    def parts(head_dim):
        n_axis = head_dim // 4
        inv_freq = ROPE_THETA ** (-jnp.arange(n_axis, dtype=F32) / n_axis)
        reps = LANES // head_dim

        def lanes(pos, first):
            ang = pos[:, None] * inv_freq
            z = jnp.zeros_like(ang)
            c = jnp.concatenate([jnp.cos(ang), z] if first else [z, jnp.cos(ang)], axis=-1)
            s = jnp.concatenate([jnp.sin(ang), z] if first else [z, jnp.sin(ang)], axis=-1)
            return (jnp.tile(jnp.concatenate([c, c], axis=-1), (1, reps)),
                    jnp.tile(jnp.concatenate([-s, s], axis=-1), (1, reps)))

        return lanes(rows, True), lanes(cols, False)

    (rca, rsa), (cca, csa) = parts(ATT_HEAD_DIM)
    (rcr, rsr), (ccr, csr) = parts(RET_HEAD_DIM)
    return jnp.stack([rca, rsa, rcr, rsr]), jnp.stack([cca, csa, ccr, csr])


def kernel(x, c, ctx, c_ctx, norm_w, w_mod, b_mod, w_in, q_norm_w, k_norm_w,
           ret_decay_fwd, ret_decay_bwd, ret_gn_w, w_out, final_norm_w):
    batch, seq, d = x.shape
    ctx_len = ctx.shape[1]
    depth = norm_w.shape[0]
    assert d == D_MODEL and depth == 1 and seq % TILE == 0
    assert ctx_len % LANES == 0 and ctx_len <= TILE
    assert batch + 1 <= MOD_ROWS and w_in.shape[-1] == IN_WIDTH
    layer = 0

    cc = jnp.zeros((MOD_ROWS, D_MODEL), F32).at[:batch].set(c).at[batch].set(c_ctx)
    dl = jnp.broadcast_to(
        jnp.concatenate([ret_decay_fwd[layer], ret_decay_bwd[layer]]).astype(F32)[:, None],
        (N_DIR, LANES))
    mod, dmask, qdec, kdec, kdec_c, cdec = _run_mod(
        cc, w_mod[layer], b_mod[layer].reshape(3, 1, D_MODEL), dl, ctx_len=ctx_len)
    shift, scale, gate = mod[0], mod[1], mod[2]

    w_in_b = w_in[layer].astype(BF16)
    nw = norm_w[layer].reshape(1, D_MODEL)
    qnw = jnp.tile(q_norm_w[layer], LANES // ATT_HEAD_DIM).reshape(1, LANES)
    knw = jnp.tile(k_norm_w[layer], LANES // ATT_HEAD_DIM).reshape(1, LANES)

    ctx_rows = ctx_len // GRID_W
    one = jnp.ones((ctx_rows, LANES), F32)
    zero = jnp.zeros((ctx_rows, LANES), F32)
    row_tab_c = jnp.stack([one, zero, one, zero])
    col_tab_c = jnp.zeros((4, ctx_len, LANES), F32)
    sc_c = jnp.broadcast_to(scale[batch].reshape(1, 1, D_MODEL), (batch, 1, D_MODEL))
    sh_c = jnp.broadcast_to(shift[batch].reshape(1, 1, D_MODEL), (batch, 1, D_MODEL))
    (_, kz_c, vT_c, _, _, _, _, _, u_c) = _run_inproj(
        ctx.reshape(batch * ctx_len, d), sc_c, sh_c, nw, w_in_b, qnw, knw,
        row_tab_c, col_tab_c, kdec_c, batch=batch, seq=ctx_len, tm=ctx_len)

    x2 = x.reshape(batch * seq, d)
    sc_l = scale[:batch].reshape(batch, 1, D_MODEL)
    sh_l = shift[:batch].reshape(batch, 1, D_MODEL)
    row_tab, col_tab = _rope_tables(seq)
    col_tab = jnp.tile(col_tab, (1, TILE // GRID_W, 1))
    qT, kz, vT, ga, qr, krT, vr, gr, u = _run_inproj(
        x2, sc_l, sh_l, nw, w_in_b, qnw, knw, row_tab, col_tab, kdec,
        batch=batch, seq=seq, tm=TILE)

    score_bound = (ATT_HEAD_DIM * Q_SCALE_LOG2 * BF16_ROUND_UP ** 2
                   * jnp.max(jnp.abs(q_norm_w[layer])) * jnp.max(jnp.abs(k_norm_w[layer])))
    attn = functools.partial(_run_attn, batch=batch, seq=seq, ctx_len=ctx_len)
    att, sf, sb = lax.cond(score_bound <= SCORE_LOG2_LIMIT,
                           functools.partial(attn, bounded=True),
                           functools.partial(attn, bounded=False),
                           qT, kz_c, vT_c, kz, vT, ga, u_c, u, cdec)
    y = _run_epilogue(att, qr, krT, vr, gr, x2, sf, sb, dmask, qdec,
                      ret_gn_w[layer].reshape(1, RET_WIDTH),
                      gate[:batch].reshape(batch, 1, D_MODEL),
                      w_out[layer].astype(BF16), final_norm_w.reshape(1, D_MODEL),
                      batch=batch, seq=seq)
    return y.reshape(batch, seq, d)
```

```python
import functools

import jax
import jax.numpy as jnp
from jax import lax
from jax.experimental import pallas as pl
from jax.experimental.pallas import tpu as pltpu

F32 = jnp.float32
BF16 = jnp.bfloat16

D_MODEL = 1024
GRID_W = 64
ATT_HEADS = 8
ATT_KV_HEADS = 2
ATT_GROUP = ATT_HEADS // ATT_KV_HEADS
ATT_HEAD_DIM = 64
ATT_WIDTH = ATT_HEADS * ATT_HEAD_DIM
ATT_KV_WIDTH = ATT_KV_HEADS * ATT_HEAD_DIM
RET_HEADS = 4
RET_HEAD_DIM = 128
RET_WIDTH = RET_HEADS * RET_HEAD_DIM
ROPE_THETA = 10000.0
NORM_EPS = 1e-6
LANES = 128
TILE = 512
MOD_ROWS = 8
VT_ROWS = ATT_HEAD_DIM
SUBLANES = 8
N_DIR = 2 * RET_HEADS
MXU_TILE = 256
PV_LAG = 2

O_QA = 0
O_KA = O_QA + ATT_WIDTH
O_VA = O_KA + ATT_KV_WIDTH
O_GA = O_VA + ATT_KV_WIDTH
O_QR = O_GA + ATT_WIDTH
O_KR = O_QR + RET_WIDTH
O_VR = O_KR + RET_WIDTH
O_GR = O_VR + RET_WIDTH
IN_WIDTH = O_GR + RET_WIDTH

VMEM_LIMIT = 48 * 1024 * 1024

LOG2_E = 1.4426950408889634
Q_SCALE_LOG2 = ATT_HEAD_DIM ** -0.5 * LOG2_E
SCORE_LOG2_LIMIT = 100.0
BF16_ROUND_UP = 1.0 + 2.0 ** -8


def _silu(x):
    return x * (1.0 / (1.0 + jnp.exp(-x)))


def _silu_tanh(x):
    hx = 0.5 * x
    return hx + hx * jnp.tanh(hx)


def _mod_kernel(cc_ref, w_ref, b_ref, dl_ref,
                mod_ref, dmask_ref, qdec_ref, kdec_ref, kdec_c_ref, cdec_ref):
    sc = _silu(cc_ref[...])
    mod_ref[0] = jnp.dot(sc, w_ref[...], preferred_element_type=F32,
                         precision=lax.Precision.HIGHEST) + b_ref[0]

    @pl.when(pl.program_id(0) == 0)
    def _():
        x = dl_ref[...]
        lg = jnp.minimum(x, 0.0) - jnp.log1p(jnp.exp(-jnp.abs(x)))
        n = float(TILE)
        ii = lax.broadcasted_iota(jnp.int32, (TILE, TILE), 0).astype(F32)
        jj = lax.broadcasted_iota(jnp.int32, (TILE, TILE), 1).astype(F32)
        d = ii - jj
        i_col = lax.broadcasted_iota(jnp.int32, (TILE, LANES), 0).astype(F32)
        for h in range(RET_HEADS):
            lf = lg[h:h + 1, :]
            lb = lg[RET_HEADS + h:RET_HEADS + h + 1, :]
            lf_t = jnp.concatenate([lf] * (TILE // LANES), axis=1)
            lb_t = jnp.concatenate([lb] * (TILE // LANES), axis=1)
            fwd = jnp.exp(lf_t * jnp.maximum(d, 0.0))
            bwd = jnp.exp(lb_t * jnp.maximum(-d, 0.0))
            dmask_ref[h] = jnp.where(d > 0, fwd, jnp.where(d < 0, bwd, 2.0))
            qdec_ref[h] = jnp.exp(lf * (i_col + 1.0))
            qdec_ref[RET_HEADS + h] = jnp.exp(lb * (n - i_col))
        for ref in (kdec_ref, kdec_c_ref):
            length = ref.shape[1]
            tok = lax.broadcasted_iota(jnp.int32, (N_DIR, length), 1).astype(F32)
            row = lax.broadcasted_iota(jnp.int32, (N_DIR, length), 0)
            lg_t = jnp.concatenate([lg] * (length // LANES), axis=1)
            ref[...] = jnp.where(row < RET_HEADS, jnp.exp(lg_t * (float(length) - 1.0 - tok)),
                                 jnp.exp(lg_t * tok))
        cdec_ref[...] = jnp.exp(lg * n)


def _run_mod(cc, w_mod, b_mod, dl, *, ctx_len):
    n3 = 3
    const2 = lambda j: (0, 0)
    const3 = lambda j: (0, 0, 0)
    return pl.pallas_call(
        _mod_kernel,
        grid=(n3,),
        in_specs=[
            pl.BlockSpec((MOD_ROWS, D_MODEL), const2),
            pl.BlockSpec((D_MODEL, D_MODEL), lambda j: (0, j)),
            pl.BlockSpec((1, 1, D_MODEL), lambda j: (j, 0, 0)),
            pl.BlockSpec((N_DIR, LANES), const2),
        ],
        out_specs=[
            pl.BlockSpec((1, MOD_ROWS, D_MODEL), lambda j: (j, 0, 0)),
            pl.BlockSpec((RET_HEADS, TILE, TILE), const3),
            pl.BlockSpec((N_DIR, TILE, LANES), const3),
            pl.BlockSpec((N_DIR, TILE), const2),
            pl.BlockSpec((N_DIR, ctx_len), const2),
            pl.BlockSpec((N_DIR, LANES), const2),
        ],
        out_shape=[
            jax.ShapeDtypeStruct((n3, MOD_ROWS, D_MODEL), F32),
            jax.ShapeDtypeStruct((RET_HEADS, TILE, TILE), F32),
            jax.ShapeDtypeStruct((N_DIR, TILE, LANES), F32),
            jax.ShapeDtypeStruct((N_DIR, TILE), F32),
            jax.ShapeDtypeStruct((N_DIR, ctx_len), F32),
            jax.ShapeDtypeStruct((N_DIR, LANES), F32),
        ],
        compiler_params=pltpu.CompilerParams(dimension_semantics=("arbitrary",),
                                             vmem_limit_bytes=VMEM_LIMIT),
        name="mod",
    )(cc, w_mod, b_mod, dl)


def _inproj_kernel(x_ref, sc_ref, sh_ref, nw_ref, w_ref, qnw_ref, knw_ref, rt_ref, ct_ref, kdec_ref,
                   qT_o, kz_o, vT_o, ga_o, qr_o, krT_o, vr_o, gr_o, u_o):
    tm = x_ref.shape[0]
    x = x_ref[...]
    ms = jnp.mean(x * x, axis=-1, keepdims=True)
    a = nw_ref[...] * (1.0 + sc_ref[0])
    hb = ((x * lax.rsqrt(ms + NORM_EPS)) * a + sh_ref[0]).astype(BF16)

    def proj(lo, width):
        return jnp.dot(hb, w_ref[:, lo:lo + width], preferred_element_type=F32)

    def rope_table(k):
        rt = rt_ref[k]
        rows = [jnp.broadcast_to(rt[i:i + 1, :], (GRID_W, LANES)) for i in range(tm // GRID_W)]
        return jnp.concatenate(rows, axis=0) + ct_ref[k]

    ca, sa, cr, sr = (rope_table(k) for k in range(4))
    lane = lax.broadcasted_iota(jnp.int32, (tm, LANES), 1)
    head_lo = lane < ATT_HEAD_DIM
    half_lo = (lane & (ATT_HEAD_DIM // 2)) == 0
    inv_hd = 1.0 / ATT_HEAD_DIM

    def att_norm_rope(v, w):
        sq = v * v
        s_lo = jnp.sum(jnp.where(head_lo, sq, 0.0), axis=-1, keepdims=True)
        s_hi = jnp.sum(jnp.where(head_lo, 0.0, sq), axis=-1, keepdims=True)
        r = jnp.where(head_lo, lax.rsqrt(s_lo * inv_hd + NORM_EPS),
                      lax.rsqrt(s_hi * inv_hd + NORM_EPS))
        vn = v * r * w
        rot = jnp.where(half_lo, pltpu.roll(vn, LANES - ATT_HEAD_DIM // 2, 1),
                        pltpu.roll(vn, ATT_HEAD_DIM // 2, 1))
        return vn * ca + rot * sa

    def ret_rope(v):
        return v * cr + pltpu.roll(v, RET_HEAD_DIM // 2, 1) * sr

    qnw = qnw_ref[...]
    qa = proj(O_QA, ATT_WIDTH)
    for c in range(ATT_WIDTH // LANES):
        cs = slice(c * LANES, (c + 1) * LANES)
        qT = (att_norm_rope(qa[:, cs], qnw) * Q_SCALE_LOG2).T
        qT_o[0, 0, cs, :] = qT.astype(BF16)

    ka = att_norm_rope(proj(O_KA, ATT_KV_WIDTH), knw_ref[...])
    kz_o[0] = jnp.where(head_lo, ka, 0.0).astype(BF16)
    kz_o[1] = jnp.where(head_lo, 0.0, ka).astype(BF16)

    va = proj(O_VA, ATT_KV_WIDTH)
    for g in range(ATT_KV_HEADS):
        vg = va if g == 0 else pltpu.roll(va, ATT_HEAD_DIM, 1)
        vT_o[0, 0, g] = jnp.where(head_lo, vg, 0.0).T[0:ATT_HEAD_DIM, :].astype(BF16)

    ga_o[...] = _silu_tanh(proj(O_GA, ATT_WIDTH)).astype(BF16)

    vr = proj(O_VR, RET_WIDTH).astype(BF16)
    vr_o[...] = vr
    qr = proj(O_QR, RET_WIDTH)
    kr = proj(O_KR, RET_WIDTH)
    kdec = kdec_ref[...]
    krT = []
    for h in range(RET_HEADS):
        hs = slice(h * RET_HEAD_DIM, (h + 1) * RET_HEAD_DIM)
        qr_o[:, hs] = ret_rope(qr[:, hs]).astype(BF16)
        krT.append((ret_rope(kr[:, hs]) * (RET_HEAD_DIM ** -0.5)).T)
        krT_o[0, 0, hs, :] = krT[h].astype(BF16)

    gr_o[...] = _silu_tanh(proj(O_GR, RET_WIDTH)).astype(BF16)

    for h in range(RET_HEADS):
        hs = slice(h * RET_HEAD_DIM, (h + 1) * RET_HEAD_DIM)
        for row in (h, RET_HEADS + h):
            kd = (krT[h] * kdec[row:row + 1, :]).astype(BF16)
            u_o[0, 0, row] = jnp.dot(kd, vr[:, hs], preferred_element_type=F32)


def _run_inproj(x2, scale, shift, norm_w, w_in, qnw, knw, row_tab, col_tab, kdec,
                *, batch, seq, tm):
    nt = seq // tm
    rows = batch * seq
    row_map = lambda t, b: (b * nt + t, 0)
    const2 = lambda t, b: (0, 0)
    vec_spec = pl.BlockSpec((1, 1, D_MODEL), lambda t, b: (b, 0, 0))
    t_map = lambda t, b: (b, t, 0, 0)

    def row_out(width):
        return pl.BlockSpec((tm, width), row_map)

    return pl.pallas_call(
        _inproj_kernel,
        grid=(nt, batch),
        in_specs=[
            pl.BlockSpec((tm, D_MODEL), row_map),
            vec_spec, vec_spec,
            pl.BlockSpec((1, D_MODEL), const2),
            pl.BlockSpec((D_MODEL, IN_WIDTH), const2),
            pl.BlockSpec((1, LANES), const2),
            pl.BlockSpec((1, LANES), const2),
            pl.BlockSpec((4, tm // GRID_W, LANES), lambda t, b: (0, t, 0)),
            pl.BlockSpec((4, tm, LANES), lambda t, b: (0, 0, 0)),
            pl.BlockSpec((N_DIR, tm), const2),
        ],
        out_specs=[
            pl.BlockSpec((1, 1, ATT_WIDTH, tm), t_map),
            pl.BlockSpec((ATT_KV_HEADS, tm, LANES), lambda t, b: (0, b * nt + t, 0)),
            pl.BlockSpec((1, 1, ATT_KV_HEADS, VT_ROWS, tm), lambda t, b: (b, t, 0, 0, 0)),
            row_out(ATT_WIDTH),
            row_out(RET_WIDTH),
            pl.BlockSpec((1, 1, RET_WIDTH, tm), t_map),
            row_out(RET_WIDTH),
            row_out(RET_WIDTH),
            pl.BlockSpec((1, 1, N_DIR, RET_HEAD_DIM, RET_HEAD_DIM), lambda t, b: (b, t, 0, 0, 0)),
        ],
        out_shape=[
            jax.ShapeDtypeStruct((batch, nt, ATT_WIDTH, tm), BF16),
            jax.ShapeDtypeStruct((ATT_KV_HEADS, rows, LANES), BF16),
            jax.ShapeDtypeStruct((batch, nt, ATT_KV_HEADS, VT_ROWS, tm), BF16),
            jax.ShapeDtypeStruct((rows, ATT_WIDTH), BF16),
            jax.ShapeDtypeStruct((rows, RET_WIDTH), BF16),
            jax.ShapeDtypeStruct((batch, nt, RET_WIDTH, tm), BF16),
            jax.ShapeDtypeStruct((rows, RET_WIDTH), BF16),
            jax.ShapeDtypeStruct((rows, RET_WIDTH), BF16),
            jax.ShapeDtypeStruct((batch, nt, N_DIR, RET_HEAD_DIM, RET_HEAD_DIM), F32),
        ],
        compiler_params=pltpu.CompilerParams(dimension_semantics=("arbitrary", "arbitrary"),
                                             vmem_limit_bytes=VMEM_LIMIT),
        name="inproj",
    )(x2, scale, shift, norm_w, w_in, qnw, knw, row_tab, col_tab, kdec)


def _scan_step(first, uc_ref, uf_ref, ub_ref, cdec_ref, sf_o, sb_o, st):
    nh = RET_HEADS
    cdec = cdec_ref[...]

    @pl.when(first)
    def _():
        st[...] = uc_ref[0, 0]

    for h in range(nh):
        s = st[h]
        sf_o[0, 0, 0, h] = s.astype(BF16)
        st[h] = cdec[h:h + 1, :] * s + uf_ref[0, 0, h]
        s = st[nh + h]
        sb_o[0, 0, 0, h] = s.astype(BF16)
        st[nh + h] = cdec[nh + h:nh + h + 1, :] * s + ub_ref[0, 0, h]


def _attn_kernel(qT_ref, kc_ref, vTc_ref, k_ref, vT_ref, ga_ref, uc_ref, uf_ref, ub_ref, cdec_ref,
                 o_ref, sf_o, sb_o, st, *, tq, tk, n_kv, bounded):
    _scan_step(pl.program_id(2) == 0, uc_ref, uf_ref, ub_ref, cdec_ref, sf_o, sb_o, st)
    hd = ATT_HEAD_DIM
    qT = qT_ref[0, 0]
    wq = jnp.concatenate(
        [jnp.concatenate([qT[h * hd:(h + 1) * hd, :]] * ATT_KV_HEADS, axis=0)
         for h in range(ATT_GROUP)], axis=1)
    cols = ATT_GROUP * tq

    if bounded:
        n_ct = cols // MXU_TILE
        kv = [(kc_ref[0], vTc_ref[0, 0, 0])]
        kv += [(k_ref[0, j * tk:(j + 1) * tk, :], vT_ref[0, j, 0]) for j in range(n_kv)]
        units = [(b, c) for b in range(len(kv)) for c in range(n_ct)]
        accs = [jnp.zeros((hd, MXU_TILE), F32) for _ in range(n_ct)]
        dens = [jnp.zeros((SUBLANES, MXU_TILE), F32) for _ in range(n_ct)]
        pending = []

        def value_matmul(b, c, sT):
            p = jnp.exp2(sT)
            dens[c] = dens[c] + p.reshape(p.shape[0] // SUBLANES, SUBLANES, MXU_TILE).sum(axis=0)
            accs[c] = accs[c] + jnp.dot(kv[b][1], p.astype(BF16), preferred_element_type=F32)

        for b, c in units:
            sT = jnp.dot(kv[b][0], wq[:, c * MXU_TILE:(c + 1) * MXU_TILE],
                         preferred_element_type=F32)
            pending.append((b, c, sT))
            if len(pending) > PV_LAG:
                value_matmul(*pending.pop(0))
        for item in pending:
            value_matmul(*item)
        acc = jnp.concatenate(accs, axis=1)
        den = jnp.sum(jnp.concatenate(dens, axis=1), axis=0, keepdims=True)
    else:
        def block(carry, k, vT):
            m, den, acc = carry
            sT = jnp.dot(k, wq, preferred_element_type=F32)
            m_new = jnp.maximum(m, jnp.max(sT, axis=0, keepdims=True))
            p = jnp.exp2(sT - m_new)
            alpha = jnp.exp2(m - m_new)
            den = alpha * den + jnp.sum(p, axis=0, keepdims=True)
            acc = alpha * acc + jnp.dot(vT, p.astype(BF16), preferred_element_type=F32)
            return m_new, den, acc

        carry = (jnp.full((1, cols), -1e30, F32), jnp.zeros((1, cols), F32),
                 jnp.zeros((hd, cols), F32))
        carry = block(carry, kc_ref[0], vTc_ref[0, 0, 0])

        def body(j, carry):
            off = pl.multiple_of(j * tk, tk)
            return block(carry, k_ref[0, pl.ds(off, tk), :], vT_ref[0, j, 0])

        _, den, acc = lax.fori_loop(0, n_kv, body, carry)
    oT = acc * (1.0 / den)
    oT = jnp.concatenate([oT, jnp.zeros((LANES - hd, cols), F32)], axis=0)
    out = jnp.concatenate([oT[:, h * tq:(h + 1) * tq].T[:, 0:hd] for h in range(ATT_GROUP)], axis=1)
    o_ref[...] = (out * ga_ref[...].astype(F32)).astype(BF16)


def _run_attn(qT, kz_c, vT_c, kz, vT, ga, u_c, u, cdec, *, batch, seq, ctx_len, bounded):
    tq = TILE
    nq = seq // tq
    n_kv = seq // TILE
    gw = ATT_GROUP * ATT_HEAD_DIM
    hd = RET_HEAD_DIM
    s_blk = (1, 1, RET_HEADS, hd, hd)
    s_out_blk = (1,) + s_blk
    s_shape = jax.ShapeDtypeStruct((ATT_KV_HEADS, batch, nq, RET_HEADS, hd, hd), BF16)
    kern = functools.partial(_attn_kernel, tq=tq, tk=TILE, n_kv=n_kv, bounded=bounded)
    return pl.pallas_call(
        kern,
        grid=(batch, ATT_KV_HEADS, nq),
        in_specs=[
            pl.BlockSpec((1, 1, gw, tq), lambda b, g, i: (b, i, g, 0)),
            pl.BlockSpec((1, ctx_len, LANES), lambda b, g, i: (g, b, 0)),
            pl.BlockSpec((1, 1, 1, VT_ROWS, ctx_len), lambda b, g, i: (b, 0, g, 0, 0)),
            pl.BlockSpec((1, seq, LANES), lambda b, g, i: (g, b, 0)),
            pl.BlockSpec((1, n_kv, 1, VT_ROWS, TILE), lambda b, g, i: (b, 0, g, 0, 0)),
            pl.BlockSpec((tq, gw), lambda b, g, i: (b * nq + i, g)),
            pl.BlockSpec((1, 1, N_DIR, hd, hd), lambda b, g, i: (b, 0, 0, 0, 0)),
            pl.BlockSpec(s_blk, lambda b, g, i: (b, i, 0, 0, 0)),
            pl.BlockSpec(s_blk, lambda b, g, i: (b, nq - 1 - i, 1, 0, 0)),
            pl.BlockSpec((N_DIR, LANES), lambda b, g, i: (0, 0)),
        ],
        out_specs=[
            pl.BlockSpec((tq, gw), lambda b, g, i: (b * nq + i, g)),
            pl.BlockSpec(s_out_blk, lambda b, g, i: (g, b, i, 0, 0, 0)),
            pl.BlockSpec(s_out_blk, lambda b, g, i: (g, b, nq - 1 - i, 0, 0, 0)),
        ],
        out_shape=[jax.ShapeDtypeStruct((batch * seq, ATT_WIDTH), BF16), s_shape, s_shape],
        scratch_shapes=[pltpu.VMEM((N_DIR, hd, hd), F32)],
        compiler_params=pltpu.CompilerParams(
            dimension_semantics=("arbitrary", "arbitrary", "arbitrary"),
            vmem_limit_bytes=VMEM_LIMIT),
        name="attn_bounded" if bounded else "attn_online",
    )(qT, kz_c, vT_c, kz, vT, ga, u_c, u, u, cdec)


def _epilogue_kernel(att_ref, qr_ref, krT_ref, vr_ref, gr_ref, x_ref, sf_ref, sb_ref,
                     dmask_ref, qdec_ref, gnw_ref, gate_ref, wout_ref, fnw_ref, y_ref, ret_scr):
    nh = RET_HEADS
    heads = [slice(h * RET_HEAD_DIM, (h + 1) * RET_HEAD_DIM) for h in range(nh)]
    a = [jnp.dot(qr_ref[:, hs], krT_ref[0, 0, hs, :], preferred_element_type=F32) * dmask_ref[h]
         for h, hs in enumerate(heads)]
    proj_att = jnp.dot(att_ref[...], wout_ref[0:ATT_WIDTH, :], preferred_element_type=F32)
    for h, hs in enumerate(heads):
        qf = qr_ref[:, hs].astype(F32)
        o = (jnp.dot((qf * qdec_ref[h]).astype(BF16), sf_ref[0, 0, 0, h],
                     preferred_element_type=F32)
             + jnp.dot((qf * qdec_ref[nh + h]).astype(BF16), sb_ref[0, 0, 0, h],
                       preferred_element_type=F32)
             + jnp.dot(a[h].astype(BF16), vr_ref[:, hs], preferred_element_type=F32))
        mu = jnp.mean(o, axis=-1, keepdims=True)
        d = o - mu
        var = jnp.mean(d * d, axis=-1, keepdims=True)
        yh = d * lax.rsqrt(var + NORM_EPS) * gnw_ref[:, hs] * gr_ref[:, hs].astype(F32)
        ret_scr[:, hs] = yh.astype(BF16)
    proj = proj_att + jnp.dot(ret_scr[...], wout_ref[ATT_WIDTH:, :], preferred_element_type=F32)
    z = x_ref[...] + gate_ref[0] * proj
    ms = jnp.mean(z * z, axis=-1, keepdims=True)
    y_ref[...] = z * lax.rsqrt(ms + NORM_EPS) * fnw_ref[...]


def _run_epilogue(att, qr, krT, vr, gr, x2, sf, sb, dmask, qdec, gnw, gate, w_out, fnw,
                  *, batch, seq):
    nt = seq // TILE
    row_map = lambda b, t: (b * nt + t, 0)
    const2 = lambda b, t: (0, 0)
    const3 = lambda b, t: (0, 0, 0)
    s_spec = pl.BlockSpec((1, 1, 1, RET_HEADS, RET_HEAD_DIM, RET_HEAD_DIM),
                          lambda b, t: (0, b, t, 0, 0, 0))
    return pl.pallas_call(
        _epilogue_kernel,
        grid=(batch, nt),
        in_specs=[
            pl.BlockSpec((TILE, ATT_WIDTH), row_map),
            pl.BlockSpec((TILE, RET_WIDTH), row_map),
            pl.BlockSpec((1, 1, RET_WIDTH, TILE), lambda b, t: (b, t, 0, 0)),
            pl.BlockSpec((TILE, RET_WIDTH), row_map),
            pl.BlockSpec((TILE, RET_WIDTH), row_map),
            pl.BlockSpec((TILE, D_MODEL), row_map),
            s_spec, s_spec,
            pl.BlockSpec((RET_HEADS, TILE, TILE), const3),
            pl.BlockSpec((N_DIR, TILE, LANES), const3),
            pl.BlockSpec((1, RET_WIDTH), const2),
            pl.BlockSpec((1, 1, D_MODEL), lambda b, t: (b, 0, 0)),
            pl.BlockSpec((ATT_WIDTH + RET_WIDTH, D_MODEL), const2),
            pl.BlockSpec((1, D_MODEL), const2),
        ],
        out_specs=pl.BlockSpec((TILE, D_MODEL), row_map),
        out_shape=jax.ShapeDtypeStruct((batch * seq, D_MODEL), F32),
        scratch_shapes=[pltpu.VMEM((TILE, RET_WIDTH), BF16)],
        compiler_params=pltpu.CompilerParams(dimension_semantics=("arbitrary", "arbitrary"),
                                             vmem_limit_bytes=VMEM_LIMIT),
        name="epilogue",
    )(att, qr, krT, vr, gr, x2, sf, sb, dmask, qdec, gnw, gate, w_out, fnw)


def _rope_tables(seq):
    rows = jnp.arange(seq // GRID_W, dtype=F32)
    cols = jnp.arange(GRID_W, dtype=F32)

    def parts(head_dim):
        n_axis = head_dim // 4
        inv_freq = ROPE_THETA ** (-jnp.arange(n_axis, dtype=F32) / n_axis)
        reps = LANES // head_dim

        def lanes(pos, first):
            ang = pos[:, None] * inv_freq
            z = jnp.zeros_like(ang)
            c = jnp.concatenate([jnp.cos(ang), z] if first else [z, jnp.cos(ang)], axis=-1)
            s = jnp.concatenate([jnp.sin(ang), z] if first else [z, jnp.sin(ang)], axis=-1)
            return (jnp.tile(jnp.concatenate([c, c], axis=-1), (1, reps)),
                    jnp.tile(jnp.concatenate([-s, s], axis=-1), (1, reps)))

        return lanes(rows, True), lanes(cols, False)

    (rca, rsa), (cca, csa) = parts(ATT_HEAD_DIM)
    (rcr, rsr), (ccr, csr) = parts(RET_HEAD_DIM)
    return jnp.stack([rca, rsa, rcr, rsr]), jnp.stack([cca, csa, ccr, csr])


def kernel(x, c, ctx, c_ctx, norm_w, w_mod, b_mod, w_in, q_norm_w, k_norm_w,
           ret_decay_fwd, ret_decay_bwd, ret_gn_w, w_out, final_norm_w):
    batch, seq, d = x.shape
    ctx_len = ctx.shape[1]
    depth = norm_w.shape[0]
    assert d == D_MODEL and depth == 1 and seq % TILE == 0
    assert ctx_len % LANES == 0 and ctx_len <= TILE
    assert batch + 1 <= MOD_ROWS and w_in.shape[-1] == IN_WIDTH
    layer = 0

    cc = jnp.zeros((MOD_ROWS, D_MODEL), F32).at[:batch].set(c).at[batch].set(c_ctx)
    dl = jnp.broadcast_to(
        jnp.concatenate([ret_decay_fwd[layer], ret_decay_bwd[layer]]).astype(F32)[:, None],
        (N_DIR, LANES))
    mod, dmask, qdec, kdec, kdec_c, cdec = _run_mod(
        cc, w_mod[layer], b_mod[layer].reshape(3, 1, D_MODEL), dl, ctx_len=ctx_len)
    shift, scale, gate = mod[0], mod[1], mod[2]

    w_in_b = w_in[layer].astype(BF16)
    nw = norm_w[layer].reshape(1, D_MODEL)
    qnw = jnp.tile(q_norm_w[layer], LANES // ATT_HEAD_DIM).reshape(1, LANES)
    knw = jnp.tile(k_norm_w[layer], LANES // ATT_HEAD_DIM).reshape(1, LANES)

    ctx_rows = ctx_len // GRID_W
    one = jnp.ones((ctx_rows, LANES), F32)
    zero = jnp.zeros((ctx_rows, LANES), F32)
    row_tab_c = jnp.stack([one, zero, one, zero])
    col_tab_c = jnp.zeros((4, ctx_len, LANES), F32)
    sc_c = jnp.broadcast_to(scale[batch].reshape(1, 1, D_MODEL), (batch, 1, D_MODEL))
    sh_c = jnp.broadcast_to(shift[batch].reshape(1, 1, D_MODEL), (batch, 1, D_MODEL))
    (_, kz_c, vT_c, _, _, _, _, _, u_c) = _run_inproj(
        ctx.reshape(batch * ctx_len, d), sc_c, sh_c, nw, w_in_b, qnw, knw,
        row_tab_c, col_tab_c, kdec_c, batch=batch, seq=ctx_len, tm=ctx_len)

    x2 = x.reshape(batch * seq, d)
    sc_l = scale[:batch].reshape(batch, 1, D_MODEL)
    sh_l = shift[:batch].reshape(batch, 1, D_MODEL)
    row_tab, col_tab = _rope_tables(seq)
    col_tab = jnp.tile(col_tab, (1, TILE // GRID_W, 1))
    qT, kz, vT, ga, qr, krT, vr, gr, u = _run_inproj(
        x2, sc_l, sh_l, nw, w_in_b, qnw, knw, row_tab, col_tab, kdec,
        batch=batch, seq=seq, tm=TILE)

    score_bound = (ATT_HEAD_DIM * Q_SCALE_LOG2 * BF16_ROUND_UP ** 2
                   * jnp.max(jnp.abs(q_norm_w[layer])) * jnp.max(jnp.abs(k_norm_w[layer])))
    attn = functools.partial(_run_attn, batch=batch, seq=seq, ctx_len=ctx_len)
    att, sf, sb = lax.cond(score_bound <= SCORE_LOG2_LIMIT,
                           functools.partial(attn, bounded=True),
                           functools.partial(attn, bounded=False),
                           qT, kz_c, vT_c, kz, vT, ga, u_c, u, cdec)
    y = _run_epilogue(att, qr, krT, vr, gr, x2, sf, sb, dmask, qdec,
                      ret_gn_w[layer].reshape(1, RET_WIDTH),
                      gate[:batch].reshape(batch, 1, D_MODEL),
                      w_out[layer].astype(BF16), final_norm_w.reshape(1, D_MODEL),
                      batch=batch, seq=seq)
    return y.reshape(batch, seq, d)
```

```python
import functools

import jax
import jax.numpy as jnp
from jax import lax
from jax.experimental import pallas as pl
from jax.experimental.pallas import tpu as pltpu

F32 = jnp.float32
BF16 = jnp.bfloat16

D_MODEL = 1024
GRID_W = 64
ATT_HEADS = 8
ATT_KV_HEADS = 2
ATT_GROUP = ATT_HEADS // ATT_KV_HEADS
ATT_HEAD_DIM = 64
ATT_WIDTH = ATT_HEADS * ATT_HEAD_DIM
ATT_KV_WIDTH = ATT_KV_HEADS * ATT_HEAD_DIM
RET_HEADS = 4
RET_HEAD_DIM = 128
RET_WIDTH = RET_HEADS * RET_HEAD_DIM
ROPE_THETA = 10000.0
NORM_EPS = 1e-6
LANES = 128
TILE = 512
MOD_ROWS = 8
VT_ROWS = ATT_HEAD_DIM
SUBLANES = 8
N_DIR = 2 * RET_HEADS
MXU_TILE = 256
ATT_Q_TILES = 2
PV_LAG = 2

O_QA = 0
O_KA = O_QA + ATT_WIDTH
O_VA = O_KA + ATT_KV_WIDTH
O_GA = O_VA + ATT_KV_WIDTH
O_QR = O_GA + ATT_WIDTH
O_KR = O_QR + RET_WIDTH
O_VR = O_KR + RET_WIDTH
O_GR = O_VR + RET_WIDTH
IN_WIDTH = O_GR + RET_WIDTH

VMEM_LIMIT = 48 * 1024 * 1024

LOG2_E = 1.4426950408889634
Q_SCALE_LOG2 = ATT_HEAD_DIM ** -0.5 * LOG2_E
SCORE_LOG2_LIMIT = 100.0
BF16_ROUND_UP = 1.0 + 2.0 ** -8


def _silu(x):
    return x * (1.0 / (1.0 + jnp.exp(-x)))


def _silu_tanh(x):
    hx = 0.5 * x
    return hx + hx * jnp.tanh(hx)


def _mod_kernel(cc_ref, w_ref, b_ref, dl_ref,
                mod_ref, dmask_ref, qdec_ref, kdec_ref, kdec_c_ref, cdec_ref):
    sc = _silu(cc_ref[...])
    mod_ref[0] = jnp.dot(sc, w_ref[...], preferred_element_type=F32,
                         precision=lax.Precision.HIGHEST) + b_ref[0]

    @pl.when(pl.program_id(0) == 0)
    def _():
        x = dl_ref[...]
        lg = jnp.minimum(x, 0.0) - jnp.log1p(jnp.exp(-jnp.abs(x)))
        n = float(TILE)
        ii = lax.broadcasted_iota(jnp.int32, (TILE, TILE), 0).astype(F32)
        jj = lax.broadcasted_iota(jnp.int32, (TILE, TILE), 1).astype(F32)
        d = ii - jj
        i_col = lax.broadcasted_iota(jnp.int32, (TILE, LANES), 0).astype(F32)
        for h in range(RET_HEADS):
            lf = lg[h:h + 1, :]
            lb = lg[RET_HEADS + h:RET_HEADS + h + 1, :]
            lf_t = jnp.concatenate([lf] * (TILE // LANES), axis=1)
            lb_t = jnp.concatenate([lb] * (TILE // LANES), axis=1)
            fwd = jnp.exp(lf_t * jnp.maximum(d, 0.0))
            bwd = jnp.exp(lb_t * jnp.maximum(-d, 0.0))
            dmask_ref[h] = jnp.where(d > 0, fwd, jnp.where(d < 0, bwd, 2.0))
            qdec_ref[h] = jnp.exp(lf * (i_col + 1.0))
            qdec_ref[RET_HEADS + h] = jnp.exp(lb * (n - i_col))
        for ref in (kdec_ref, kdec_c_ref):
            length = ref.shape[1]
            tok = lax.broadcasted_iota(jnp.int32, (N_DIR, length), 1).astype(F32)
            row = lax.broadcasted_iota(jnp.int32, (N_DIR, length), 0)
            lg_t = jnp.concatenate([lg] * (length // LANES), axis=1)
            ref[...] = jnp.where(row < RET_HEADS, jnp.exp(lg_t * (float(length) - 1.0 - tok)),
                                 jnp.exp(lg_t * tok))
        cdec_ref[...] = jnp.exp(lg * n)


def _run_mod(cc, w_mod, b_mod, dl, *, ctx_len):
    n3 = 3
    const2 = lambda j: (0, 0)
    const3 = lambda j: (0, 0, 0)
    return pl.pallas_call(
        _mod_kernel,
        grid=(n3,),
        in_specs=[
            pl.BlockSpec((MOD_ROWS, D_MODEL), const2),
            pl.BlockSpec((D_MODEL, D_MODEL), lambda j: (0, j)),
            pl.BlockSpec((1, 1, D_MODEL), lambda j: (j, 0, 0)),
            pl.BlockSpec((N_DIR, LANES), const2),
        ],
        out_specs=[
            pl.BlockSpec((1, MOD_ROWS, D_MODEL), lambda j: (j, 0, 0)),
            pl.BlockSpec((RET_HEADS, TILE, TILE), const3),
            pl.BlockSpec((N_DIR, TILE, LANES), const3),
            pl.BlockSpec((N_DIR, TILE), const2),
            pl.BlockSpec((N_DIR, ctx_len), const2),
            pl.BlockSpec((N_DIR, LANES), const2),
        ],
        out_shape=[
            jax.ShapeDtypeStruct((n3, MOD_ROWS, D_MODEL), F32),
            jax.ShapeDtypeStruct((RET_HEADS, TILE, TILE), F32),
            jax.ShapeDtypeStruct((N_DIR, TILE, LANES), F32),
            jax.ShapeDtypeStruct((N_DIR, TILE), F32),
            jax.ShapeDtypeStruct((N_DIR, ctx_len), F32),
            jax.ShapeDtypeStruct((N_DIR, LANES), F32),
        ],
        compiler_params=pltpu.CompilerParams(dimension_semantics=("arbitrary",),
                                             vmem_limit_bytes=VMEM_LIMIT),
        name="mod",
    )(cc, w_mod, b_mod, dl)


def _inproj_kernel(x_ref, sc_ref, sh_ref, nw_ref, w_ref, qnw_ref, knw_ref, rt_ref, ct_ref, kdec_ref,
                   qT_o, kz_o, vT_o, ga_o, qr_o, krT_o, vr_o, gr_o, u_o):
    tm = x_ref.shape[0]
    x = x_ref[...]
    ms = jnp.mean(x * x, axis=-1, keepdims=True)
    a = nw_ref[...] * (1.0 + sc_ref[0])
    hb = ((x * lax.rsqrt(ms + NORM_EPS)) * a + sh_ref[0]).astype(BF16)

    def proj(lo, width):
        return jnp.dot(hb, w_ref[:, lo:lo + width], preferred_element_type=F32)

    def rope_table(k):
        rt = rt_ref[k]
        rows = [jnp.broadcast_to(rt[i:i + 1, :], (GRID_W, LANES)) for i in range(tm // GRID_W)]
        return jnp.concatenate(rows, axis=0) + ct_ref[k]

    ca, sa, cr, sr = (rope_table(k) for k in range(4))
    lane = lax.broadcasted_iota(jnp.int32, (tm, LANES), 1)
    head_lo = lane < ATT_HEAD_DIM
    half_lo = (lane & (ATT_HEAD_DIM // 2)) == 0
    inv_hd = 1.0 / ATT_HEAD_DIM

    def att_norm_rope(v, w):
        sq = v * v
        s_lo = jnp.sum(jnp.where(head_lo, sq, 0.0), axis=-1, keepdims=True)
        s_hi = jnp.sum(jnp.where(head_lo, 0.0, sq), axis=-1, keepdims=True)
        r = jnp.where(head_lo, lax.rsqrt(s_lo * inv_hd + NORM_EPS),
                      lax.rsqrt(s_hi * inv_hd + NORM_EPS))
        vn = v * r * w
        rot = jnp.where(half_lo, pltpu.roll(vn, LANES - ATT_HEAD_DIM // 2, 1),
                        pltpu.roll(vn, ATT_HEAD_DIM // 2, 1))
        return vn * ca + rot * sa

    def ret_rope(v):
        return v * cr + pltpu.roll(v, RET_HEAD_DIM // 2, 1) * sr

    qnw = qnw_ref[...]
    qa = proj(O_QA, ATT_WIDTH)
    for c in range(ATT_WIDTH // LANES):
        cs = slice(c * LANES, (c + 1) * LANES)
        qT = (att_norm_rope(qa[:, cs], qnw) * Q_SCALE_LOG2).T
        qT_o[0, 0, cs, :] = qT.astype(BF16)

    ka = att_norm_rope(proj(O_KA, ATT_KV_WIDTH), knw_ref[...])
    kz_o[0] = jnp.where(head_lo, ka, 0.0).astype(BF16)
    kz_o[1] = jnp.where(head_lo, 0.0, ka).astype(BF16)

    va = proj(O_VA, ATT_KV_WIDTH)
    for g in range(ATT_KV_HEADS):
        vg = va if g == 0 else pltpu.roll(va, ATT_HEAD_DIM, 1)
        vT_o[0, 0, g] = jnp.where(head_lo, vg, 0.0).T[0:ATT_HEAD_DIM, :].astype(BF16)

    ga_o[...] = _silu_tanh(proj(O_GA, ATT_WIDTH)).astype(BF16)

    vr = proj(O_VR, RET_WIDTH).astype(BF16)
    vr_o[...] = vr
    qr = proj(O_QR, RET_WIDTH)
    kr = proj(O_KR, RET_WIDTH)
    kdec = kdec_ref[...]
    krT = []
    for h in range(RET_HEADS):
        hs = slice(h * RET_HEAD_DIM, (h + 1) * RET_HEAD_DIM)
        qr_o[:, hs] = ret_rope(qr[:, hs]).astype(BF16)
        krT.append((ret_rope(kr[:, hs]) * (RET_HEAD_DIM ** -0.5)).T)
        krT_o[0, 0, hs, :] = krT[h].astype(BF16)

    gr_o[...] = _silu_tanh(proj(O_GR, RET_WIDTH)).astype(BF16)

    for h in range(RET_HEADS):
        hs = slice(h * RET_HEAD_DIM, (h + 1) * RET_HEAD_DIM)
        for row in (h, RET_HEADS + h):
            kd = (krT[h] * kdec[row:row + 1, :]).astype(BF16)
            u_o[0, 0, row] = jnp.dot(kd, vr[:, hs], preferred_element_type=F32)


def _run_inproj(x2, scale, shift, norm_w, w_in, qnw, knw, row_tab, col_tab, kdec,
                *, batch, seq, tm):
    nt = seq // tm
    rows = batch * seq
    row_map = lambda t, b: (b * nt + t, 0)
    const2 = lambda t, b: (0, 0)
    vec_spec = pl.BlockSpec((1, 1, D_MODEL), lambda t, b: (b, 0, 0))
    t_map = lambda t, b: (b, t, 0, 0)

    def row_out(width):
        return pl.BlockSpec((tm, width), row_map)

    return pl.pallas_call(
        _inproj_kernel,
        grid=(nt, batch),
        in_specs=[
            pl.BlockSpec((tm, D_MODEL), row_map),
            vec_spec, vec_spec,
            pl.BlockSpec((1, D_MODEL), const2),
            pl.BlockSpec((D_MODEL, IN_WIDTH), const2),
            pl.BlockSpec((1, LANES), const2),
            pl.BlockSpec((1, LANES), const2),
            pl.BlockSpec((4, tm // GRID_W, LANES), lambda t, b: (0, t, 0)),
            pl.BlockSpec((4, tm, LANES), lambda t, b: (0, 0, 0)),
            pl.BlockSpec((N_DIR, tm), const2),
        ],
        out_specs=[
            pl.BlockSpec((1, 1, ATT_WIDTH, tm), t_map),
            pl.BlockSpec((ATT_KV_HEADS, tm, LANES), lambda t, b: (0, b * nt + t, 0)),
            pl.BlockSpec((1, 1, ATT_KV_HEADS, VT_ROWS, tm), lambda t, b: (b, t, 0, 0, 0)),
            row_out(ATT_WIDTH),
            row_out(RET_WIDTH),
            pl.BlockSpec((1, 1, RET_WIDTH, tm), t_map),
            row_out(RET_WIDTH),
            row_out(RET_WIDTH),
            pl.BlockSpec((1, 1, N_DIR, RET_HEAD_DIM, RET_HEAD_DIM), lambda t, b: (b, t, 0, 0, 0)),
        ],
        out_shape=[
            jax.ShapeDtypeStruct((batch, nt, ATT_WIDTH, tm), BF16),
            jax.ShapeDtypeStruct((ATT_KV_HEADS, rows, LANES), BF16),
            jax.ShapeDtypeStruct((batch, nt, ATT_KV_HEADS, VT_ROWS, tm), BF16),
            jax.ShapeDtypeStruct((rows, ATT_WIDTH), BF16),
            jax.ShapeDtypeStruct((rows, RET_WIDTH), BF16),
            jax.ShapeDtypeStruct((batch, nt, RET_WIDTH, tm), BF16),
            jax.ShapeDtypeStruct((rows, RET_WIDTH), BF16),
            jax.ShapeDtypeStruct((rows, RET_WIDTH), BF16),
            jax.ShapeDtypeStruct((batch, nt, N_DIR, RET_HEAD_DIM, RET_HEAD_DIM), F32),
        ],
        compiler_params=pltpu.CompilerParams(dimension_semantics=("arbitrary", "arbitrary"),
                                             vmem_limit_bytes=VMEM_LIMIT),
        name="inproj",
    )(x2, scale, shift, norm_w, w_in, qnw, knw, row_tab, col_tab, kdec)


def _scan_step(first, uc_ref, uf_ref, ub_ref, cdec_ref, sf_o, sb_o, st):
    nh = RET_HEADS
    n = uf_ref.shape[1]
    cdec = cdec_ref[...]

    @pl.when(first)
    def _():
        st[...] = uc_ref[0, 0]

    for t in range(n):
        tb = n - 1 - t
        for h in range(nh):
            s = st[h]
            sf_o[0, 0, t, h] = s.astype(BF16)
            st[h] = cdec[h:h + 1, :] * s + uf_ref[0, t, h]
            s = st[nh + h]
            sb_o[0, 0, tb, h] = s.astype(BF16)
            st[nh + h] = cdec[nh + h:nh + h + 1, :] * s + ub_ref[0, tb, h]


def _attn_kernel(qT_ref, kc_ref, vTc_ref, k_ref, vT_ref, ga_ref, uc_ref, uf_ref, ub_ref, cdec_ref,
                 o_ref, sf_o, sb_o, st, *, tq, tk, n_kv, bounded):
    _scan_step(pl.program_id(2) == 0, uc_ref, uf_ref, ub_ref, cdec_ref, sf_o, sb_o, st)
    hd = ATT_HEAD_DIM
    q_tiles = qT_ref.shape[1]
    wq = jnp.concatenate(
        [jnp.concatenate([qT_ref[0, t, h * hd:(h + 1) * hd, :]] * ATT_KV_HEADS, axis=0)
         for t in range(q_tiles) for h in range(ATT_GROUP)],
        axis=1)
    cols = q_tiles * ATT_GROUP * tq

    if bounded:
        n_ct = cols // MXU_TILE
        kv = [(kc_ref[0], vTc_ref[0, 0, 0])]
        kv += [(k_ref[0, j * tk:(j + 1) * tk, :], vT_ref[0, j, 0]) for j in range(n_kv)]
        units = [(b, c) for b in range(len(kv)) for c in range(n_ct)]
        accs = [jnp.zeros((hd, MXU_TILE), F32) for _ in range(n_ct)]
        dens = [jnp.zeros((SUBLANES, MXU_TILE), F32) for _ in range(n_ct)]
        pending = []

        def value_matmul(b, c, sT):
            p = jnp.exp2(sT)
            dens[c] = dens[c] + p.reshape(p.shape[0] // SUBLANES, SUBLANES, MXU_TILE).sum(axis=0)
            accs[c] = accs[c] + jnp.dot(kv[b][1], p.astype(BF16), preferred_element_type=F32)

        for b, c in units:
            sT = jnp.dot(kv[b][0], wq[:, c * MXU_TILE:(c + 1) * MXU_TILE],
                         preferred_element_type=F32)
            pending.append((b, c, sT))
            if len(pending) > PV_LAG:
                value_matmul(*pending.pop(0))
        for item in pending:
            value_matmul(*item)
        acc = jnp.concatenate(accs, axis=1)
        den = jnp.sum(jnp.concatenate(dens, axis=1), axis=0, keepdims=True)
    else:
        def block(carry, k, vT):
            m, den, acc = carry
            sT = jnp.dot(k, wq, preferred_element_type=F32)
            m_new = jnp.maximum(m, jnp.max(sT, axis=0, keepdims=True))
            p = jnp.exp2(sT - m_new)
            alpha = jnp.exp2(m - m_new)
            den = alpha * den + jnp.sum(p, axis=0, keepdims=True)
            acc = alpha * acc + jnp.dot(vT, p.astype(BF16), preferred_element_type=F32)
            return m_new, den, acc

        carry = (jnp.full((1, cols), -1e30, F32), jnp.zeros((1, cols), F32),
                 jnp.zeros((hd, cols), F32))
        carry = block(carry, kc_ref[0], vTc_ref[0, 0, 0])

        def body(j, carry):
            off = pl.multiple_of(j * tk, tk)
            return block(carry, k_ref[0, pl.ds(off, tk), :], vT_ref[0, j, 0])

        _, den, acc = lax.fori_loop(0, n_kv, body, carry)
    oT = acc * (1.0 / den)
    oT = jnp.concatenate([oT, jnp.zeros((LANES - hd, cols), F32)], axis=0)
    for t in range(q_tiles):
        base = t * ATT_GROUP * tq
        out = jnp.concatenate([oT[:, base + h * tq:base + (h + 1) * tq].T[:, 0:hd]
                               for h in range(ATT_GROUP)], axis=1)
        rs = slice(t * tq, (t + 1) * tq)
        o_ref[rs, :] = (out * ga_ref[rs, :].astype(F32)).astype(BF16)


def _run_attn(qT, kz_c, vT_c, kz, vT, ga, u_c, u, cdec, *, batch, seq, ctx_len, bounded):
    tq = TILE
    qt = ATT_Q_TILES
    nt = seq // tq
    nq = nt // qt
    n_kv = seq // TILE
    gw = ATT_GROUP * ATT_HEAD_DIM
    hd = RET_HEAD_DIM
    s_blk = (1, qt, RET_HEADS, hd, hd)
    s_out_blk = (1,) + s_blk
    s_shape = jax.ShapeDtypeStruct((ATT_KV_HEADS, batch, nt, RET_HEADS, hd, hd), BF16)
    kern = functools.partial(_attn_kernel, tq=tq, tk=TILE, n_kv=n_kv, bounded=bounded)
    return pl.pallas_call(
        kern,
        grid=(batch, ATT_KV_HEADS, nq),
        in_specs=[
            pl.BlockSpec((1, qt, gw, tq), lambda b, g, i: (b, i, g, 0)),
            pl.BlockSpec((1, ctx_len, LANES), lambda b, g, i: (g, b, 0)),
            pl.BlockSpec((1, 1, 1, VT_ROWS, ctx_len), lambda b, g, i: (b, 0, g, 0, 0)),
            pl.BlockSpec((1, seq, LANES), lambda b, g, i: (g, b, 0)),
            pl.BlockSpec((1, n_kv, 1, VT_ROWS, TILE), lambda b, g, i: (b, 0, g, 0, 0)),
            pl.BlockSpec((qt * tq, gw), lambda b, g, i: (b * nq + i, g)),
            pl.BlockSpec((1, 1, N_DIR, hd, hd), lambda b, g, i: (b, 0, 0, 0, 0)),
            pl.BlockSpec(s_blk, lambda b, g, i: (b, i, 0, 0, 0)),
            pl.BlockSpec(s_blk, lambda b, g, i: (b, nq - 1 - i, 1, 0, 0)),
            pl.BlockSpec((N_DIR, LANES), lambda b, g, i: (0, 0)),
        ],
        out_specs=[
            pl.BlockSpec((qt * tq, gw), lambda b, g, i: (b * nq + i, g)),
            pl.BlockSpec(s_out_blk, lambda b, g, i: (g, b, i, 0, 0, 0)),
            pl.BlockSpec(s_out_blk, lambda b, g, i: (g, b, nq - 1 - i, 0, 0, 0)),
        ],
        out_shape=[jax.ShapeDtypeStruct((batch * seq, ATT_WIDTH), BF16), s_shape, s_shape],
        scratch_shapes=[pltpu.VMEM((N_DIR, hd, hd), F32)],
        compiler_params=pltpu.CompilerParams(
            dimension_semantics=("arbitrary", "arbitrary", "arbitrary"),
            vmem_limit_bytes=VMEM_LIMIT),
        name="attn_bounded" if bounded else "attn_online",
    )(qT, kz_c, vT_c, kz, vT, ga, u_c, u, u, cdec)


def _epilogue_kernel(att_ref, qr_ref, krT_ref, vr_ref, gr_ref, x_ref, sf_ref, sb_ref,
                     dmask_ref, qdec_ref, gnw_ref, gate_ref, wout_ref, fnw_ref, y_ref, ret_scr):
    nh = RET_HEADS
    heads = [slice(h * RET_HEAD_DIM, (h + 1) * RET_HEAD_DIM) for h in range(nh)]
    a = [jnp.dot(qr_ref[:, hs], krT_ref[0, 0, hs, :], preferred_element_type=F32) * dmask_ref[h]
         for h, hs in enumerate(heads)]
    proj_att = jnp.dot(att_ref[...], wout_ref[0:ATT_WIDTH, :], preferred_element_type=F32)
    for h, hs in enumerate(heads):
        qf = qr_ref[:, hs].astype(F32)
        o = (jnp.dot((qf * qdec_ref[h]).astype(BF16), sf_ref[0, 0, 0, h],
                     preferred_element_type=F32)
             + jnp.dot((qf * qdec_ref[nh + h]).astype(BF16), sb_ref[0, 0, 0, h],
                       preferred_element_type=F32)
             + jnp.dot(a[h].astype(BF16), vr_ref[:, hs], preferred_element_type=F32))
        mu = jnp.mean(o, axis=-1, keepdims=True)
        d = o - mu
        var = jnp.mean(d * d, axis=-1, keepdims=True)
        yh = d * lax.rsqrt(var + NORM_EPS) * gnw_ref[:, hs] * gr_ref[:, hs].astype(F32)
        ret_scr[:, hs] = yh.astype(BF16)
    proj = proj_att + jnp.dot(ret_scr[...], wout_ref[ATT_WIDTH:, :], preferred_element_type=F32)
    z = x_ref[...] + gate_ref[0] * proj
    ms = jnp.mean(z * z, axis=-1, keepdims=True)
    y_ref[...] = z * lax.rsqrt(ms + NORM_EPS) * fnw_ref[...]


def _run_epilogue(att, qr, krT, vr, gr, x2, sf, sb, dmask, qdec, gnw, gate, w_out, fnw,
                  *, batch, seq):
    nt = seq // TILE
    row_map = lambda b, t: (b * nt + t, 0)
    const2 = lambda b, t: (0, 0)
    const3 = lambda b, t: (0, 0, 0)
    s_spec = pl.BlockSpec((1, 1, 1, RET_HEADS, RET_HEAD_DIM, RET_HEAD_DIM),
                          lambda b, t: (0, b, t, 0, 0, 0))
    return pl.pallas_call(
        _epilogue_kernel,
        grid=(batch, nt),
        in_specs=[
            pl.BlockSpec((TILE, ATT_WIDTH), row_map),
            pl.BlockSpec((TILE, RET_WIDTH), row_map),
            pl.BlockSpec((1, 1, RET_WIDTH, TILE), lambda b, t: (b, t, 0, 0)),
            pl.BlockSpec((TILE, RET_WIDTH), row_map),
            pl.BlockSpec((TILE, RET_WIDTH), row_map),
            pl.BlockSpec((TILE, D_MODEL), row_map),
            s_spec, s_spec,
            pl.BlockSpec((RET_HEADS, TILE, TILE), const3),
            pl.BlockSpec((N_DIR, TILE, LANES), const3),
            pl.BlockSpec((1, RET_WIDTH), const2),
            pl.BlockSpec((1, 1, D_MODEL), lambda b, t: (b, 0, 0)),
            pl.BlockSpec((ATT_WIDTH + RET_WIDTH, D_MODEL), const2),
            pl.BlockSpec((1, D_MODEL), const2),
        ],
        out_specs=pl.BlockSpec((TILE, D_MODEL), row_map),
        out_shape=jax.ShapeDtypeStruct((batch * seq, D_MODEL), F32),
        scratch_shapes=[pltpu.VMEM((TILE, RET_WIDTH), BF16)],
        compiler_params=pltpu.CompilerParams(dimension_semantics=("arbitrary", "arbitrary"),
                                             vmem_limit_bytes=VMEM_LIMIT),
        name="epilogue",
    )(att, qr, krT, vr, gr, x2, sf, sb, dmask, qdec, gnw, gate, w_out, fnw)


def _rope_tables(seq):
    rows = jnp.arange(seq // GRID_W, dtype=F32)
    cols = jnp.arange(GRID_W, dtype=F32)

    def parts(head_dim):
        n_axis = head_dim // 4
        inv_freq = ROPE_THETA ** (-jnp.arange(n_axis, dtype=F32) / n_axis)
        reps = LANES // head_dim

        def lanes(pos, first):
            ang = pos[:, None] * inv_freq
            z = jnp.zeros_like(ang)
            c = jnp.concatenate([jnp.cos(ang), z] if first else [z, jnp.cos(ang)], axis=-1)
            s = jnp.concatenate([jnp.sin(ang), z] if first else [z, jnp.sin(ang)], axis=-1)
            return (jnp.tile(jnp.concatenate([c, c], axis=-1), (1, reps)),
                    jnp.tile(jnp.concatenate([-s, s], axis=-1), (1, reps)))

        return lanes(rows, True), lanes(cols, False)

    (rca, rsa), (cca, csa) = parts(ATT_HEAD_DIM)
    (rcr, rsr), (ccr, csr) = parts(RET_HEAD_DIM)
    return jnp.stack([rca, rsa, rcr, rsr]), jnp.stack([cca, csa, ccr, csr])


def kernel(x, c, ctx, c_ctx, norm_w, w_mod, b_mod, w_in, q_norm_w, k_norm_w,
           ret_decay_fwd, ret_decay_bwd, ret_gn_w, w_out, final_norm_w):
    batch, seq, d = x.shape
    ctx_len = ctx.shape[1]
    depth = norm_w.shape[0]
    assert d == D_MODEL and depth == 1 and seq % (TILE * ATT_Q_TILES) == 0
    assert ctx_len % LANES == 0 and ctx_len <= TILE
    assert batch + 1 <= MOD_ROWS and w_in.shape[-1] == IN_WIDTH
    layer = 0

    cc = jnp.zeros((MOD_ROWS, D_MODEL), F32).at[:batch].set(c).at[batch].set(c_ctx)
    dl = jnp.broadcast_to(
        jnp.concatenate([ret_decay_fwd[layer], ret_decay_bwd[layer]]).astype(F32)[:, None],
        (N_DIR, LANES))
    mod, dmask, qdec, kdec, kdec_c, cdec = _run_mod(
        cc, w_mod[layer], b_mod[layer].reshape(3, 1, D_MODEL), dl, ctx_len=ctx_len)
    shift, scale, gate = mod[0], mod[1], mod[2]

    w_in_b = w_in[layer].astype(BF16)
    nw = norm_w[layer].reshape(1, D_MODEL)
    qnw = jnp.tile(q_norm_w[layer], LANES // ATT_HEAD_DIM).reshape(1, LANES)
    knw = jnp.tile(k_norm_w[layer], LANES // ATT_HEAD_DIM).reshape(1, LANES)

    ctx_rows = ctx_len // GRID_W
    one = jnp.ones((ctx_rows, LANES), F32)
    zero = jnp.zeros((ctx_rows, LANES), F32)
    row_tab_c = jnp.stack([one, zero, one, zero])
    col_tab_c = jnp.zeros((4, ctx_len, LANES), F32)
    sc_c = jnp.broadcast_to(scale[batch].reshape(1, 1, D_MODEL), (batch, 1, D_MODEL))
    sh_c = jnp.broadcast_to(shift[batch].reshape(1, 1, D_MODEL), (batch, 1, D_MODEL))
    (_, kz_c, vT_c, _, _, _, _, _, u_c) = _run_inproj(
        ctx.reshape(batch * ctx_len, d), sc_c, sh_c, nw, w_in_b, qnw, knw,
        row_tab_c, col_tab_c, kdec_c, batch=batch, seq=ctx_len, tm=ctx_len)

    x2 = x.reshape(batch * seq, d)
    sc_l = scale[:batch].reshape(batch, 1, D_MODEL)
    sh_l = shift[:batch].reshape(batch, 1, D_MODEL)
    row_tab, col_tab = _rope_tables(seq)
    col_tab = jnp.tile(col_tab, (1, TILE // GRID_W, 1))
    qT, kz, vT, ga, qr, krT, vr, gr, u = _run_inproj(
        x2, sc_l, sh_l, nw, w_in_b, qnw, knw, row_tab, col_tab, kdec,
        batch=batch, seq=seq, tm=TILE)

    score_bound = (ATT_HEAD_DIM * Q_SCALE_LOG2 * BF16_ROUND_UP ** 2
                   * jnp.max(jnp.abs(q_norm_w[layer])) * jnp.max(jnp.abs(k_norm_w[layer])))
    attn = functools.partial(_run_attn, batch=batch, seq=seq, ctx_len=ctx_len)
    att, sf, sb = lax.cond(score_bound <= SCORE_LOG2_LIMIT,
                           functools.partial(attn, bounded=True),
                           functools.partial(attn, bounded=False),
                           qT, kz_c, vT_c, kz, vT, ga, u_c, u, cdec)
    y = _run_epilogue(att, qr, krT, vr, gr, x2, sf, sb, dmask, qdec,
                      ret_gn_w[layer].reshape(1, RET_WIDTH),
                      gate[:batch].reshape(batch, 1, D_MODEL),
                      w_out[layer].astype(BF16), final_norm_w.reshape(1, D_MODEL),
                      batch=batch, seq=seq)
    return y.reshape(batch, seq, d)
```

```python
import functools

import jax
import jax.numpy as jnp
from jax import lax
from jax.experimental import pallas as pl
from jax.experimental.pallas import tpu as pltpu

F32 = jnp.float32
BF16 = jnp.bfloat16

D_MODEL = 1024
GRID_W = 64
ATT_HEADS = 8
ATT_KV_HEADS = 2
ATT_GROUP = ATT_HEADS // ATT_KV_HEADS
ATT_HEAD_DIM = 64
ATT_WIDTH = ATT_HEADS * ATT_HEAD_DIM
ATT_KV_WIDTH = ATT_KV_HEADS * ATT_HEAD_DIM
RET_HEADS = 4
RET_HEAD_DIM = 128
RET_WIDTH = RET_HEADS * RET_HEAD_DIM
ROPE_THETA = 10000.0
NORM_EPS = 1e-6
LANES = 128
TILE = 512
MOD_ROWS = 8
VT_ROWS = ATT_HEAD_DIM
SUBLANES = 8
N_DIR = 2 * RET_HEADS
MXU_TILE = 256
UNIT_COLS = MXU_TILE
UNIT_KEYS = 256
ATT_Q_TILES = 1
PV_LAG = 4

O_QA = 0
O_KA = O_QA + ATT_WIDTH
O_VA = O_KA + ATT_KV_WIDTH
O_GA = O_VA + ATT_KV_WIDTH
O_QR = O_GA + ATT_WIDTH
O_KR = O_QR + RET_WIDTH
O_VR = O_KR + RET_WIDTH
O_GR = O_VR + RET_WIDTH
IN_WIDTH = O_GR + RET_WIDTH

VMEM_LIMIT = 48 * 1024 * 1024

LOG2_E = 1.4426950408889634
Q_SCALE_LOG2 = ATT_HEAD_DIM ** -0.5 * LOG2_E
SCORE_LOG2_LIMIT = 100.0
BF16_ROUND_UP = 1.0 + 2.0 ** -8


def _silu(x):
    return x * (1.0 / (1.0 + jnp.exp(-x)))


def _silu_tanh(x):
    hx = 0.5 * x
    return hx + hx * jnp.tanh(hx)


def _mod_kernel(cc_ref, w_ref, b_ref, dl_ref,
                mod_ref, dmask_ref, qdec_ref, kdec_ref, kdec_c_ref, cdec_ref):
    sc = _silu(cc_ref[...])
    mod_ref[0] = jnp.dot(sc, w_ref[...], preferred_element_type=F32,
                         precision=lax.Precision.HIGHEST) + b_ref[0]

    @pl.when(pl.program_id(0) == 0)
    def _():
        x = dl_ref[...]
        lg = jnp.minimum(x, 0.0) - jnp.log1p(jnp.exp(-jnp.abs(x)))
        n = float(TILE)
        ii = lax.broadcasted_iota(jnp.int32, (TILE, TILE), 0).astype(F32)
        jj = lax.broadcasted_iota(jnp.int32, (TILE, TILE), 1).astype(F32)
        d = ii - jj
        i_col = lax.broadcasted_iota(jnp.int32, (TILE, LANES), 0).astype(F32)
        for h in range(RET_HEADS):
            lf = lg[h:h + 1, :]
            lb = lg[RET_HEADS + h:RET_HEADS + h + 1, :]
            lf_t = jnp.concatenate([lf] * (TILE // LANES), axis=1)
            lb_t = jnp.concatenate([lb] * (TILE // LANES), axis=1)
            fwd = jnp.exp(lf_t * jnp.maximum(d, 0.0))
            bwd = jnp.exp(lb_t * jnp.maximum(-d, 0.0))
            dmask_ref[h] = jnp.where(d > 0, fwd, jnp.where(d < 0, bwd, 2.0))
            qdec_ref[h] = jnp.exp(lf * (i_col + 1.0))
            qdec_ref[RET_HEADS + h] = jnp.exp(lb * (n - i_col))
        for ref in (kdec_ref, kdec_c_ref):
            length = ref.shape[1]
            tok = lax.broadcasted_iota(jnp.int32, (N_DIR, length), 1).astype(F32)
            row = lax.broadcasted_iota(jnp.int32, (N_DIR, length), 0)
            lg_t = jnp.concatenate([lg] * (length // LANES), axis=1)
            ref[...] = jnp.where(row < RET_HEADS, jnp.exp(lg_t * (float(length) - 1.0 - tok)),
                                 jnp.exp(lg_t * tok))
        cdec_ref[...] = jnp.exp(lg * n)


def _run_mod(cc, w_mod, b_mod, dl, *, ctx_len):
    n3 = 3
    const2 = lambda j: (0, 0)
    const3 = lambda j: (0, 0, 0)
    return pl.pallas_call(
        _mod_kernel,
        grid=(n3,),
        in_specs=[
            pl.BlockSpec((MOD_ROWS, D_MODEL), const2),
            pl.BlockSpec((D_MODEL, D_MODEL), lambda j: (0, j)),
            pl.BlockSpec((1, 1, D_MODEL), lambda j: (j, 0, 0)),
            pl.BlockSpec((N_DIR, LANES), const2),
        ],
        out_specs=[
            pl.BlockSpec((1, MOD_ROWS, D_MODEL), lambda j: (j, 0, 0)),
            pl.BlockSpec((RET_HEADS, TILE, TILE), const3),
            pl.BlockSpec((N_DIR, TILE, LANES), const3),
            pl.BlockSpec((N_DIR, TILE), const2),
            pl.BlockSpec((N_DIR, ctx_len), const2),
            pl.BlockSpec((N_DIR, LANES), const2),
        ],
        out_shape=[
            jax.ShapeDtypeStruct((n3, MOD_ROWS, D_MODEL), F32),
            jax.ShapeDtypeStruct((RET_HEADS, TILE, TILE), F32),
            jax.ShapeDtypeStruct((N_DIR, TILE, LANES), F32),
            jax.ShapeDtypeStruct((N_DIR, TILE), F32),
            jax.ShapeDtypeStruct((N_DIR, ctx_len), F32),
            jax.ShapeDtypeStruct((N_DIR, LANES), F32),
        ],
        compiler_params=pltpu.CompilerParams(dimension_semantics=("arbitrary",),
                                             vmem_limit_bytes=VMEM_LIMIT),
        name="mod",
    )(cc, w_mod, b_mod, dl)


def _inproj_kernel(x_ref, sc_ref, sh_ref, nw_ref, w_ref, qnw_ref, knw_ref, rt_ref, ct_ref, kdec_ref,
                   qT_o, kz_o, vT_o, ga_o, qr_o, krT_o, vr_o, gr_o, u_o):
    tm = x_ref.shape[0]
    x = x_ref[...]
    ms = jnp.mean(x * x, axis=-1, keepdims=True)
    a = nw_ref[...] * (1.0 + sc_ref[0])
    hb = ((x * lax.rsqrt(ms + NORM_EPS)) * a + sh_ref[0]).astype(BF16)

    def proj(lo, width):
        return jnp.dot(hb, w_ref[:, lo:lo + width], preferred_element_type=F32)

    def rope_table(k):
        rt = rt_ref[k]
        rows = [jnp.broadcast_to(rt[i:i + 1, :], (GRID_W, LANES)) for i in range(tm // GRID_W)]
        return jnp.concatenate(rows, axis=0) + ct_ref[k]

    ca, sa, cr, sr = (rope_table(k) for k in range(4))
    lane = lax.broadcasted_iota(jnp.int32, (tm, LANES), 1)
    head_lo = lane < ATT_HEAD_DIM
    half_lo = (lane & (ATT_HEAD_DIM // 2)) == 0
    inv_hd = 1.0 / ATT_HEAD_DIM

    def att_norm_rope(v, w):
        sq = v * v
        s_lo = jnp.sum(jnp.where(head_lo, sq, 0.0), axis=-1, keepdims=True)
        s_hi = jnp.sum(jnp.where(head_lo, 0.0, sq), axis=-1, keepdims=True)
        r = jnp.where(head_lo, lax.rsqrt(s_lo * inv_hd + NORM_EPS),
                      lax.rsqrt(s_hi * inv_hd + NORM_EPS))
        vn = v * r * w
        rot = jnp.where(half_lo, pltpu.roll(vn, LANES - ATT_HEAD_DIM // 2, 1),
                        pltpu.roll(vn, ATT_HEAD_DIM // 2, 1))
        return vn * ca + rot * sa

    def ret_rope(v):
        return v * cr + pltpu.roll(v, RET_HEAD_DIM // 2, 1) * sr

    qnw = qnw_ref[...]
    qa = proj(O_QA, ATT_WIDTH)
    for c in range(ATT_WIDTH // LANES):
        cs = slice(c * LANES, (c + 1) * LANES)
        qT = (att_norm_rope(qa[:, cs], qnw) * Q_SCALE_LOG2).T
        qT_o[0, 0, cs, :] = qT.astype(BF16)

    ka = att_norm_rope(proj(O_KA, ATT_KV_WIDTH), knw_ref[...])
    kz_o[0] = jnp.where(head_lo, ka, 0.0).astype(BF16)
    kz_o[1] = jnp.where(head_lo, 0.0, ka).astype(BF16)

    va = proj(O_VA, ATT_KV_WIDTH)
    for g in range(ATT_KV_HEADS):
        vg = va if g == 0 else pltpu.roll(va, ATT_HEAD_DIM, 1)
        vT_o[0, 0, g] = jnp.where(head_lo, vg, 0.0).T[0:ATT_HEAD_DIM, :].astype(BF16)

    ga_o[...] = _silu_tanh(proj(O_GA, ATT_WIDTH)).astype(BF16)

    vr = proj(O_VR, RET_WIDTH).astype(BF16)
    vr_o[...] = vr
    qr = proj(O_QR, RET_WIDTH)
    kr = proj(O_KR, RET_WIDTH)
    kdec = kdec_ref[...]
    krT = []
    for h in range(RET_HEADS):
        hs = slice(h * RET_HEAD_DIM, (h + 1) * RET_HEAD_DIM)
        qr_o[:, hs] = ret_rope(qr[:, hs]).astype(BF16)
        krT.append((ret_rope(kr[:, hs]) * (RET_HEAD_DIM ** -0.5)).T)
        krT_o[0, 0, hs, :] = krT[h].astype(BF16)

    gr_o[...] = _silu_tanh(proj(O_GR, RET_WIDTH)).astype(BF16)

    for h in range(RET_HEADS):
        hs = slice(h * RET_HEAD_DIM, (h + 1) * RET_HEAD_DIM)
        for row in (h, RET_HEADS + h):
            kd = (krT[h] * kdec[row:row + 1, :]).astype(BF16)
            u_o[0, 0, row] = jnp.dot(kd, vr[:, hs], preferred_element_type=F32)


def _run_inproj(x2, scale, shift, norm_w, w_in, qnw, knw, row_tab, col_tab, kdec,
                *, batch, seq, tm):
    nt = seq // tm
    rows = batch * seq
    row_map = lambda t, b: (b * nt + t, 0)
    const2 = lambda t, b: (0, 0)
    vec_spec = pl.BlockSpec((1, 1, D_MODEL), lambda t, b: (b, 0, 0))
    t_map = lambda t, b: (b, t, 0, 0)

    def row_out(width):
        return pl.BlockSpec((tm, width), row_map)

    return pl.pallas_call(
        _inproj_kernel,
        grid=(nt, batch),
        in_specs=[
            pl.BlockSpec((tm, D_MODEL), row_map),
            vec_spec, vec_spec,
            pl.BlockSpec((1, D_MODEL), const2),
            pl.BlockSpec((D_MODEL, IN_WIDTH), const2),
            pl.BlockSpec((1, LANES), const2),
            pl.BlockSpec((1, LANES), const2),
            pl.BlockSpec((4, tm // GRID_W, LANES), lambda t, b: (0, t, 0)),
            pl.BlockSpec((4, tm, LANES), lambda t, b: (0, 0, 0)),
            pl.BlockSpec((N_DIR, tm), const2),
        ],
        out_specs=[
            pl.BlockSpec((1, 1, ATT_WIDTH, tm), t_map),
            pl.BlockSpec((ATT_KV_HEADS, tm, LANES), lambda t, b: (0, b * nt + t, 0)),
            pl.BlockSpec((1, 1, ATT_KV_HEADS, VT_ROWS, tm), lambda t, b: (b, t, 0, 0, 0)),
            row_out(ATT_WIDTH),
            row_out(RET_WIDTH),
            pl.BlockSpec((1, 1, RET_WIDTH, tm), t_map),
            row_out(RET_WIDTH),
            row_out(RET_WIDTH),
            pl.BlockSpec((1, 1, N_DIR, RET_HEAD_DIM, RET_HEAD_DIM), lambda t, b: (b, t, 0, 0, 0)),
        ],
        out_shape=[
            jax.ShapeDtypeStruct((batch, nt, ATT_WIDTH, tm), BF16),
            jax.ShapeDtypeStruct((ATT_KV_HEADS, rows, LANES), BF16),
            jax.ShapeDtypeStruct((batch, nt, ATT_KV_HEADS, VT_ROWS, tm), BF16),
            jax.ShapeDtypeStruct((rows, ATT_WIDTH), BF16),
            jax.ShapeDtypeStruct((rows, RET_WIDTH), BF16),
            jax.ShapeDtypeStruct((batch, nt, RET_WIDTH, tm), BF16),
            jax.ShapeDtypeStruct((rows, RET_WIDTH), BF16),
            jax.ShapeDtypeStruct((rows, RET_WIDTH), BF16),
            jax.ShapeDtypeStruct((batch, nt, N_DIR, RET_HEAD_DIM, RET_HEAD_DIM), F32),
        ],
        compiler_params=pltpu.CompilerParams(dimension_semantics=("arbitrary", "arbitrary"),
                                             vmem_limit_bytes=VMEM_LIMIT),
        name="inproj",
    )(x2, scale, shift, norm_w, w_in, qnw, knw, row_tab, col_tab, kdec)


def _scan_step(first, uc_ref, uf_ref, ub_ref, cdec_ref, sf_o, sb_o, st):
    nh = RET_HEADS
    n = uf_ref.shape[1]
    cdec = cdec_ref[...]

    @pl.when(first)
    def _():
        st[...] = uc_ref[0, 0]

    for t in range(n):
        tb = n - 1 - t
        for h in range(nh):
            s = st[h]
            sf_o[0, 0, t, h] = s.astype(BF16)
            st[h] = cdec[h:h + 1, :] * s + uf_ref[0, t, h]
            s = st[nh + h]
            sb_o[0, 0, tb, h] = s.astype(BF16)
            st[nh + h] = cdec[nh + h:nh + h + 1, :] * s + ub_ref[0, tb, h]


def _attn_kernel(qT_ref, kc_ref, vTc_ref, k_ref, vT_ref, ga_ref, uc_ref, uf_ref, ub_ref, cdec_ref,
                 o_ref, sf_o, sb_o, st, *, tq, tk, n_kv, bounded):
    _scan_step(pl.program_id(2) == 0, uc_ref, uf_ref, ub_ref, cdec_ref, sf_o, sb_o, st)
    hd = ATT_HEAD_DIM
    q_tiles = qT_ref.shape[1]
    wq = jnp.concatenate(
        [jnp.concatenate([qT_ref[0, t, h * hd:(h + 1) * hd, :]] * ATT_KV_HEADS, axis=0)
         for t in range(q_tiles) for h in range(ATT_GROUP)],
        axis=1)
    cols = q_tiles * ATT_GROUP * tq

    if bounded:
        uc = UNIT_COLS
        n_ct = cols // uc
        uk = UNIT_KEYS
        ctx_len = kc_ref.shape[1]
        kv = [(kc_ref[0, s:s + uk, :], vTc_ref[0, 0, 0, :, s:s + uk]) for s in range(0, ctx_len, uk)]
        kv += [(k_ref[0, j * tk + s:j * tk + s + uk, :], vT_ref[0, j, 0, :, s:s + uk])
               for j in range(n_kv) for s in range(0, tk, uk)]
        units = [(b, c) for b in range(len(kv)) for c in range(n_ct)]
        accs = [jnp.zeros((hd, uc), F32) for _ in range(n_ct)]
        dens = [jnp.zeros((SUBLANES, uc), F32) for _ in range(n_ct)]
        pending = []

        def value_matmul(b, c, sT):
            p = jnp.exp2(sT)
            dens[c] = dens[c] + p.reshape(p.shape[0] // SUBLANES, SUBLANES, uc).sum(axis=0)
            accs[c] = accs[c] + jnp.dot(kv[b][1], p.astype(BF16), preferred_element_type=F32)

        for b, c in units:
            sT = jnp.dot(kv[b][0], wq[:, c * uc:(c + 1) * uc],
                         preferred_element_type=F32)
            pending.append((b, c, sT))
            if len(pending) > PV_LAG:
                value_matmul(*pending.pop(0))
        for item in pending:
            value_matmul(*item)
        acc = jnp.concatenate(accs, axis=1)
        den = jnp.sum(jnp.concatenate(dens, axis=1), axis=0, keepdims=True)
    else:
        def block(carry, k, vT):
            m, den, acc = carry
            sT = jnp.dot(k, wq, preferred_element_type=F32)
            m_new = jnp.maximum(m, jnp.max(sT, axis=0, keepdims=True))
            p = jnp.exp2(sT - m_new)
            alpha = jnp.exp2(m - m_new)
            den = alpha * den + jnp.sum(p, axis=0, keepdims=True)
            acc = alpha * acc + jnp.dot(vT, p.astype(BF16), preferred_element_type=F32)
            return m_new, den, acc

        carry = (jnp.full((1, cols), -1e30, F32), jnp.zeros((1, cols), F32),
                 jnp.zeros((hd, cols), F32))
        carry = block(carry, kc_ref[0], vTc_ref[0, 0, 0])

        def body(j, carry):
            off = pl.multiple_of(j * tk, tk)
            return block(carry, k_ref[0, pl.ds(off, tk), :], vT_ref[0, j, 0])

        _, den, acc = lax.fori_loop(0, n_kv, body, carry)
    oT = acc * (1.0 / den)
    oT = jnp.concatenate([oT, jnp.zeros((LANES - hd, cols), F32)], axis=0)
    for t in range(q_tiles):
        base = t * ATT_GROUP * tq
        out = jnp.concatenate([oT[:, base + h * tq:base + (h + 1) * tq].T[:, 0:hd]
                               for h in range(ATT_GROUP)], axis=1)
        rs = slice(t * tq, (t + 1) * tq)
        o_ref[rs, :] = (out * ga_ref[rs, :].astype(F32)).astype(BF16)


def _run_attn(qT, kz_c, vT_c, kz, vT, ga, u_c, u, cdec, *, batch, seq, ctx_len, bounded):
    tq = TILE
    qt = ATT_Q_TILES
    nt = seq // tq
    nq = nt // qt
    n_kv = seq // TILE
    gw = ATT_GROUP * ATT_HEAD_DIM
    hd = RET_HEAD_DIM
    s_blk = (1, qt, RET_HEADS, hd, hd)
    s_out_blk = (1,) + s_blk
    s_shape = jax.ShapeDtypeStruct((ATT_KV_HEADS, batch, nt, RET_HEADS, hd, hd), BF16)
    kern = functools.partial(_attn_kernel, tq=tq, tk=TILE, n_kv=n_kv, bounded=bounded)
    return pl.pallas_call(
        kern,
        grid=(batch, ATT_KV_HEADS, nq),
        in_specs=[
            pl.BlockSpec((1, qt, gw, tq), lambda b, g, i: (b, i, g, 0)),
            pl.BlockSpec((1, ctx_len, LANES), lambda b, g, i: (g, b, 0)),
            pl.BlockSpec((1, 1, 1, VT_ROWS, ctx_len), lambda b, g, i: (b, 0, g, 0, 0)),
            pl.BlockSpec((1, seq, LANES), lambda b, g, i: (g, b, 0)),
            pl.BlockSpec((1, n_kv, 1, VT_ROWS, TILE), lambda b, g, i: (b, 0, g, 0, 0)),
            pl.BlockSpec((qt * tq, gw), lambda b, g, i: (b * nq + i, g)),
            pl.BlockSpec((1, 1, N_DIR, hd, hd), lambda b, g, i: (b, 0, 0, 0, 0)),
            pl.BlockSpec(s_blk, lambda b, g, i: (b, i, 0, 0, 0)),
            pl.BlockSpec(s_blk, lambda b, g, i: (b, nq - 1 - i, 1, 0, 0)),
            pl.BlockSpec((N_DIR, LANES), lambda b, g, i: (0, 0)),
        ],
        out_specs=[
            pl.BlockSpec((qt * tq, gw), lambda b, g, i: (b * nq + i, g)),
            pl.BlockSpec(s_out_blk, lambda b, g, i: (g, b, i, 0, 0, 0)),
            pl.BlockSpec(s_out_blk, lambda b, g, i: (g, b, nq - 1 - i, 0, 0, 0)),
        ],
        out_shape=[jax.ShapeDtypeStruct((batch * seq, ATT_WIDTH), BF16), s_shape, s_shape],
        scratch_shapes=[pltpu.VMEM((N_DIR, hd, hd), F32)],
        compiler_params=pltpu.CompilerParams(
            dimension_semantics=("arbitrary", "arbitrary", "arbitrary"),
            vmem_limit_bytes=VMEM_LIMIT),
        name="attn_bounded" if bounded else "attn_online",
    )(qT, kz_c, vT_c, kz, vT, ga, u_c, u, u, cdec)


def _epilogue_kernel(att_ref, qr_ref, krT_ref, vr_ref, gr_ref, x_ref, sf_ref, sb_ref,
                     dmask_ref, qdec_ref, gnw_ref, gate_ref, wout_ref, fnw_ref, y_ref, ret_scr):
    nh = RET_HEADS
    heads = [slice(h * RET_HEAD_DIM, (h + 1) * RET_HEAD_DIM) for h in range(nh)]
    a = [jnp.dot(qr_ref[:, hs], krT_ref[0, 0, hs, :], preferred_element_type=F32) * dmask_ref[h]
         for h, hs in enumerate(heads)]
    proj_att = jnp.dot(att_ref[...], wout_ref[0:ATT_WIDTH, :], preferred_element_type=F32)
    for h, hs in enumerate(heads):
        qf = qr_ref[:, hs].astype(F32)
        o = (jnp.dot((qf * qdec_ref[h]).astype(BF16), sf_ref[0, 0, 0, h],
                     preferred_element_type=F32)
             + jnp.dot((qf * qdec_ref[nh + h]).astype(BF16), sb_ref[0, 0, 0, h],
                       preferred_element_type=F32)
             + jnp.dot(a[h].astype(BF16), vr_ref[:, hs], preferred_element_type=F32))
        mu = jnp.mean(o, axis=-1, keepdims=True)
        d = o - mu
        var = jnp.mean(d * d, axis=-1, keepdims=True)
        yh = d * lax.rsqrt(var + NORM_EPS) * gnw_ref[:, hs] * gr_ref[:, hs].astype(F32)
        ret_scr[:, hs] = yh.astype(BF16)
    proj = proj_att + jnp.dot(ret_scr[...], wout_ref[ATT_WIDTH:, :], preferred_element_type=F32)
    z = x_ref[...] + gate_ref[0] * proj
    ms = jnp.mean(z * z, axis=-1, keepdims=True)
    y_ref[...] = z * lax.rsqrt(ms + NORM_EPS) * fnw_ref[...]


def _run_epilogue(att, qr, krT, vr, gr, x2, sf, sb, dmask, qdec, gnw, gate, w_out, fnw,
                  *, batch, seq):
    nt = seq // TILE
    row_map = lambda b, t: (b * nt + t, 0)
    const2 = lambda b, t: (0, 0)
    const3 = lambda b, t: (0, 0, 0)
    s_spec = pl.BlockSpec((1, 1, 1, RET_HEADS, RET_HEAD_DIM, RET_HEAD_DIM),
                          lambda b, t: (0, b, t, 0, 0, 0))
    return pl.pallas_call(
        _epilogue_kernel,
        grid=(batch, nt),
        in_specs=[
            pl.BlockSpec((TILE, ATT_WIDTH), row_map),
            pl.BlockSpec((TILE, RET_WIDTH), row_map),
            pl.BlockSpec((1, 1, RET_WIDTH, TILE), lambda b, t: (b, t, 0, 0)),
            pl.BlockSpec((TILE, RET_WIDTH), row_map),
            pl.BlockSpec((TILE, RET_WIDTH), row_map),
            pl.BlockSpec((TILE, D_MODEL), row_map),
            s_spec, s_spec,
            pl.BlockSpec((RET_HEADS, TILE, TILE), const3),
            pl.BlockSpec((N_DIR, TILE, LANES), const3),
            pl.BlockSpec((1, RET_WIDTH), const2),
            pl.BlockSpec((1, 1, D_MODEL), lambda b, t: (b, 0, 0)),
            pl.BlockSpec((ATT_WIDTH + RET_WIDTH, D_MODEL), const2),
            pl.BlockSpec((1, D_MODEL), const2),
        ],
        out_specs=pl.BlockSpec((TILE, D_MODEL), row_map),
        out_shape=jax.ShapeDtypeStruct((batch * seq, D_MODEL), F32),
        scratch_shapes=[pltpu.VMEM((TILE, RET_WIDTH), BF16)],
        compiler_params=pltpu.CompilerParams(dimension_semantics=("arbitrary", "arbitrary"),
                                             vmem_limit_bytes=VMEM_LIMIT),
        name="epilogue",
    )(att, qr, krT, vr, gr, x2, sf, sb, dmask, qdec, gnw, gate, w_out, fnw)


def _rope_tables(seq):
    rows = jnp.arange(seq // GRID_W, dtype=F32)
    cols = jnp.arange(GRID_W, dtype=F32)

    def parts(head_dim):
        n_axis = head_dim // 4
        inv_freq = ROPE_THETA ** (-jnp.arange(n_axis, dtype=F32) / n_axis)
        reps = LANES // head_dim

        def lanes(pos, first):
            ang = pos[:, None] * inv_freq
            z = jnp.zeros_like(ang)
            c = jnp.concatenate([jnp.cos(ang), z] if first else [z, jnp.cos(ang)], axis=-1)
            s = jnp.concatenate([jnp.sin(ang), z] if first else [z, jnp.sin(ang)], axis=-1)
            return (jnp.tile(jnp.concatenate([c, c], axis=-1), (1, reps)),
                    jnp.tile(jnp.concatenate([-s, s], axis=-1), (1, reps)))

        return lanes(rows, True), lanes(cols, False)

    (rca, rsa), (cca, csa) = parts(ATT_HEAD_DIM)
    (rcr, rsr), (ccr, csr) = parts(RET_HEAD_DIM)
    return jnp.stack([rca, rsa, rcr, rsr]), jnp.stack([cca, csa, ccr, csr])


def kernel(x, c, ctx, c_ctx, norm_w, w_mod, b_mod, w_in, q_norm_w, k_norm_w,
           ret_decay_fwd, ret_decay_bwd, ret_gn_w, w_out, final_norm_w):
    batch, seq, d = x.shape
    ctx_len = ctx.shape[1]
    depth = norm_w.shape[0]
    assert d == D_MODEL and depth == 1 and seq % (TILE * ATT_Q_TILES) == 0
    assert ctx_len % LANES == 0 and ctx_len <= TILE
    assert batch + 1 <= MOD_ROWS and w_in.shape[-1] == IN_WIDTH
    layer = 0

    cc = jnp.zeros((MOD_ROWS, D_MODEL), F32).at[:batch].set(c).at[batch].set(c_ctx)
    dl = jnp.broadcast_to(
        jnp.concatenate([ret_decay_fwd[layer], ret_decay_bwd[layer]]).astype(F32)[:, None],
        (N_DIR, LANES))
    mod, dmask, qdec, kdec, kdec_c, cdec = _run_mod(
        cc, w_mod[layer], b_mod[layer].reshape(3, 1, D_MODEL), dl, ctx_len=ctx_len)
    shift, scale, gate = mod[0], mod[1], mod[2]

    w_in_b = w_in[layer].astype(BF16)
    nw = norm_w[layer].reshape(1, D_MODEL)
    qnw = jnp.tile(q_norm_w[layer], LANES // ATT_HEAD_DIM).reshape(1, LANES)
    knw = jnp.tile(k_norm_w[layer], LANES // ATT_HEAD_DIM).reshape(1, LANES)

    ctx_rows = ctx_len // GRID_W
    one = jnp.ones((ctx_rows, LANES), F32)
    zero = jnp.zeros((ctx_rows, LANES), F32)
    row_tab_c = jnp.stack([one, zero, one, zero])
    col_tab_c = jnp.zeros((4, ctx_len, LANES), F32)
    sc_c = jnp.broadcast_to(scale[batch].reshape(1, 1, D_MODEL), (batch, 1, D_MODEL))
    sh_c = jnp.broadcast_to(shift[batch].reshape(1, 1, D_MODEL), (batch, 1, D_MODEL))
    (_, kz_c, vT_c, _, _, _, _, _, u_c) = _run_inproj(
        ctx.reshape(batch * ctx_len, d), sc_c, sh_c, nw, w_in_b, qnw, knw,
        row_tab_c, col_tab_c, kdec_c, batch=batch, seq=ctx_len, tm=ctx_len)

    x2 = x.reshape(batch * seq, d)
    sc_l = scale[:batch].reshape(batch, 1, D_MODEL)
    sh_l = shift[:batch].reshape(batch, 1, D_MODEL)
    row_tab, col_tab = _rope_tables(seq)
    col_tab = jnp.tile(col_tab, (1, TILE // GRID_W, 1))
    qT, kz, vT, ga, qr, krT, vr, gr, u = _run_inproj(
        x2, sc_l, sh_l, nw, w_in_b, qnw, knw, row_tab, col_tab, kdec,
        batch=batch, seq=seq, tm=TILE)

    score_bound = (ATT_HEAD_DIM * Q_SCALE_LOG2 * BF16_ROUND_UP ** 2
                   * jnp.max(jnp.abs(q_norm_w[layer])) * jnp.max(jnp.abs(k_norm_w[layer])))
    attn = functools.partial(_run_attn, batch=batch, seq=seq, ctx_len=ctx_len)
    att, sf, sb = lax.cond(score_bound <= SCORE_LOG2_LIMIT,
                           functools.partial(attn, bounded=True),
                           functools.partial(attn, bounded=False),
                           qT, kz_c, vT_c, kz, vT, ga, u_c, u, cdec)
    y = _run_epilogue(att, qr, krT, vr, gr, x2, sf, sb, dmask, qdec,
                      ret_gn_w[layer].reshape(1, RET_WIDTH),
                      gate[:batch].reshape(batch, 1, D_MODEL),
                      w_out[layer].astype(BF16), final_norm_w.reshape(1, D_MODEL),
                      batch=batch, seq=seq)
    return y.reshape(batch, seq, d)
```

```python
import functools

import jax
import jax.numpy as jnp
from jax import lax
from jax.experimental import pallas as pl
from jax.experimental.pallas import tpu as pltpu

F32 = jnp.float32
BF16 = jnp.bfloat16

D_MODEL = 1024
GRID_W = 64
ATT_HEADS = 8
ATT_KV_HEADS = 2
ATT_GROUP = ATT_HEADS // ATT_KV_HEADS
ATT_HEAD_DIM = 64
ATT_WIDTH = ATT_HEADS * ATT_HEAD_DIM
ATT_KV_WIDTH = ATT_KV_HEADS * ATT_HEAD_DIM
RET_HEADS = 4
RET_HEAD_DIM = 128
RET_WIDTH = RET_HEADS * RET_HEAD_DIM
ROPE_THETA = 10000.0
NORM_EPS = 1e-6
LANES = 128
TILE = 512
MOD_ROWS = 8
VT_ROWS = ATT_HEAD_DIM
SUBLANES = 8
N_DIR = 2 * RET_HEADS
MXU_TILE = 256
UNIT_COLS = MXU_TILE
UNIT_KEYS = 256
IN_TILES = 1
EPI_TILES = 2
ATT_Q_TILES = 4
PV_LAG = 4

O_QA = 0
O_KA = O_QA + ATT_WIDTH
O_VA = O_KA + ATT_KV_WIDTH
O_GA = O_VA + ATT_KV_WIDTH
O_QR = O_GA + ATT_WIDTH
O_KR = O_QR + RET_WIDTH
O_VR = O_KR + RET_WIDTH
O_GR = O_VR + RET_WIDTH
IN_WIDTH = O_GR + RET_WIDTH

VMEM_LIMIT = 48 * 1024 * 1024

LOG2_E = 1.4426950408889634
Q_SCALE_LOG2 = ATT_HEAD_DIM ** -0.5 * LOG2_E
SCORE_LOG2_LIMIT = 100.0
BF16_ROUND_UP = 1.0 + 2.0 ** -8


def _silu(x):
    return x * (1.0 / (1.0 + jnp.exp(-x)))


def _silu_tanh(x):
    hx = 0.5 * x
    return hx + hx * jnp.tanh(hx)


def _mod_kernel(cc_ref, w_ref, b_ref, dl_ref,
                mod_ref, dmask_ref, qdec_ref, kdec_ref, kdec_c_ref, cdec_ref):
    sc = _silu(cc_ref[...])
    mod_ref[0] = jnp.dot(sc, w_ref[...], preferred_element_type=F32,
                         precision=lax.Precision.HIGHEST) + b_ref[0]

    @pl.when(pl.program_id(0) == 0)
    def _():
        x = dl_ref[...]
        lg = jnp.minimum(x, 0.0) - jnp.log1p(jnp.exp(-jnp.abs(x)))
        n = float(TILE)
        ii = lax.broadcasted_iota(jnp.int32, (TILE, TILE), 0).astype(F32)
        jj = lax.broadcasted_iota(jnp.int32, (TILE, TILE), 1).astype(F32)
        d = ii - jj
        i_col = lax.broadcasted_iota(jnp.int32, (TILE, LANES), 0).astype(F32)
        for h in range(RET_HEADS):
            lf = lg[h:h + 1, :]
            lb = lg[RET_HEADS + h:RET_HEADS + h + 1, :]
            lf_t = jnp.concatenate([lf] * (TILE // LANES), axis=1)
            lb_t = jnp.concatenate([lb] * (TILE // LANES), axis=1)
            fwd = jnp.exp(lf_t * jnp.maximum(d, 0.0))
            bwd = jnp.exp(lb_t * jnp.maximum(-d, 0.0))
            dmask_ref[h] = jnp.where(d > 0, fwd, jnp.where(d < 0, bwd, 2.0))
            qdec_ref[h] = jnp.exp(lf * (i_col + 1.0))
            qdec_ref[RET_HEADS + h] = jnp.exp(lb * (n - i_col))
        for ref in (kdec_ref, kdec_c_ref):
            length = ref.shape[1]
            tok = lax.broadcasted_iota(jnp.int32, (N_DIR, length), 1).astype(F32)
            row = lax.broadcasted_iota(jnp.int32, (N_DIR, length), 0)
            lg_t = jnp.concatenate([lg] * (length // LANES), axis=1)
            ref[...] = jnp.where(row < RET_HEADS, jnp.exp(lg_t * (float(length) - 1.0 - tok)),
                                 jnp.exp(lg_t * tok))
        cdec_ref[...] = jnp.exp(lg * n)


def _run_mod(cc, w_mod, b_mod, dl, *, ctx_len):
    n3 = 3
    const2 = lambda j: (0, 0)
    const3 = lambda j: (0, 0, 0)
    return pl.pallas_call(
        _mod_kernel,
        grid=(n3,),
        in_specs=[
            pl.BlockSpec((MOD_ROWS, D_MODEL), const2),
            pl.BlockSpec((D_MODEL, D_MODEL), lambda j: (0, j)),
            pl.BlockSpec((1, 1, D_MODEL), lambda j: (j, 0, 0)),
            pl.BlockSpec((N_DIR, LANES), const2),
        ],
        out_specs=[
            pl.BlockSpec((1, MOD_ROWS, D_MODEL), lambda j: (j, 0, 0)),
            pl.BlockSpec((RET_HEADS, TILE, TILE), const3),
            pl.BlockSpec((N_DIR, TILE, LANES), const3),
            pl.BlockSpec((N_DIR, TILE), const2),
            pl.BlockSpec((N_DIR, ctx_len), const2),
            pl.BlockSpec((N_DIR, LANES), const2),
        ],
        out_shape=[
            jax.ShapeDtypeStruct((n3, MOD_ROWS, D_MODEL), F32),
            jax.ShapeDtypeStruct((RET_HEADS, TILE, TILE), F32),
            jax.ShapeDtypeStruct((N_DIR, TILE, LANES), F32),
            jax.ShapeDtypeStruct((N_DIR, TILE), F32),
            jax.ShapeDtypeStruct((N_DIR, ctx_len), F32),
            jax.ShapeDtypeStruct((N_DIR, LANES), F32),
        ],
        compiler_params=pltpu.CompilerParams(dimension_semantics=("arbitrary",),
                                             vmem_limit_bytes=VMEM_LIMIT),
        name="mod",
    )(cc, w_mod, b_mod, dl)


def _inproj_kernel(x_ref, sc_ref, sh_ref, nw_ref, w_ref, qnw_ref, knw_ref, rt_ref, ct_ref, kdec_ref,
                   qT_o, kz_o, vT_o, ga_o, qr_o, krT_o, vr_o, gr_o, u_o):
    tm = ct_ref.shape[1]
    for s in range(x_ref.shape[0] // tm):
        _inproj_tile(s, tm, x_ref, sc_ref, sh_ref, nw_ref, w_ref, qnw_ref, knw_ref, rt_ref, ct_ref,
                     kdec_ref, qT_o, kz_o, vT_o, ga_o, qr_o, krT_o, vr_o, gr_o, u_o)


def _inproj_tile(s, tm, x_ref, sc_ref, sh_ref, nw_ref, w_ref, qnw_ref, knw_ref, rt_ref, ct_ref,
                 kdec_ref, qT_o, kz_o, vT_o, ga_o, qr_o, krT_o, vr_o, gr_o, u_o):
    rs = slice(s * tm, (s + 1) * tm)
    x = x_ref[rs, :]
    ms = jnp.mean(x * x, axis=-1, keepdims=True)
    a = nw_ref[...] * (1.0 + sc_ref[0])
    hb = ((x * lax.rsqrt(ms + NORM_EPS)) * a + sh_ref[0]).astype(BF16)

    def proj(lo, width):
        return jnp.dot(hb, w_ref[:, lo:lo + width], preferred_element_type=F32)

    grid_rows = tm // GRID_W

    def rope_table(k):
        rt = rt_ref[k, s * grid_rows:(s + 1) * grid_rows, :]
        rows = [jnp.broadcast_to(rt[i:i + 1, :], (GRID_W, LANES)) for i in range(grid_rows)]
        return jnp.concatenate(rows, axis=0) + ct_ref[k]

    ca, sa, cr, sr = (rope_table(k) for k in range(4))
    lane = lax.broadcasted_iota(jnp.int32, (tm, LANES), 1)
    head_lo = lane < ATT_HEAD_DIM
    half_lo = (lane & (ATT_HEAD_DIM // 2)) == 0
    inv_hd = 1.0 / ATT_HEAD_DIM

    def att_norm_rope(v, w):
        sq = v * v
        s_lo = jnp.sum(jnp.where(head_lo, sq, 0.0), axis=-1, keepdims=True)
        s_hi = jnp.sum(jnp.where(head_lo, 0.0, sq), axis=-1, keepdims=True)
        r = jnp.where(head_lo, lax.rsqrt(s_lo * inv_hd + NORM_EPS),
                      lax.rsqrt(s_hi * inv_hd + NORM_EPS))
        vn = v * r * w
        rot = jnp.where(half_lo, pltpu.roll(vn, LANES - ATT_HEAD_DIM // 2, 1),
                        pltpu.roll(vn, ATT_HEAD_DIM // 2, 1))
        return vn * ca + rot * sa

    def ret_rope(v):
        return v * cr + pltpu.roll(v, RET_HEAD_DIM // 2, 1) * sr

    qnw = qnw_ref[...]
    qa = proj(O_QA, ATT_WIDTH)
    for c in range(ATT_WIDTH // LANES):
        cs = slice(c * LANES, (c + 1) * LANES)
        qT = (att_norm_rope(qa[:, cs], qnw) * Q_SCALE_LOG2).T
        qT_o[0, s, cs, :] = qT.astype(BF16)

    ka = att_norm_rope(proj(O_KA, ATT_KV_WIDTH), knw_ref[...])
    kz_o[0, rs, :] = jnp.where(head_lo, ka, 0.0).astype(BF16)
    kz_o[1, rs, :] = jnp.where(head_lo, 0.0, ka).astype(BF16)

    va = proj(O_VA, ATT_KV_WIDTH)
    for g in range(ATT_KV_HEADS):
        vg = va if g == 0 else pltpu.roll(va, ATT_HEAD_DIM, 1)
        vT_o[0, s, g] = jnp.where(head_lo, vg, 0.0).T[0:ATT_HEAD_DIM, :].astype(BF16)

    ga_o[rs, :] = _silu_tanh(proj(O_GA, ATT_WIDTH)).astype(BF16)

    vr = proj(O_VR, RET_WIDTH).astype(BF16)
    vr_o[rs, :] = vr
    qr = proj(O_QR, RET_WIDTH)
    kr = proj(O_KR, RET_WIDTH)
    kdec = kdec_ref[...]
    krT = []
    for h in range(RET_HEADS):
        hs = slice(h * RET_HEAD_DIM, (h + 1) * RET_HEAD_DIM)
        qr_o[rs, hs] = ret_rope(qr[:, hs]).astype(BF16)
        krT.append((ret_rope(kr[:, hs]) * (RET_HEAD_DIM ** -0.5)).T)
        krT_o[0, s, hs, :] = krT[h].astype(BF16)

    gr_o[rs, :] = _silu_tanh(proj(O_GR, RET_WIDTH)).astype(BF16)

    for h in range(RET_HEADS):
        hs = slice(h * RET_HEAD_DIM, (h + 1) * RET_HEAD_DIM)
        for row in (h, RET_HEADS + h):
            kd = (krT[h] * kdec[row:row + 1, :]).astype(BF16)
            u_o[0, s, row] = jnp.dot(kd, vr[:, hs], preferred_element_type=F32)


def _run_inproj(x2, scale, shift, norm_w, w_in, qnw, knw, row_tab, col_tab, kdec,
                *, batch, seq, tm, tiles_per_step):
    nt = seq // tm
    ts = tiles_per_step
    ns = nt // ts
    rows = batch * seq
    row_map = lambda t, b: (b * ns + t, 0)
    const2 = lambda t, b: (0, 0)
    vec_spec = pl.BlockSpec((1, 1, D_MODEL), lambda t, b: (b, 0, 0))
    t_map = lambda t, b: (b, t, 0, 0)
    once = pl.Buffered(1)

    def row_out(width):
        return pl.BlockSpec((ts * tm, width), row_map)

    return pl.pallas_call(
        _inproj_kernel,
        grid=(ns, batch),
        in_specs=[
            pl.BlockSpec((ts * tm, D_MODEL), row_map),
            vec_spec, vec_spec,
            pl.BlockSpec((1, D_MODEL), const2),
            pl.BlockSpec((D_MODEL, IN_WIDTH), const2, pipeline_mode=once),
            pl.BlockSpec((1, LANES), const2),
            pl.BlockSpec((1, LANES), const2),
            pl.BlockSpec((4, ts * tm // GRID_W, LANES), lambda t, b: (0, t, 0)),
            pl.BlockSpec((4, tm, LANES), lambda t, b: (0, 0, 0), pipeline_mode=once),
            pl.BlockSpec((N_DIR, tm), const2),
        ],
        out_specs=[
            pl.BlockSpec((1, ts, ATT_WIDTH, tm), t_map),
            pl.BlockSpec((ATT_KV_HEADS, ts * tm, LANES), lambda t, b: (0, b * ns + t, 0)),
            pl.BlockSpec((1, ts, ATT_KV_HEADS, VT_ROWS, tm), lambda t, b: (b, t, 0, 0, 0)),
            row_out(ATT_WIDTH),
            row_out(RET_WIDTH),
            pl.BlockSpec((1, ts, RET_WIDTH, tm), t_map),
            row_out(RET_WIDTH),
            row_out(RET_WIDTH),
            pl.BlockSpec((1, ts, N_DIR, RET_HEAD_DIM, RET_HEAD_DIM), lambda t, b: (b, t, 0, 0, 0)),
        ],
        out_shape=[
            jax.ShapeDtypeStruct((batch, nt, ATT_WIDTH, tm), BF16),
            jax.ShapeDtypeStruct((ATT_KV_HEADS, rows, LANES), BF16),
            jax.ShapeDtypeStruct((batch, nt, ATT_KV_HEADS, VT_ROWS, tm), BF16),
            jax.ShapeDtypeStruct((rows, ATT_WIDTH), BF16),
            jax.ShapeDtypeStruct((rows, RET_WIDTH), BF16),
            jax.ShapeDtypeStruct((batch, nt, RET_WIDTH, tm), BF16),
            jax.ShapeDtypeStruct((rows, RET_WIDTH), BF16),
            jax.ShapeDtypeStruct((rows, RET_WIDTH), BF16),
            jax.ShapeDtypeStruct((batch, nt, N_DIR, RET_HEAD_DIM, RET_HEAD_DIM), F32),
        ],
        compiler_params=pltpu.CompilerParams(dimension_semantics=("arbitrary", "arbitrary"),
                                             vmem_limit_bytes=VMEM_LIMIT),
        name="inproj",
    )(x2, scale, shift, norm_w, w_in, qnw, knw, row_tab, col_tab, kdec)


def _scan_step(first, uc_ref, uf_ref, ub_ref, cdec_ref, sf_o, sb_o, st):
    nh = RET_HEADS
    n = uf_ref.shape[1]
    cdec = cdec_ref[...]

    @pl.when(first)
    def _():
        st[...] = uc_ref[0, 0]

    for t in range(n):
        tb = n - 1 - t
        for h in range(nh):
            s = st[h]
            sf_o[0, 0, t, h] = s.astype(BF16)
            st[h] = cdec[h:h + 1, :] * s + uf_ref[0, t, h]
            s = st[nh + h]
            sb_o[0, 0, tb, h] = s.astype(BF16)
            st[nh + h] = cdec[nh + h:nh + h + 1, :] * s + ub_ref[0, tb, h]


def _attn_kernel(qT_ref, kc_ref, vTc_ref, k_ref, vT_ref, ga_ref, uc_ref, uf_ref, ub_ref, cdec_ref,
                 o_ref, sf_o, sb_o, st, *, tq, tk, n_kv, bounded):
    _scan_step(pl.program_id(2) == 0, uc_ref, uf_ref, ub_ref, cdec_ref, sf_o, sb_o, st)
    hd = ATT_HEAD_DIM
    q_tiles = qT_ref.shape[1]
    wq = jnp.concatenate(
        [jnp.concatenate([qT_ref[0, t, h * hd:(h + 1) * hd, :]] * ATT_KV_HEADS, axis=0)
         for t in range(q_tiles) for h in range(ATT_GROUP)],
        axis=1)
    cols = q_tiles * ATT_GROUP * tq

    if bounded:
        uc = UNIT_COLS
        n_ct = cols // uc
        uk = UNIT_KEYS
        ctx_len = kc_ref.shape[1]
        kv = [(kc_ref[0, s:s + uk, :], vTc_ref[0, 0, 0, :, s:s + uk]) for s in range(0, ctx_len, uk)]
        kv += [(k_ref[0, j * tk + s:j * tk + s + uk, :], vT_ref[0, j, 0, :, s:s + uk])
               for j in range(n_kv) for s in range(0, tk, uk)]
        ct_per_tile = n_ct // q_tiles
        units = [(b, t * ct_per_tile + c)
                 for t in range(q_tiles) for b in range(len(kv)) for c in range(ct_per_tile)]
        accs = [jnp.zeros((hd, uc), F32) for _ in range(n_ct)]
        dens = [jnp.zeros((SUBLANES, uc), F32) for _ in range(n_ct)]
        last_unit = {t: (len(kv) - 1, (t + 1) * ct_per_tile - 1) for t in range(q_tiles)}
        pending = []

        def finish_tile(t):
            cs = slice(t * ct_per_tile, (t + 1) * ct_per_tile)
            den = jnp.sum(jnp.concatenate(dens[cs], axis=1), axis=0, keepdims=True)
            _attn_store_tile(t, jnp.concatenate(accs[cs], axis=1) * (1.0 / den), ga_ref, o_ref, tq)

        def value_matmul(b, c, sT):
            p = jnp.exp2(sT)
            dens[c] = dens[c] + p.reshape(p.shape[0] // SUBLANES, SUBLANES, uc).sum(axis=0)
            accs[c] = accs[c] + jnp.dot(kv[b][1], p.astype(BF16), preferred_element_type=F32)
            t = c // ct_per_tile
            if (b, c) == last_unit[t]:
                finish_tile(t)

        for b, c in units:
            sT = jnp.dot(kv[b][0], wq[:, c * uc:(c + 1) * uc],
                         preferred_element_type=F32)
            pending.append((b, c, sT))
            if len(pending) > PV_LAG:
                value_matmul(*pending.pop(0))
        for item in pending:
            value_matmul(*item)
    else:
        def block(carry, k, vT):
            m, den, acc = carry
            sT = jnp.dot(k, wq, preferred_element_type=F32)
            m_new = jnp.maximum(m, jnp.max(sT, axis=0, keepdims=True))
            p = jnp.exp2(sT - m_new)
            alpha = jnp.exp2(m - m_new)
            den = alpha * den + jnp.sum(p, axis=0, keepdims=True)
            acc = alpha * acc + jnp.dot(vT, p.astype(BF16), preferred_element_type=F32)
            return m_new, den, acc

        carry = (jnp.full((1, cols), -1e30, F32), jnp.zeros((1, cols), F32),
                 jnp.zeros((hd, cols), F32))
        carry = block(carry, kc_ref[0], vTc_ref[0, 0, 0])

        def body(j, carry):
            off = pl.multiple_of(j * tk, tk)
            return block(carry, k_ref[0, pl.ds(off, tk), :], vT_ref[0, j, 0])

        _, den, acc = lax.fori_loop(0, n_kv, body, carry)
        oT = acc * (1.0 / den)
        tile_cols = ATT_GROUP * tq
        for t in range(q_tiles):
            _attn_store_tile(t, oT[:, t * tile_cols:(t + 1) * tile_cols], ga_ref, o_ref, tq)


def _attn_store_tile(t, oT, ga_ref, o_ref, tq):
    hd = ATT_HEAD_DIM
    oT = jnp.concatenate([oT, jnp.zeros((LANES - hd, oT.shape[1]), F32)], axis=0)
    out = jnp.concatenate([oT[:, h * tq:(h + 1) * tq].T[:, 0:hd] for h in range(ATT_GROUP)], axis=1)
    rs = slice(t * tq, (t + 1) * tq)
    o_ref[rs, :] = (out * ga_ref[rs, :].astype(F32)).astype(BF16)


def _run_attn(qT, kz_c, vT_c, kz, vT, ga, u_c, u, cdec, *, batch, seq, ctx_len, bounded):
    tq = TILE
    qt = ATT_Q_TILES
    nt = seq // tq
    nq = nt // qt
    n_kv = seq // TILE
    gw = ATT_GROUP * ATT_HEAD_DIM
    hd = RET_HEAD_DIM
    s_blk = (1, qt, RET_HEADS, hd, hd)
    s_out_blk = (1,) + s_blk
    s_shape = jax.ShapeDtypeStruct((ATT_KV_HEADS, batch, nt, RET_HEADS, hd, hd), BF16)
    kern = functools.partial(_attn_kernel, tq=tq, tk=TILE, n_kv=n_kv, bounded=bounded)
    return pl.pallas_call(
        kern,
        grid=(batch, ATT_KV_HEADS, nq),
        in_specs=[
            pl.BlockSpec((1, qt, gw, tq), lambda b, g, i: (b, i, g, 0)),
            pl.BlockSpec((1, ctx_len, LANES), lambda b, g, i: (g, b, 0)),
            pl.BlockSpec((1, 1, 1, VT_ROWS, ctx_len), lambda b, g, i: (b, 0, g, 0, 0)),
            pl.BlockSpec((1, seq, LANES), lambda b, g, i: (g, b, 0)),
            pl.BlockSpec((1, n_kv, 1, VT_ROWS, TILE), lambda b, g, i: (b, 0, g, 0, 0)),
            pl.BlockSpec((qt * tq, gw), lambda b, g, i: (b * nq + i, g)),
            pl.BlockSpec((1, 1, N_DIR, hd, hd), lambda b, g, i: (b, 0, 0, 0, 0)),
            pl.BlockSpec(s_blk, lambda b, g, i: (b, i, 0, 0, 0)),
            pl.BlockSpec(s_blk, lambda b, g, i: (b, nq - 1 - i, 1, 0, 0)),
            pl.BlockSpec((N_DIR, LANES), lambda b, g, i: (0, 0)),
        ],
        out_specs=[
            pl.BlockSpec((qt * tq, gw), lambda b, g, i: (b * nq + i, g)),
            pl.BlockSpec(s_out_blk, lambda b, g, i: (g, b, i, 0, 0, 0)),
            pl.BlockSpec(s_out_blk, lambda b, g, i: (g, b, nq - 1 - i, 0, 0, 0)),
        ],
        out_shape=[jax.ShapeDtypeStruct((batch * seq, ATT_WIDTH), BF16), s_shape, s_shape],
        scratch_shapes=[pltpu.VMEM((N_DIR, hd, hd), F32)],
        compiler_params=pltpu.CompilerParams(
            dimension_semantics=("arbitrary", "arbitrary", "arbitrary"),
            vmem_limit_bytes=VMEM_LIMIT),
        name="attn_bounded" if bounded else "attn_online",
    )(qT, kz_c, vT_c, kz, vT, ga, u_c, u, u, cdec)


def _epilogue_kernel(att_ref, qr_ref, krT_ref, vr_ref, gr_ref, x_ref, sf_ref, sb_ref,
                     dmask_ref, qdec_ref, gnw_ref, gate_ref, wout_ref, fnw_ref, y_ref, ret_scr):
    nh = RET_HEADS
    heads = [slice(h * RET_HEAD_DIM, (h + 1) * RET_HEAD_DIM) for h in range(nh)]
    for s in range(krT_ref.shape[1]):
        rs = slice(s * TILE, (s + 1) * TILE)
        a = [jnp.dot(qr_ref[rs, hs], krT_ref[0, s, hs, :], preferred_element_type=F32) * dmask_ref[h]
             for h, hs in enumerate(heads)]
        proj_att = jnp.dot(att_ref[rs, :], wout_ref[0:ATT_WIDTH, :], preferred_element_type=F32)
        for h, hs in enumerate(heads):
            qf = qr_ref[rs, hs].astype(F32)
            o = (jnp.dot((qf * qdec_ref[h]).astype(BF16), sf_ref[0, 0, s, h],
                         preferred_element_type=F32)
                 + jnp.dot((qf * qdec_ref[nh + h]).astype(BF16), sb_ref[0, 0, s, h],
                           preferred_element_type=F32)
                 + jnp.dot(a[h].astype(BF16), vr_ref[rs, hs], preferred_element_type=F32))
            mu = jnp.mean(o, axis=-1, keepdims=True)
            d = o - mu
            var = jnp.mean(d * d, axis=-1, keepdims=True)
            yh = d * lax.rsqrt(var + NORM_EPS) * gnw_ref[:, hs] * gr_ref[rs, hs].astype(F32)
            ret_scr[rs, hs] = yh.astype(BF16)
        proj = proj_att + jnp.dot(ret_scr[rs, :], wout_ref[ATT_WIDTH:, :],
                                  preferred_element_type=F32)
        z = x_ref[rs, :] + gate_ref[0] * proj
        ms = jnp.mean(z * z, axis=-1, keepdims=True)
        y_ref[rs, :] = z * lax.rsqrt(ms + NORM_EPS) * fnw_ref[...]


def _run_epilogue(att, qr, krT, vr, gr, x2, sf, sb, dmask, qdec, gnw, gate, w_out, fnw,
                  *, batch, seq):
    et = EPI_TILES
    rows = et * TILE
    nt = seq // rows
    row_map = lambda b, t: (b * nt + t, 0)
    const2 = lambda b, t: (0, 0)
    const3 = lambda b, t: (0, 0, 0)
    s_spec = pl.BlockSpec((1, 1, et, RET_HEADS, RET_HEAD_DIM, RET_HEAD_DIM),
                          lambda b, t: (0, b, t, 0, 0, 0))
    return pl.pallas_call(
        _epilogue_kernel,
        grid=(batch, nt),
        in_specs=[
            pl.BlockSpec((rows, ATT_WIDTH), row_map),
            pl.BlockSpec((rows, RET_WIDTH), row_map),
            pl.BlockSpec((1, et, RET_WIDTH, TILE), lambda b, t: (b, t, 0, 0)),
            pl.BlockSpec((rows, RET_WIDTH), row_map),
            pl.BlockSpec((rows, RET_WIDTH), row_map),
            pl.BlockSpec((rows, D_MODEL), row_map),
            s_spec, s_spec,
            pl.BlockSpec((RET_HEADS, TILE, TILE), const3),
            pl.BlockSpec((N_DIR, TILE, LANES), const3),
            pl.BlockSpec((1, RET_WIDTH), const2),
            pl.BlockSpec((1, 1, D_MODEL), lambda b, t: (b, 0, 0)),
            pl.BlockSpec((ATT_WIDTH + RET_WIDTH, D_MODEL), const2),
            pl.BlockSpec((1, D_MODEL), const2),
        ],
        out_specs=pl.BlockSpec((rows, D_MODEL), row_map),
        out_shape=jax.ShapeDtypeStruct((batch * seq, D_MODEL), F32),
        scratch_shapes=[pltpu.VMEM((rows, RET_WIDTH), BF16)],
        compiler_params=pltpu.CompilerParams(dimension_semantics=("arbitrary", "arbitrary"),
                                             vmem_limit_bytes=VMEM_LIMIT),
        name="epilogue",
    )(att, qr, krT, vr, gr, x2, sf, sb, dmask, qdec, gnw, gate, w_out, fnw)


def _rope_tables(seq):
    rows = jnp.arange(seq // GRID_W, dtype=F32)
    cols = jnp.arange(GRID_W, dtype=F32)

    def parts(head_dim):
        n_axis = head_dim // 4
        inv_freq = ROPE_THETA ** (-jnp.arange(n_axis, dtype=F32) / n_axis)
        reps = LANES // head_dim

        def lanes(pos, first):
            ang = pos[:, None] * inv_freq
            z = jnp.zeros_like(ang)
            c = jnp.concatenate([jnp.cos(ang), z] if first else [z, jnp.cos(ang)], axis=-1)
            s = jnp.concatenate([jnp.sin(ang), z] if first else [z, jnp.sin(ang)], axis=-1)
            return (jnp.tile(jnp.concatenate([c, c], axis=-1), (1, reps)),
                    jnp.tile(jnp.concatenate([-s, s], axis=-1), (1, reps)))

        return lanes(rows, True), lanes(cols, False)

    (rca, rsa), (cca, csa) = parts(ATT_HEAD_DIM)
    (rcr, rsr), (ccr, csr) = parts(RET_HEAD_DIM)
    return jnp.stack([rca, rsa, rcr, rsr]), jnp.stack([cca, csa, ccr, csr])


def kernel(x, c, ctx, c_ctx, norm_w, w_mod, b_mod, w_in, q_norm_w, k_norm_w,
           ret_decay_fwd, ret_decay_bwd, ret_gn_w, w_out, final_norm_w):
    batch, seq, d = x.shape
    ctx_len = ctx.shape[1]
    depth = norm_w.shape[0]
    assert d == D_MODEL and depth == 1 and seq % (TILE * ATT_Q_TILES) == 0
    assert ctx_len % LANES == 0 and ctx_len <= TILE
    assert batch + 1 <= MOD_ROWS and w_in.shape[-1] == IN_WIDTH
    layer = 0

    cc = jnp.zeros((MOD_ROWS, D_MODEL), F32).at[:batch].set(c).at[batch].set(c_ctx)
    dl = jnp.broadcast_to(
        jnp.concatenate([ret_decay_fwd[layer], ret_decay_bwd[layer]]).astype(F32)[:, None],
        (N_DIR, LANES))
    mod, dmask, qdec, kdec, kdec_c, cdec = _run_mod(
        cc, w_mod[layer], b_mod[layer].reshape(3, 1, D_MODEL), dl, ctx_len=ctx_len)
    shift, scale, gate = mod[0], mod[1], mod[2]

    w_in_b = w_in[layer].astype(BF16)
    nw = norm_w[layer].reshape(1, D_MODEL)
    qnw = jnp.tile(q_norm_w[layer], LANES // ATT_HEAD_DIM).reshape(1, LANES)
    knw = jnp.tile(k_norm_w[layer], LANES // ATT_HEAD_DIM).reshape(1, LANES)

    ctx_rows = ctx_len // GRID_W
    one = jnp.ones((ctx_rows, LANES), F32)
    zero = jnp.zeros((ctx_rows, LANES), F32)
    row_tab_c = jnp.stack([one, zero, one, zero])
    col_tab_c = jnp.zeros((4, ctx_len, LANES), F32)
    sc_c = jnp.broadcast_to(scale[batch].reshape(1, 1, D_MODEL), (batch, 1, D_MODEL))
    sh_c = jnp.broadcast_to(shift[batch].reshape(1, 1, D_MODEL), (batch, 1, D_MODEL))
    (_, kz_c, vT_c, _, _, _, _, _, u_c) = _run_inproj(
        ctx.reshape(batch * ctx_len, d), sc_c, sh_c, nw, w_in_b, qnw, knw,
        row_tab_c, col_tab_c, kdec_c, batch=batch, seq=ctx_len, tm=ctx_len, tiles_per_step=1)

    x2 = x.reshape(batch * seq, d)
    sc_l = scale[:batch].reshape(batch, 1, D_MODEL)
    sh_l = shift[:batch].reshape(batch, 1, D_MODEL)
    row_tab, col_tab = _rope_tables(seq)
    col_tab = jnp.tile(col_tab, (1, TILE // GRID_W, 1))
    qT, kz, vT, ga, qr, krT, vr, gr, u = _run_inproj(
        x2, sc_l, sh_l, nw, w_in_b, qnw, knw, row_tab, col_tab, kdec,
        batch=batch, seq=seq, tm=TILE, tiles_per_step=IN_TILES)

    score_bound = (ATT_HEAD_DIM * Q_SCALE_LOG2 * BF16_ROUND_UP ** 2
                   * jnp.max(jnp.abs(q_norm_w[layer])) * jnp.max(jnp.abs(k_norm_w[layer])))
    attn = functools.partial(_run_attn, batch=batch, seq=seq, ctx_len=ctx_len)
    att, sf, sb = lax.cond(score_bound <= SCORE_LOG2_LIMIT,
                           functools.partial(attn, bounded=True),
                           functools.partial(attn, bounded=False),
                           qT, kz_c, vT_c, kz, vT, ga, u_c, u, cdec)
    y = _run_epilogue(att, qr, krT, vr, gr, x2, sf, sb, dmask, qdec,
                      ret_gn_w[layer].reshape(1, RET_WIDTH),
                      gate[:batch].reshape(batch, 1, D_MODEL),
                      w_out[layer].astype(BF16), final_norm_w.reshape(1, D_MODEL),
                      batch=batch, seq=seq)
    return y.reshape(batch, seq, d)
```

```python
import functools

import jax
import jax.numpy as jnp
from jax import lax
from jax.experimental import pallas as pl
from jax.experimental.pallas import tpu as pltpu

F32 = jnp.float32
BF16 = jnp.bfloat16

D_MODEL = 1024
GRID_W = 64
ATT_HEADS = 8
ATT_KV_HEADS = 2
ATT_GROUP = ATT_HEADS // ATT_KV_HEADS
ATT_HEAD_DIM = 64
ATT_WIDTH = ATT_HEADS * ATT_HEAD_DIM
ATT_KV_WIDTH = ATT_KV_HEADS * ATT_HEAD_DIM
RET_HEADS = 4
RET_HEAD_DIM = 128
RET_WIDTH = RET_HEADS * RET_HEAD_DIM
ROPE_THETA = 10000.0
NORM_EPS = 1e-6
LANES = 128
TILE = 512
MOD_ROWS = 8
VT_ROWS = ATT_HEAD_DIM
SUBLANES = 8
N_DIR = 2 * RET_HEADS
MXU_TILE = 256
UNIT_COLS = MXU_TILE
UNIT_KEYS = 256
IN_TILES = 1
EPI_TILES = 2
ATT_Q_TILES = 2
PV_LAG = 4

O_QA = 0
O_KA = O_QA + ATT_WIDTH
O_VA = O_KA + ATT_KV_WIDTH
O_GA = O_VA + ATT_KV_WIDTH
O_QR = O_GA + ATT_WIDTH
O_KR = O_QR + RET_WIDTH
O_VR = O_KR + RET_WIDTH
O_GR = O_VR + RET_WIDTH
IN_WIDTH = O_GR + RET_WIDTH

VMEM_LIMIT = 48 * 1024 * 1024

LOG2_E = 1.4426950408889634
Q_SCALE_LOG2 = ATT_HEAD_DIM ** -0.5 * LOG2_E
SCORE_LOG2_LIMIT = 64.0
BF16_ROUND_UP = 1.0 + 2.0 ** -8


def _silu(x):
    return x * (1.0 / (1.0 + jnp.exp(-x)))


def _silu_tanh(x):
    hx = 0.5 * x
    return hx + hx * jnp.tanh(hx)


def _mod_kernel(cc_ref, w_ref, b_ref, dl_ref,
                mod_ref, dmask_ref, qdec_ref, kdec_ref, kdec_c_ref, cdec_ref):
    sc = _silu(cc_ref[...])
    mod_ref[0] = jnp.dot(sc, w_ref[...], preferred_element_type=F32,
                         precision=lax.Precision.HIGHEST) + b_ref[0]

    @pl.when(pl.program_id(0) == 0)
    def _():
        x = dl_ref[...]
        lg = jnp.minimum(x, 0.0) - jnp.log1p(jnp.exp(-jnp.abs(x)))
        n = float(TILE)
        ii = lax.broadcasted_iota(jnp.int32, (TILE, TILE), 0).astype(F32)
        jj = lax.broadcasted_iota(jnp.int32, (TILE, TILE), 1).astype(F32)
        d = ii - jj
        i_col = lax.broadcasted_iota(jnp.int32, (TILE, LANES), 0).astype(F32)
        for h in range(RET_HEADS):
            lf = lg[h:h + 1, :]
            lb = lg[RET_HEADS + h:RET_HEADS + h + 1, :]
            lf_t = jnp.concatenate([lf] * (TILE // LANES), axis=1)
            lb_t = jnp.concatenate([lb] * (TILE // LANES), axis=1)
            fwd = jnp.exp(lf_t * jnp.maximum(d, 0.0))
            bwd = jnp.exp(lb_t * jnp.maximum(-d, 0.0))
            dmask_ref[h] = jnp.where(d > 0, fwd, jnp.where(d < 0, bwd, 2.0))
            qdec_ref[h] = jnp.exp(lf * (i_col + 1.0))
            qdec_ref[RET_HEADS + h] = jnp.exp(lb * (n - i_col))
        for ref in (kdec_ref, kdec_c_ref):
            length = ref.shape[1]
            tok = lax.broadcasted_iota(jnp.int32, (N_DIR, length), 1).astype(F32)
            row = lax.broadcasted_iota(jnp.int32, (N_DIR, length), 0)
            lg_t = jnp.concatenate([lg] * (length // LANES), axis=1)
            ref[...] = jnp.where(row < RET_HEADS, jnp.exp(lg_t * (float(length) - 1.0 - tok)),
                                 jnp.exp(lg_t * tok))
        cdec_ref[...] = jnp.exp(lg * n)


def _run_mod(cc, w_mod, b_mod, dl, *, ctx_len):
    n3 = 3
    const2 = lambda j: (0, 0)
    const3 = lambda j: (0, 0, 0)
    return pl.pallas_call(
        _mod_kernel,
        grid=(n3,),
        in_specs=[
            pl.BlockSpec((MOD_ROWS, D_MODEL), const2),
            pl.BlockSpec((D_MODEL, D_MODEL), lambda j: (0, j)),
            pl.BlockSpec((1, 1, D_MODEL), lambda j: (j, 0, 0)),
            pl.BlockSpec((N_DIR, LANES), const2),
        ],
        out_specs=[
            pl.BlockSpec((1, MOD_ROWS, D_MODEL), lambda j: (j, 0, 0)),
            pl.BlockSpec((RET_HEADS, TILE, TILE), const3),
            pl.BlockSpec((N_DIR, TILE, LANES), const3),
            pl.BlockSpec((N_DIR, TILE), const2),
            pl.BlockSpec((N_DIR, ctx_len), const2),
            pl.BlockSpec((N_DIR, LANES), const2),
        ],
        out_shape=[
            jax.ShapeDtypeStruct((n3, MOD_ROWS, D_MODEL), F32),
            jax.ShapeDtypeStruct((RET_HEADS, TILE, TILE), F32),
            jax.ShapeDtypeStruct((N_DIR, TILE, LANES), F32),
            jax.ShapeDtypeStruct((N_DIR, TILE), F32),
            jax.ShapeDtypeStruct((N_DIR, ctx_len), F32),
            jax.ShapeDtypeStruct((N_DIR, LANES), F32),
        ],
        compiler_params=pltpu.CompilerParams(dimension_semantics=("arbitrary",),
                                             vmem_limit_bytes=VMEM_LIMIT),
        name="mod",
    )(cc, w_mod, b_mod, dl)


def _inproj_kernel(x_ref, sc_ref, sh_ref, nw_ref, w_ref, qnw_ref, knw_ref, rt_ref, ct_ref, kdec_ref,
                   qT_o, kz_o, vT_o, ga_o, qr_o, krT_o, vr_o, gr_o, u_o):
    tm = ct_ref.shape[1]
    for s in range(x_ref.shape[0] // tm):
        _inproj_tile(s, tm, x_ref, sc_ref, sh_ref, nw_ref, w_ref, qnw_ref, knw_ref, rt_ref, ct_ref,
                     kdec_ref, qT_o, kz_o, vT_o, ga_o, qr_o, krT_o, vr_o, gr_o, u_o)


def _inproj_tile(s, tm, x_ref, sc_ref, sh_ref, nw_ref, w_ref, qnw_ref, knw_ref, rt_ref, ct_ref,
                 kdec_ref, qT_o, kz_o, vT_o, ga_o, qr_o, krT_o, vr_o, gr_o, u_o):
    rs = slice(s * tm, (s + 1) * tm)
    x = x_ref[rs, :]
    ms = jnp.mean(x * x, axis=-1, keepdims=True)
    a = nw_ref[...] * (1.0 + sc_ref[0])
    hb = ((x * lax.rsqrt(ms + NORM_EPS)) * a + sh_ref[0]).astype(BF16)

    def proj(lo, width):
        return jnp.dot(hb, w_ref[:, lo:lo + width], preferred_element_type=F32)

    grid_rows = tm // GRID_W

    def rope_table(k):
        rt = rt_ref[k, s * grid_rows:(s + 1) * grid_rows, :]
        rows = [jnp.broadcast_to(rt[i:i + 1, :], (GRID_W, LANES)) for i in range(grid_rows)]
        return jnp.concatenate(rows, axis=0) + ct_ref[k]

    ca, sa, cr, sr = (rope_table(k) for k in range(4))
    lane = lax.broadcasted_iota(jnp.int32, (tm, LANES), 1)
    head_lo = lane < ATT_HEAD_DIM
    half_lo = (lane & (ATT_HEAD_DIM // 2)) == 0
    inv_hd = 1.0 / ATT_HEAD_DIM

    def att_norm_rope(v, w):
        sq = v * v
        s_lo = jnp.sum(jnp.where(head_lo, sq, 0.0), axis=-1, keepdims=True)
        s_hi = jnp.sum(jnp.where(head_lo, 0.0, sq), axis=-1, keepdims=True)
        r = jnp.where(head_lo, lax.rsqrt(s_lo * inv_hd + NORM_EPS),
                      lax.rsqrt(s_hi * inv_hd + NORM_EPS))
        vn = v * r * w
        rot = jnp.where(half_lo, pltpu.roll(vn, LANES - ATT_HEAD_DIM // 2, 1),
                        pltpu.roll(vn, ATT_HEAD_DIM // 2, 1))
        return vn * ca + rot * sa

    def ret_rope(v):
        return v * cr + pltpu.roll(v, RET_HEAD_DIM // 2, 1) * sr

    qnw = qnw_ref[...]
    qa = proj(O_QA, ATT_WIDTH)
    for c in range(ATT_WIDTH // LANES):
        cs = slice(c * LANES, (c + 1) * LANES)
        qT = (att_norm_rope(qa[:, cs], qnw) * Q_SCALE_LOG2).T
        qT_o[0, s, cs, :] = qT.astype(BF16)

    ka = att_norm_rope(proj(O_KA, ATT_KV_WIDTH), knw_ref[...])
    kz_o[0, rs, :] = jnp.where(head_lo, ka, 0.0).astype(BF16)
    kz_o[1, rs, :] = jnp.where(head_lo, 0.0, ka).astype(BF16)

    va = proj(O_VA, ATT_KV_WIDTH)
    for g in range(ATT_KV_HEADS):
        vg = va if g == 0 else pltpu.roll(va, ATT_HEAD_DIM, 1)
        vT_o[0, s, g] = jnp.where(head_lo, vg, 0.0).T[0:ATT_HEAD_DIM, :].astype(BF16)

    ga_o[rs, :] = _silu_tanh(proj(O_GA, ATT_WIDTH)).astype(BF16)

    vr = proj(O_VR, RET_WIDTH).astype(BF16)
    vr_o[rs, :] = vr
    qr = proj(O_QR, RET_WIDTH)
    kr = proj(O_KR, RET_WIDTH)
    kdec = kdec_ref[...]
    krT = []
    for h in range(RET_HEADS):
        hs = slice(h * RET_HEAD_DIM, (h + 1) * RET_HEAD_DIM)
        qr_o[rs, hs] = ret_rope(qr[:, hs]).astype(BF16)
        krT.append((ret_rope(kr[:, hs]) * (RET_HEAD_DIM ** -0.5)).T)
        krT_o[0, s, hs, :] = krT[h].astype(BF16)

    gr_o[rs, :] = _silu_tanh(proj(O_GR, RET_WIDTH)).astype(BF16)

    for h in range(RET_HEADS):
        hs = slice(h * RET_HEAD_DIM, (h + 1) * RET_HEAD_DIM)
        for row in (h, RET_HEADS + h):
            kd = (krT[h] * kdec[row:row + 1, :]).astype(BF16)
            u_o[0, s, row] = jnp.dot(kd, vr[:, hs], preferred_element_type=F32)


def _run_inproj(x2, scale, shift, norm_w, w_in, qnw, knw, row_tab, col_tab, kdec,
                *, batch, seq, tm, tiles_per_step):
    nt = seq // tm
    ts = tiles_per_step
    ns = nt // ts
    rows = batch * seq
    row_map = lambda t, b: (b * ns + t, 0)
    const2 = lambda t, b: (0, 0)
    vec_spec = pl.BlockSpec((1, 1, D_MODEL), lambda t, b: (b, 0, 0))
    t_map = lambda t, b: (b, t, 0, 0)
    once = pl.Buffered(1)

    def row_out(width):
        return pl.BlockSpec((ts * tm, width), row_map)

    return pl.pallas_call(
        _inproj_kernel,
        grid=(ns, batch),
        in_specs=[
            pl.BlockSpec((ts * tm, D_MODEL), row_map),
            vec_spec, vec_spec,
            pl.BlockSpec((1, D_MODEL), const2),
            pl.BlockSpec((D_MODEL, IN_WIDTH), const2, pipeline_mode=once),
            pl.BlockSpec((1, LANES), const2),
            pl.BlockSpec((1, LANES), const2),
            pl.BlockSpec((4, ts * tm // GRID_W, LANES), lambda t, b: (0, t, 0)),
            pl.BlockSpec((4, tm, LANES), lambda t, b: (0, 0, 0), pipeline_mode=once),
            pl.BlockSpec((N_DIR, tm), const2),
        ],
        out_specs=[
            pl.BlockSpec((1, ts, ATT_WIDTH, tm), t_map),
            pl.BlockSpec((ATT_KV_HEADS, ts * tm, LANES), lambda t, b: (0, b * ns + t, 0)),
            pl.BlockSpec((1, ts, ATT_KV_HEADS, VT_ROWS, tm), lambda t, b: (b, t, 0, 0, 0)),
            row_out(ATT_WIDTH),
            row_out(RET_WIDTH),
            pl.BlockSpec((1, ts, RET_WIDTH, tm), t_map),
            row_out(RET_WIDTH),
            row_out(RET_WIDTH),
            pl.BlockSpec((1, ts, N_DIR, RET_HEAD_DIM, RET_HEAD_DIM), lambda t, b: (b, t, 0, 0, 0)),
        ],
        out_shape=[
            jax.ShapeDtypeStruct((batch, nt, ATT_WIDTH, tm), BF16),
            jax.ShapeDtypeStruct((ATT_KV_HEADS, rows, LANES), BF16),
            jax.ShapeDtypeStruct((batch, nt, ATT_KV_HEADS, VT_ROWS, tm), BF16),
            jax.ShapeDtypeStruct((rows, ATT_WIDTH), BF16),
            jax.ShapeDtypeStruct((rows, RET_WIDTH), BF16),
            jax.ShapeDtypeStruct((batch, nt, RET_WIDTH, tm), BF16),
            jax.ShapeDtypeStruct((rows, RET_WIDTH), BF16),
            jax.ShapeDtypeStruct((rows, RET_WIDTH), BF16),
            jax.ShapeDtypeStruct((batch, nt, N_DIR, RET_HEAD_DIM, RET_HEAD_DIM), F32),
        ],
        compiler_params=pltpu.CompilerParams(dimension_semantics=("arbitrary", "arbitrary"),
                                             vmem_limit_bytes=VMEM_LIMIT),
        name="inproj",
    )(x2, scale, shift, norm_w, w_in, qnw, knw, row_tab, col_tab, kdec)


def _scan_step(first, uc_ref, uf_ref, ub_ref, cdec_ref, sf_o, sb_o, st):
    nh = RET_HEADS
    n = uf_ref.shape[1]
    cdec = cdec_ref[...]

    @pl.when(first)
    def _():
        st[...] = uc_ref[0, 0]

    for t in range(n):
        tb = n - 1 - t
        for h in range(nh):
            s = st[h]
            sf_o[0, 0, t, h] = s.astype(BF16)
            st[h] = cdec[h:h + 1, :] * s + uf_ref[0, t, h]
            s = st[nh + h]
            sb_o[0, 0, tb, h] = s.astype(BF16)
            st[nh + h] = cdec[nh + h:nh + h + 1, :] * s + ub_ref[0, tb, h]


def _attn_kernel(qT_ref, kc_ref, vTc_ref, k_ref, vT_ref, ga_ref, uc_ref, uf_ref, ub_ref, cdec_ref,
                 o_ref, sf_o, sb_o, st, *, tq, tk, n_kv, bounded):
    _scan_step(pl.program_id(2) == 0, uc_ref, uf_ref, ub_ref, cdec_ref, sf_o, sb_o, st)
    hd = ATT_HEAD_DIM
    q_tiles = qT_ref.shape[1]
    wq = jnp.concatenate(
        [jnp.concatenate([qT_ref[0, t, h * hd:(h + 1) * hd, :]] * ATT_KV_HEADS, axis=0)
         for t in range(q_tiles) for h in range(ATT_GROUP)],
        axis=1)
    cols = q_tiles * ATT_GROUP * tq

    if bounded:
        uc = UNIT_COLS
        n_ct = cols // uc
        uk = UNIT_KEYS
        ctx_len = kc_ref.shape[1]
        kv = [(kc_ref[0, s:s + uk, :], vTc_ref[0, 0, 0, :, s:s + uk]) for s in range(0, ctx_len, uk)]
        kv += [(k_ref[0, j * tk + s:j * tk + s + uk, :], vT_ref[0, j, 0, :, s:s + uk])
               for j in range(n_kv) for s in range(0, tk, uk)]
        units = [(b, c) for b in range(len(kv)) for c in range(n_ct)]
        accs = [jnp.zeros((hd, uc), F32) for _ in range(n_ct)]
        dens = [jnp.zeros((SUBLANES, uc), F32) for _ in range(n_ct)]
        pending = []

        def value_matmul(b, c, sT):
            p = jnp.exp2(sT)
            dens[c] = dens[c] + p.reshape(p.shape[0] // SUBLANES, SUBLANES, uc).sum(axis=0)
            accs[c] = accs[c] + jnp.dot(kv[b][1], p.astype(BF16), preferred_element_type=F32)

        for b, c in units:
            sT = jnp.dot(kv[b][0], wq[:, c * uc:(c + 1) * uc],
                         preferred_element_type=F32)
            pending.append((b, c, sT))
            if len(pending) > PV_LAG:
                value_matmul(*pending.pop(0))
        for item in pending:
            value_matmul(*item)
        acc = jnp.concatenate(accs, axis=1)
        den = jnp.sum(jnp.concatenate(dens, axis=1), axis=0, keepdims=True)
    else:
        def block(carry, k, vT):
            m, den, acc = carry
            sT = jnp.dot(k, wq, preferred_element_type=F32)
            m_new = jnp.maximum(m, jnp.max(sT, axis=0, keepdims=True))
            p = jnp.exp2(sT - m_new)
            alpha = jnp.exp2(m - m_new)
            den = alpha * den + jnp.sum(p, axis=0, keepdims=True)
            acc = alpha * acc + jnp.dot(vT, p.astype(BF16), preferred_element_type=F32)
            return m_new, den, acc

        carry = (jnp.full((1, cols), -1e30, F32), jnp.zeros((1, cols), F32),
                 jnp.zeros((hd, cols), F32))
        carry = block(carry, kc_ref[0], vTc_ref[0, 0, 0])

        def body(j, carry):
            off = pl.multiple_of(j * tk, tk)
            return block(carry, k_ref[0, pl.ds(off, tk), :], vT_ref[0, j, 0])

        _, den, acc = lax.fori_loop(0, n_kv, body, carry)
    oT = acc * (1.0 / den)
    oT = jnp.concatenate([oT, jnp.zeros((LANES - hd, cols), F32)], axis=0)
    for t in range(q_tiles):
        base = t * ATT_GROUP * tq
        out = jnp.concatenate([oT[:, base + h * tq:base + (h + 1) * tq].T[:, 0:hd]
                               for h in range(ATT_GROUP)], axis=1)
        rs = slice(t * tq, (t + 1) * tq)
        o_ref[rs, :] = (out * ga_ref[rs, :].astype(F32)).astype(BF16)


def _run_attn(qT, kz_c, vT_c, kz, vT, ga, u_c, u, cdec, *, batch, seq, ctx_len, bounded):
    tq = TILE
    qt = ATT_Q_TILES
    nt = seq // tq
    nq = nt // qt
    n_kv = seq // TILE
    gw = ATT_GROUP * ATT_HEAD_DIM
    hd = RET_HEAD_DIM
    s_blk = (1, qt, RET_HEADS, hd, hd)
    s_out_blk = (1,) + s_blk
    s_shape = jax.ShapeDtypeStruct((ATT_KV_HEADS, batch, nt, RET_HEADS, hd, hd), BF16)
    kern = functools.partial(_attn_kernel, tq=tq, tk=TILE, n_kv=n_kv, bounded=bounded)
    return pl.pallas_call(
        kern,
        grid=(batch, ATT_KV_HEADS, nq),
        in_specs=[
            pl.BlockSpec((1, qt, gw, tq), lambda b, g, i: (b, i, g, 0)),
            pl.BlockSpec((1, ctx_len, LANES), lambda b, g, i: (g, b, 0)),
            pl.BlockSpec((1, 1, 1, VT_ROWS, ctx_len), lambda b, g, i: (b, 0, g, 0, 0)),
            pl.BlockSpec((1, seq, LANES), lambda b, g, i: (g, b, 0)),
            pl.BlockSpec((1, n_kv, 1, VT_ROWS, TILE), lambda b, g, i: (b, 0, g, 0, 0)),
            pl.BlockSpec((qt * tq, gw), lambda b, g, i: (b * nq + i, g)),
            pl.BlockSpec((1, 1, N_DIR, hd, hd), lambda b, g, i: (b, 0, 0, 0, 0)),
            pl.BlockSpec(s_blk, lambda b, g, i: (b, i, 0, 0, 0)),
            pl.BlockSpec(s_blk, lambda b, g, i: (b, nq - 1 - i, 1, 0, 0)),
            pl.BlockSpec((N_DIR, LANES), lambda b, g, i: (0, 0)),
        ],
        out_specs=[
            pl.BlockSpec((qt * tq, gw), lambda b, g, i: (b * nq + i, g)),
            pl.BlockSpec(s_out_blk, lambda b, g, i: (g, b, i, 0, 0, 0)),
            pl.BlockSpec(s_out_blk, lambda b, g, i: (g, b, nq - 1 - i, 0, 0, 0)),
        ],
        out_shape=[jax.ShapeDtypeStruct((batch * seq, ATT_WIDTH), BF16), s_shape, s_shape],
        scratch_shapes=[pltpu.VMEM((N_DIR, hd, hd), F32)],
        compiler_params=pltpu.CompilerParams(
            dimension_semantics=("arbitrary", "arbitrary", "arbitrary"),
            vmem_limit_bytes=VMEM_LIMIT),
        name="attn_bounded" if bounded else "attn_online",
    )(qT, kz_c, vT_c, kz, vT, ga, u_c, u, u, cdec)


def _epilogue_kernel(att_ref, qr_ref, krT_ref, vr_ref, gr_ref, x_ref, sf_ref, sb_ref,
                     dmask_ref, qdec_ref, gnw_ref, gate_ref, wout_ref, fnw_ref, y_ref, ret_scr):
    nh = RET_HEADS
    heads = [slice(h * RET_HEAD_DIM, (h + 1) * RET_HEAD_DIM) for h in range(nh)]
    for s in range(krT_ref.shape[1]):
        rs = slice(s * TILE, (s + 1) * TILE)
        a = [jnp.dot(qr_ref[rs, hs], krT_ref[0, s, hs, :], preferred_element_type=F32) * dmask_ref[h]
             for h, hs in enumerate(heads)]
        proj_att = jnp.dot(att_ref[rs, :], wout_ref[0:ATT_WIDTH, :], preferred_element_type=F32)
        for h, hs in enumerate(heads):
            qf = qr_ref[rs, hs].astype(F32)
            o = (jnp.dot((qf * qdec_ref[h]).astype(BF16), sf_ref[0, 0, s, h],
                         preferred_element_type=F32)
                 + jnp.dot((qf * qdec_ref[nh + h]).astype(BF16), sb_ref[0, 0, s, h],
                           preferred_element_type=F32)
                 + jnp.dot(a[h].astype(BF16), vr_ref[rs, hs], preferred_element_type=F32))
            mu = jnp.mean(o, axis=-1, keepdims=True)
            d = o - mu
            var = jnp.mean(d * d, axis=-1, keepdims=True)
            yh = d * lax.rsqrt(var + NORM_EPS) * gnw_ref[:, hs] * gr_ref[rs, hs].astype(F32)
            ret_scr[rs, hs] = yh.astype(BF16)
        proj = proj_att + jnp.dot(ret_scr[rs, :], wout_ref[ATT_WIDTH:, :],
                                  preferred_element_type=F32)
        z = x_ref[rs, :] + gate_ref[0] * proj
        ms = jnp.mean(z * z, axis=-1, keepdims=True)
        y_ref[rs, :] = z * lax.rsqrt(ms + NORM_EPS) * fnw_ref[...]


def _run_epilogue(att, qr, krT, vr, gr, x2, sf, sb, dmask, qdec, gnw, gate, w_out, fnw,
                  *, batch, seq):
    et = EPI_TILES
    rows = et * TILE
    nt = seq // rows
    row_map = lambda b, t: (b * nt + t, 0)
    const2 = lambda b, t: (0, 0)
    const3 = lambda b, t: (0, 0, 0)
    s_spec = pl.BlockSpec((1, 1, et, RET_HEADS, RET_HEAD_DIM, RET_HEAD_DIM),
                          lambda b, t: (0, b, t, 0, 0, 0))
    return pl.pallas_call(
        _epilogue_kernel,
        grid=(batch, nt),
        in_specs=[
            pl.BlockSpec((rows, ATT_WIDTH), row_map),
            pl.BlockSpec((rows, RET_WIDTH), row_map),
            pl.BlockSpec((1, et, RET_WIDTH, TILE), lambda b, t: (b, t, 0, 0)),
            pl.BlockSpec((rows, RET_WIDTH), row_map),
            pl.BlockSpec((rows, RET_WIDTH), row_map),
            pl.BlockSpec((rows, D_MODEL), row_map),
            s_spec, s_spec,
            pl.BlockSpec((RET_HEADS, TILE, TILE), const3),
            pl.BlockSpec((N_DIR, TILE, LANES), const3),
            pl.BlockSpec((1, RET_WIDTH), const2),
            pl.BlockSpec((1, 1, D_MODEL), lambda b, t: (b, 0, 0)),
            pl.BlockSpec((ATT_WIDTH + RET_WIDTH, D_MODEL), const2),
            pl.BlockSpec((1, D_MODEL), const2),
        ],
        out_specs=pl.BlockSpec((rows, D_MODEL), row_map),
        out_shape=jax.ShapeDtypeStruct((batch * seq, D_MODEL), F32),
        scratch_shapes=[pltpu.VMEM((rows, RET_WIDTH), BF16)],
        compiler_params=pltpu.CompilerParams(dimension_semantics=("arbitrary", "arbitrary"),
                                             vmem_limit_bytes=VMEM_LIMIT),
        name="epilogue",
    )(att, qr, krT, vr, gr, x2, sf, sb, dmask, qdec, gnw, gate, w_out, fnw)


def _rope_tables(seq):
    rows = jnp.arange(seq // GRID_W, dtype=F32)
    cols = jnp.arange(GRID_W, dtype=F32)

    def parts(head_dim):
        n_axis = head_dim // 4
        inv_freq = ROPE_THETA ** (-jnp.arange(n_axis, dtype=F32) / n_axis)
        reps = LANES // head_dim

        def lanes(pos, first):
            ang = pos[:, None] * inv_freq
            z = jnp.zeros_like(ang)
            c = jnp.concatenate([jnp.cos(ang), z] if first else [z, jnp.cos(ang)], axis=-1)
            s = jnp.concatenate([jnp.sin(ang), z] if first else [z, jnp.sin(ang)], axis=-1)
            return (jnp.tile(jnp.concatenate([c, c], axis=-1), (1, reps)),
                    jnp.tile(jnp.concatenate([-s, s], axis=-1), (1, reps)))

        return lanes(rows, True), lanes(cols, False)

    (rca, rsa), (cca, csa) = parts(ATT_HEAD_DIM)
    (rcr, rsr), (ccr, csr) = parts(RET_HEAD_DIM)
    return jnp.stack([rca, rsa, rcr, rsr]), jnp.stack([cca, csa, ccr, csr])


def kernel(x, c, ctx, c_ctx, norm_w, w_mod, b_mod, w_in, q_norm_w, k_norm_w,
           ret_decay_fwd, ret_decay_bwd, ret_gn_w, w_out, final_norm_w):
    batch, seq, d = x.shape
    ctx_len = ctx.shape[1]
    depth = norm_w.shape[0]
    assert d == D_MODEL and depth == 1 and seq % (TILE * ATT_Q_TILES) == 0
    assert ctx_len % LANES == 0 and ctx_len <= TILE
    assert batch + 1 <= MOD_ROWS and w_in.shape[-1] == IN_WIDTH
    layer = 0

    cc = jnp.zeros((MOD_ROWS, D_MODEL), F32).at[:batch].set(c).at[batch].set(c_ctx)
    dl = jnp.broadcast_to(
        jnp.concatenate([ret_decay_fwd[layer], ret_decay_bwd[layer]]).astype(F32)[:, None],
        (N_DIR, LANES))
    mod, dmask, qdec, kdec, kdec_c, cdec = _run_mod(
        cc, w_mod[layer], b_mod[layer].reshape(3, 1, D_MODEL), dl, ctx_len=ctx_len)
    shift, scale, gate = mod[0], mod[1], mod[2]

    w_in_b = w_in[layer].astype(BF16)
    nw = norm_w[layer].reshape(1, D_MODEL)
    qnw = jnp.tile(q_norm_w[layer], LANES // ATT_HEAD_DIM).reshape(1, LANES)
    knw = jnp.tile(k_norm_w[layer], LANES // ATT_HEAD_DIM).reshape(1, LANES)

    ctx_rows = ctx_len // GRID_W
    one = jnp.ones((ctx_rows, LANES), F32)
    zero = jnp.zeros((ctx_rows, LANES), F32)
    row_tab_c = jnp.stack([one, zero, one, zero])
    col_tab_c = jnp.zeros((4, ctx_len, LANES), F32)
    sc_c = jnp.broadcast_to(scale[batch].reshape(1, 1, D_MODEL), (batch, 1, D_MODEL))
    sh_c = jnp.broadcast_to(shift[batch].reshape(1, 1, D_MODEL), (batch, 1, D_MODEL))
    (_, kz_c, vT_c, _, _, _, _, _, u_c) = _run_inproj(
        ctx.reshape(batch * ctx_len, d), sc_c, sh_c, nw, w_in_b, qnw, knw,
        row_tab_c, col_tab_c, kdec_c, batch=batch, seq=ctx_len, tm=ctx_len, tiles_per_step=1)

    x2 = x.reshape(batch * seq, d)
    sc_l = scale[:batch].reshape(batch, 1, D_MODEL)
    sh_l = shift[:batch].reshape(batch, 1, D_MODEL)
    row_tab, col_tab = _rope_tables(seq)
    col_tab = jnp.tile(col_tab, (1, TILE // GRID_W, 1))
    qT, kz, vT, ga, qr, krT, vr, gr, u = _run_inproj(
        x2, sc_l, sh_l, nw, w_in_b, qnw, knw, row_tab, col_tab, kdec,
        batch=batch, seq=seq, tm=TILE, tiles_per_step=IN_TILES)

    score_bound = (ATT_HEAD_DIM * Q_SCALE_LOG2 * BF16_ROUND_UP ** 2
                   * jnp.max(jnp.abs(q_norm_w[layer])) * jnp.max(jnp.abs(k_norm_w[layer])))
    attn = functools.partial(_run_attn, batch=batch, seq=seq, ctx_len=ctx_len)
    att, sf, sb = lax.cond(score_bound <= SCORE_LOG2_LIMIT,
                           functools.partial(attn, bounded=True),
                           functools.partial(attn, bounded=False),
                           qT, kz_c, vT_c, kz, vT, ga, u_c, u, cdec)
    y = _run_epilogue(att, qr, krT, vr, gr, x2, sf, sb, dmask, qdec,
                      ret_gn_w[layer].reshape(1, RET_WIDTH),
                      gate[:batch].reshape(batch, 1, D_MODEL),
                      w_out[layer].astype(BF16), final_norm_w.reshape(1, D_MODEL),
                      batch=batch, seq=seq)
    return y.reshape(batch, seq, d)
```

```python
import functools

import jax
import jax.numpy as jnp
from jax import lax
from jax.experimental import pallas as pl
from jax.experimental.pallas import tpu as pltpu

F32 = jnp.float32
BF16 = jnp.bfloat16

D_MODEL = 1024
GRID_W = 64
ATT_HEADS = 8
ATT_KV_HEADS = 2
ATT_GROUP = ATT_HEADS // ATT_KV_HEADS
ATT_HEAD_DIM = 64
ATT_WIDTH = ATT_HEADS * ATT_HEAD_DIM
ATT_KV_WIDTH = ATT_KV_HEADS * ATT_HEAD_DIM
RET_HEADS = 4
RET_HEAD_DIM = 128
RET_WIDTH = RET_HEADS * RET_HEAD_DIM
ROPE_THETA = 10000.0
NORM_EPS = 1e-6
LANES = 128
TILE = 512
MOD_ROWS = 8
VT_ROWS = ATT_HEAD_DIM
SUBLANES = 8
N_DIR = 2 * RET_HEADS
MXU_TILE = 256
UNIT_COLS = MXU_TILE
UNIT_KEYS = 256
IN_TILES = 1
EPI_TILES = 2
ATT_Q_TILES = 2
PV_LAG = 4

O_QA = 0
O_KA = O_QA + ATT_WIDTH
O_VA = O_KA + ATT_KV_WIDTH
O_GA = O_VA + ATT_KV_WIDTH
O_QR = O_GA + ATT_WIDTH
O_KR = O_QR + RET_WIDTH
O_VR = O_KR + RET_WIDTH
O_GR = O_VR + RET_WIDTH
IN_WIDTH = O_GR + RET_WIDTH

VMEM_LIMIT = 48 * 1024 * 1024

LOG2_E = 1.4426950408889634
Q_SCALE_LOG2 = ATT_HEAD_DIM ** -0.5 * LOG2_E
SCORE_LOG2_LIMIT = 64.0
BF16_ROUND_UP = 1.0 + 2.0 ** -8


def _silu(x):
    return x * (1.0 / (1.0 + jnp.exp(-x)))


def _silu_tanh(x):
    hx = 0.5 * x
    return hx + hx * jnp.tanh(hx)


def _mod_kernel(cc_ref, w_ref, b_ref, dl_ref,
                mod_ref, dmask_ref, qdec_ref, kdec_ref, kdec_c_ref, cdec_ref):
    sc = _silu(cc_ref[...])
    mod_ref[0] = jnp.dot(sc, w_ref[...], preferred_element_type=F32,
                         precision=lax.Precision.HIGHEST) + b_ref[0]

    @pl.when(pl.program_id(0) == 0)
    def _():
        x = dl_ref[...]
        lg = jnp.minimum(x, 0.0) - jnp.log1p(jnp.exp(-jnp.abs(x)))
        n = float(TILE)
        ii = lax.broadcasted_iota(jnp.int32, (TILE, TILE), 0).astype(F32)
        jj = lax.broadcasted_iota(jnp.int32, (TILE, TILE), 1).astype(F32)
        d = ii - jj
        i_col = lax.broadcasted_iota(jnp.int32, (TILE, LANES), 0).astype(F32)
        for h in range(RET_HEADS):
            lf = lg[h:h + 1, :]
            lb = lg[RET_HEADS + h:RET_HEADS + h + 1, :]
            lf_t = jnp.concatenate([lf] * (TILE // LANES), axis=1)
            lb_t = jnp.concatenate([lb] * (TILE // LANES), axis=1)
            fwd = jnp.exp(lf_t * jnp.maximum(d, 0.0))
            bwd = jnp.exp(lb_t * jnp.maximum(-d, 0.0))
            dmask_ref[h] = jnp.where(d > 0, fwd, jnp.where(d < 0, bwd, 2.0))
            qdec_ref[h] = jnp.exp(lf * (i_col + 1.0))
            qdec_ref[RET_HEADS + h] = jnp.exp(lb * (n - i_col))
        for ref in (kdec_ref, kdec_c_ref):
            length = ref.shape[1]
            tok = lax.broadcasted_iota(jnp.int32, (N_DIR, length), 1).astype(F32)
            row = lax.broadcasted_iota(jnp.int32, (N_DIR, length), 0)
            lg_t = jnp.concatenate([lg] * (length // LANES), axis=1)
            ref[...] = jnp.where(row < RET_HEADS, jnp.exp(lg_t * (float(length) - 1.0 - tok)),
                                 jnp.exp(lg_t * tok))
        cdec_ref[...] = jnp.exp(lg * n)


def _run_mod(cc, w_mod, b_mod, dl, *, ctx_len):
    n3 = 3
    const2 = lambda j: (0, 0)
    const3 = lambda j: (0, 0, 0)
    return pl.pallas_call(
        _mod_kernel,
        grid=(n3,),
        in_specs=[
            pl.BlockSpec((MOD_ROWS, D_MODEL), const2),
            pl.BlockSpec((D_MODEL, D_MODEL), lambda j: (0, j)),
            pl.BlockSpec((1, 1, D_MODEL), lambda j: (j, 0, 0)),
            pl.BlockSpec((N_DIR, LANES), const2),
        ],
        out_specs=[
            pl.BlockSpec((1, MOD_ROWS, D_MODEL), lambda j: (j, 0, 0)),
            pl.BlockSpec((RET_HEADS, TILE, TILE), const3),
            pl.BlockSpec((N_DIR, TILE, LANES), const3),
            pl.BlockSpec((N_DIR, TILE), const2),
            pl.BlockSpec((N_DIR, ctx_len), const2),
            pl.BlockSpec((N_DIR, LANES), const2),
        ],
        out_shape=[
            jax.ShapeDtypeStruct((n3, MOD_ROWS, D_MODEL), F32),
            jax.ShapeDtypeStruct((RET_HEADS, TILE, TILE), F32),
            jax.ShapeDtypeStruct((N_DIR, TILE, LANES), F32),
            jax.ShapeDtypeStruct((N_DIR, TILE), F32),
            jax.ShapeDtypeStruct((N_DIR, ctx_len), F32),
            jax.ShapeDtypeStruct((N_DIR, LANES), F32),
        ],
        compiler_params=pltpu.CompilerParams(dimension_semantics=("arbitrary",),
                                             vmem_limit_bytes=VMEM_LIMIT),
        name="mod",
    )(cc, w_mod, b_mod, dl)


def _inproj_kernel(x_ref, sc_ref, sh_ref, nw_ref, w_ref, qnw_ref, knw_ref, rt_ref, ct_ref, kdec_ref,
                   qT_o, kz_o, vT_o, ga_o, qr_o, krT_o, vr_o, gr_o, u_o):
    tm = ct_ref.shape[1]
    for s in range(x_ref.shape[0] // tm):
        _inproj_tile(s, tm, x_ref, sc_ref, sh_ref, nw_ref, w_ref, qnw_ref, knw_ref, rt_ref, ct_ref,
                     kdec_ref, qT_o, kz_o, vT_o, ga_o, qr_o, krT_o, vr_o, gr_o, u_o)


def _inproj_tile(s, tm, x_ref, sc_ref, sh_ref, nw_ref, w_ref, qnw_ref, knw_ref, rt_ref, ct_ref,
                 kdec_ref, qT_o, kz_o, vT_o, ga_o, qr_o, krT_o, vr_o, gr_o, u_o):
    rs = slice(s * tm, (s + 1) * tm)
    x = x_ref[rs, :]
    ms = jnp.mean(x * x, axis=-1, keepdims=True)
    a = nw_ref[...] * (1.0 + sc_ref[0])
    hb = ((x * lax.rsqrt(ms + NORM_EPS)) * a + sh_ref[0]).astype(BF16)

    def proj(lo, width):
        return jnp.dot(hb, w_ref[:, lo:lo + width], preferred_element_type=F32)

    grid_rows = tm // GRID_W

    def rope_table(k):
        rt = rt_ref[k, s * grid_rows:(s + 1) * grid_rows, :]
        rows = [jnp.broadcast_to(rt[i:i + 1, :], (GRID_W, LANES)) for i in range(grid_rows)]
        return jnp.concatenate(rows, axis=0) + ct_ref[k]

    ca, sa, cr, sr = (rope_table(k) for k in range(4))
    lane = lax.broadcasted_iota(jnp.int32, (tm, LANES), 1)
    head_lo = lane < ATT_HEAD_DIM
    half_lo = (lane & (ATT_HEAD_DIM // 2)) == 0
    inv_hd = 1.0 / ATT_HEAD_DIM

    def att_norm_rope(v, w):
        sq = v * v
        s_lo = jnp.sum(jnp.where(head_lo, sq, 0.0), axis=-1, keepdims=True)
        s_hi = jnp.sum(jnp.where(head_lo, 0.0, sq), axis=-1, keepdims=True)
        r = jnp.where(head_lo, lax.rsqrt(s_lo * inv_hd + NORM_EPS),
                      lax.rsqrt(s_hi * inv_hd + NORM_EPS))
        vn = v * r * w
        rot = jnp.where(half_lo, pltpu.roll(vn, LANES - ATT_HEAD_DIM // 2, 1),
                        pltpu.roll(vn, ATT_HEAD_DIM // 2, 1))
        return vn * ca + rot * sa

    def ret_rope(v):
        return v * cr + pltpu.roll(v, RET_HEAD_DIM // 2, 1) * sr

    qnw = qnw_ref[...]
    qa = proj(O_QA, ATT_WIDTH)
    for c in range(ATT_WIDTH // LANES):
        cs = slice(c * LANES, (c + 1) * LANES)
        qT = (att_norm_rope(qa[:, cs], qnw) * Q_SCALE_LOG2).T
        qT_o[0, s, cs, :] = qT.astype(BF16)

    ka = att_norm_rope(proj(O_KA, ATT_KV_WIDTH), knw_ref[...])
    kz_o[0, rs, :] = jnp.where(head_lo, ka, 0.0).astype(BF16)
    kz_o[1, rs, :] = jnp.where(head_lo, 0.0, ka).astype(BF16)

    va = proj(O_VA, ATT_KV_WIDTH)
    for g in range(ATT_KV_HEADS):
        vg = va if g == 0 else pltpu.roll(va, ATT_HEAD_DIM, 1)
        vT_o[0, s, g] = jnp.where(head_lo, vg, 0.0).T[0:ATT_HEAD_DIM, :].astype(BF16)

    ga_o[rs, :] = _silu_tanh(proj(O_GA, ATT_WIDTH)).astype(BF16)

    vr = proj(O_VR, RET_WIDTH).astype(BF16)
    vr_o[rs, :] = vr
    qr = proj(O_QR, RET_WIDTH)
    kr = proj(O_KR, RET_WIDTH)
    kdec = kdec_ref[...]
    krT = []
    for h in range(RET_HEADS):
        hs = slice(h * RET_HEAD_DIM, (h + 1) * RET_HEAD_DIM)
        qr_o[rs, hs] = ret_rope(qr[:, hs]).astype(BF16)
        krT.append((ret_rope(kr[:, hs]) * (RET_HEAD_DIM ** -0.5)).T)
        krT_o[0, s, hs, :] = krT[h].astype(BF16)

    gr_o[rs, :] = _silu_tanh(proj(O_GR, RET_WIDTH)).astype(BF16)

    for h in range(RET_HEADS):
        hs = slice(h * RET_HEAD_DIM, (h + 1) * RET_HEAD_DIM)
        for row in (h, RET_HEADS + h):
            kd = (krT[h] * kdec[row:row + 1, :]).astype(BF16)
            u_o[0, s, row] = jnp.dot(kd, vr[:, hs], preferred_element_type=F32)


def _run_inproj(x2, scale, shift, norm_w, w_in, qnw, knw, row_tab, col_tab, kdec,
                *, batch, seq, tm, tiles_per_step):
    nt = seq // tm
    ts = tiles_per_step
    ns = nt // ts
    rows = batch * seq
    row_map = lambda t, b: (b * ns + t, 0)
    const2 = lambda t, b: (0, 0)
    vec_spec = pl.BlockSpec((1, 1, D_MODEL), lambda t, b: (b, 0, 0))
    t_map = lambda t, b: (b, t, 0, 0)
    once = pl.Buffered(1)

    def row_out(width):
        return pl.BlockSpec((ts * tm, width), row_map)

    return pl.pallas_call(
        _inproj_kernel,
        grid=(ns, batch),
        in_specs=[
            pl.BlockSpec((ts * tm, D_MODEL), row_map),
            vec_spec, vec_spec,
            pl.BlockSpec((1, D_MODEL), const2),
            pl.BlockSpec((D_MODEL, IN_WIDTH), const2, pipeline_mode=once),
            pl.BlockSpec((1, LANES), const2),
            pl.BlockSpec((1, LANES), const2),
            pl.BlockSpec((4, ts * tm // GRID_W, LANES), lambda t, b: (0, t, 0)),
            pl.BlockSpec((4, tm, LANES), lambda t, b: (0, 0, 0), pipeline_mode=once),
            pl.BlockSpec((N_DIR, tm), const2),
        ],
        out_specs=[
            pl.BlockSpec((1, ts, ATT_WIDTH, tm), t_map),
            pl.BlockSpec((ATT_KV_HEADS, ts * tm, LANES), lambda t, b: (0, b * ns + t, 0)),
            pl.BlockSpec((1, ts, ATT_KV_HEADS, VT_ROWS, tm), lambda t, b: (b, t, 0, 0, 0)),
            row_out(ATT_WIDTH),
            row_out(RET_WIDTH),
            pl.BlockSpec((1, ts, RET_WIDTH, tm), t_map),
            row_out(RET_WIDTH),
            row_out(RET_WIDTH),
            pl.BlockSpec((1, ts, N_DIR, RET_HEAD_DIM, RET_HEAD_DIM), lambda t, b: (b, t, 0, 0, 0)),
        ],
        out_shape=[
            jax.ShapeDtypeStruct((batch, nt, ATT_WIDTH, tm), BF16),
            jax.ShapeDtypeStruct((ATT_KV_HEADS, rows, LANES), BF16),
            jax.ShapeDtypeStruct((batch, nt, ATT_KV_HEADS, VT_ROWS, tm), BF16),
            jax.ShapeDtypeStruct((rows, ATT_WIDTH), BF16),
            jax.ShapeDtypeStruct((rows, RET_WIDTH), BF16),
            jax.ShapeDtypeStruct((batch, nt, RET_WIDTH, tm), BF16),
            jax.ShapeDtypeStruct((rows, RET_WIDTH), BF16),
            jax.ShapeDtypeStruct((rows, RET_WIDTH), BF16),
            jax.ShapeDtypeStruct((batch, nt, N_DIR, RET_HEAD_DIM, RET_HEAD_DIM), F32),
        ],
        compiler_params=pltpu.CompilerParams(dimension_semantics=("arbitrary", "arbitrary"),
                                             vmem_limit_bytes=VMEM_LIMIT),
        name="inproj",
    )(x2, scale, shift, norm_w, w_in, qnw, knw, row_tab, col_tab, kdec)


def _scan_step(first, uc_ref, uf_ref, ub_ref, cdec_ref, sf_o, sb_o, st):
    nh = RET_HEADS
    n = uf_ref.shape[1]
    cdec = cdec_ref[...]

    @pl.when(first)
    def _():
        st[...] = uc_ref[0, 0]

    for t in range(n):
        tb = n - 1 - t
        for h in range(nh):
            s = st[h]
            sf_o[0, 0, t, h] = s.astype(BF16)
            st[h] = cdec[h:h + 1, :] * s + uf_ref[0, t, h]
            s = st[nh + h]
            sb_o[0, 0, tb, h] = s.astype(BF16)
            st[nh + h] = cdec[nh + h:nh + h + 1, :] * s + ub_ref[0, tb, h]


def _attn_kernel(qT_ref, kc_ref, vTc_ref, k_ref, vT_ref, ga_ref, uc_ref, uf_ref, ub_ref, cdec_ref,
                 o_ref, sf_o, sb_o, st, *, tq, tk, n_kv, bounded):
    _scan_step(pl.program_id(2) == 0, uc_ref, uf_ref, ub_ref, cdec_ref, sf_o, sb_o, st)
    hd = ATT_HEAD_DIM
    q_tiles = qT_ref.shape[1]
    wq = jnp.concatenate(
        [jnp.concatenate([qT_ref[0, t, h * hd:(h + 1) * hd, :]] * ATT_KV_HEADS, axis=0)
         for t in range(q_tiles) for h in range(ATT_GROUP)],
        axis=1)
    cols = q_tiles * ATT_GROUP * tq

    if bounded:
        uc = UNIT_COLS
        n_ct = cols // uc
        uk = UNIT_KEYS
        ctx_len = kc_ref.shape[1]
        kv = [(kc_ref[0, s:s + uk, :], vTc_ref[0, 0, 0, :, s:s + uk]) for s in range(0, ctx_len, uk)]
        kv += [(k_ref[0, j * tk + s:j * tk + s + uk, :], vT_ref[0, j, 0, :, s:s + uk])
               for j in range(n_kv) for s in range(0, tk, uk)]
        units = [(b, c) for b in range(len(kv)) for c in range(n_ct)]
        accs = [jnp.zeros((hd, uc), F32) for _ in range(n_ct)]
        dens = [jnp.zeros((SUBLANES, uc), F32) for _ in range(n_ct)]
        pending = []

        def value_matmul(b, c, sT):
            p = jnp.exp2(sT)
            dens[c] = dens[c] + p.reshape(p.shape[0] // SUBLANES, SUBLANES, uc).sum(axis=0)
            accs[c] = accs[c] + jnp.dot(kv[b][1], p.astype(BF16), preferred_element_type=F32)

        for b, c in units:
            sT = jnp.dot(kv[b][0], wq[:, c * uc:(c + 1) * uc],
                         preferred_element_type=F32)
            pending.append((b, c, sT))
            if len(pending) > PV_LAG:
                value_matmul(*pending.pop(0))
        for item in pending:
            value_matmul(*item)
        acc = jnp.concatenate(accs, axis=1)
        den = jnp.sum(jnp.concatenate(dens, axis=1), axis=0, keepdims=True)
    else:
        def block(carry, k, vT):
            m, den, acc = carry
            sT = jnp.dot(k, wq, preferred_element_type=F32)
            m_new = jnp.maximum(m, jnp.max(sT, axis=0, keepdims=True))
            p = jnp.exp2(sT - m_new)
            alpha = jnp.exp2(m - m_new)
            den = alpha * den + jnp.sum(p, axis=0, keepdims=True)
            acc = alpha * acc + jnp.dot(vT, p.astype(BF16), preferred_element_type=F32)
            return m_new, den, acc

        carry = (jnp.full((1, cols), -1e30, F32), jnp.zeros((1, cols), F32),
                 jnp.zeros((hd, cols), F32))
        carry = block(carry, kc_ref[0], vTc_ref[0, 0, 0])

        def body(j, carry):
            off = pl.multiple_of(j * tk, tk)
            return block(carry, k_ref[0, pl.ds(off, tk), :], vT_ref[0, j, 0])

        _, den, acc = lax.fori_loop(0, n_kv, body, carry)
    oT = acc * (1.0 / den)
    for t in range(q_tiles):
        base = t * ATT_GROUP * tq
        pairs = [jnp.concatenate([oT[:, base + h * tq:base + (h + 1) * tq] for h in (h0, h0 + 1)],
                                 axis=0).T for h0 in range(0, ATT_GROUP, 2)]
        out = jnp.concatenate(pairs, axis=1)
        rs = slice(t * tq, (t + 1) * tq)
        o_ref[rs, :] = (out * ga_ref[rs, :].astype(F32)).astype(BF16)


def _run_attn(qT, kz_c, vT_c, kz, vT, ga, u_c, u, cdec, *, batch, seq, ctx_len, bounded):
    tq = TILE
    qt = ATT_Q_TILES
    nt = seq // tq
    nq = nt // qt
    n_kv = seq // TILE
    gw = ATT_GROUP * ATT_HEAD_DIM
    hd = RET_HEAD_DIM
    s_blk = (1, qt, RET_HEADS, hd, hd)
    s_out_blk = (1,) + s_blk
    s_shape = jax.ShapeDtypeStruct((ATT_KV_HEADS, batch, nt, RET_HEADS, hd, hd), BF16)
    kern = functools.partial(_attn_kernel, tq=tq, tk=TILE, n_kv=n_kv, bounded=bounded)
    return pl.pallas_call(
        kern,
        grid=(batch, ATT_KV_HEADS, nq),
        in_specs=[
            pl.BlockSpec((1, qt, gw, tq), lambda b, g, i: (b, i, g, 0)),
            pl.BlockSpec((1, ctx_len, LANES), lambda b, g, i: (g, b, 0)),
            pl.BlockSpec((1, 1, 1, VT_ROWS, ctx_len), lambda b, g, i: (b, 0, g, 0, 0)),
            pl.BlockSpec((1, seq, LANES), lambda b, g, i: (g, b, 0)),
            pl.BlockSpec((1, n_kv, 1, VT_ROWS, TILE), lambda b, g, i: (b, 0, g, 0, 0)),
            pl.BlockSpec((qt * tq, gw), lambda b, g, i: (b * nq + i, g)),
            pl.BlockSpec((1, 1, N_DIR, hd, hd), lambda b, g, i: (b, 0, 0, 0, 0)),
            pl.BlockSpec(s_blk, lambda b, g, i: (b, i, 0, 0, 0)),
            pl.BlockSpec(s_blk, lambda b, g, i: (b, nq - 1 - i, 1, 0, 0)),
            pl.BlockSpec((N_DIR, LANES), lambda b, g, i: (0, 0)),
        ],
        out_specs=[
            pl.BlockSpec((qt * tq, gw), lambda b, g, i: (b * nq + i, g)),
            pl.BlockSpec(s_out_blk, lambda b, g, i: (g, b, i, 0, 0, 0)),
            pl.BlockSpec(s_out_blk, lambda b, g, i: (g, b, nq - 1 - i, 0, 0, 0)),
        ],
        out_shape=[jax.ShapeDtypeStruct((batch * seq, ATT_WIDTH), BF16), s_shape, s_shape],
        scratch_shapes=[pltpu.VMEM((N_DIR, hd, hd), F32)],
        compiler_params=pltpu.CompilerParams(
            dimension_semantics=("arbitrary", "arbitrary", "arbitrary"),
            vmem_limit_bytes=VMEM_LIMIT),
        name="attn_bounded" if bounded else "attn_online",
    )(qT, kz_c, vT_c, kz, vT, ga, u_c, u, u, cdec)


def _epilogue_kernel(att_ref, qr_ref, krT_ref, vr_ref, gr_ref, x_ref, sf_ref, sb_ref,
                     dmask_ref, qdec_ref, gnw_ref, gate_ref, wout_ref, fnw_ref, y_ref, ret_scr):
    nh = RET_HEADS
    heads = [slice(h * RET_HEAD_DIM, (h + 1) * RET_HEAD_DIM) for h in range(nh)]
    for s in range(krT_ref.shape[1]):
        rs = slice(s * TILE, (s + 1) * TILE)
        a = [jnp.dot(qr_ref[rs, hs], krT_ref[0, s, hs, :], preferred_element_type=F32) * dmask_ref[h]
             for h, hs in enumerate(heads)]
        proj_att = jnp.dot(att_ref[rs, :], wout_ref[0:ATT_WIDTH, :], preferred_element_type=F32)
        for h, hs in enumerate(heads):
            qf = qr_ref[rs, hs].astype(F32)
            o = (jnp.dot((qf * qdec_ref[h]).astype(BF16), sf_ref[0, 0, s, h],
                         preferred_element_type=F32)
                 + jnp.dot((qf * qdec_ref[nh + h]).astype(BF16), sb_ref[0, 0, s, h],
                           preferred_element_type=F32)
                 + jnp.dot(a[h].astype(BF16), vr_ref[rs, hs], preferred_element_type=F32))
            mu = jnp.mean(o, axis=-1, keepdims=True)
            d = o - mu
            var = jnp.mean(d * d, axis=-1, keepdims=True)
            yh = d * lax.rsqrt(var + NORM_EPS) * gnw_ref[:, hs] * gr_ref[rs, hs].astype(F32)
            ret_scr[rs, hs] = yh.astype(BF16)
        proj = proj_att + jnp.dot(ret_scr[rs, :], wout_ref[ATT_WIDTH:, :],
                                  preferred_element_type=F32)
        z = x_ref[rs, :] + gate_ref[0] * proj
        ms = jnp.mean(z * z, axis=-1, keepdims=True)
        y_ref[rs, :] = z * lax.rsqrt(ms + NORM_EPS) * fnw_ref[...]


def _run_epilogue(att, qr, krT, vr, gr, x2, sf, sb, dmask, qdec, gnw, gate, w_out, fnw,
                  *, batch, seq):
    et = EPI_TILES
    rows = et * TILE
    nt = seq // rows
    row_map = lambda b, t: (b * nt + t, 0)
    const2 = lambda b, t: (0, 0)
    const3 = lambda b, t: (0, 0, 0)
    s_spec = pl.BlockSpec((1, 1, et, RET_HEADS, RET_HEAD_DIM, RET_HEAD_DIM),
                          lambda b, t: (0, b, t, 0, 0, 0))
    return pl.pallas_call(
        _epilogue_kernel,
        grid=(batch, nt),
        in_specs=[
            pl.BlockSpec((rows, ATT_WIDTH), row_map),
            pl.BlockSpec((rows, RET_WIDTH), row_map),
            pl.BlockSpec((1, et, RET_WIDTH, TILE), lambda b, t: (b, t, 0, 0)),
            pl.BlockSpec((rows, RET_WIDTH), row_map),
            pl.BlockSpec((rows, RET_WIDTH), row_map),
            pl.BlockSpec((rows, D_MODEL), row_map),
            s_spec, s_spec,
            pl.BlockSpec((RET_HEADS, TILE, TILE), const3),
            pl.BlockSpec((N_DIR, TILE, LANES), const3),
            pl.BlockSpec((1, RET_WIDTH), const2),
            pl.BlockSpec((1, 1, D_MODEL), lambda b, t: (b, 0, 0)),
            pl.BlockSpec((ATT_WIDTH + RET_WIDTH, D_MODEL), const2),
            pl.BlockSpec((1, D_MODEL), const2),
        ],
        out_specs=pl.BlockSpec((rows, D_MODEL), row_map),
        out_shape=jax.ShapeDtypeStruct((batch * seq, D_MODEL), F32),
        scratch_shapes=[pltpu.VMEM((rows, RET_WIDTH), BF16)],
        compiler_params=pltpu.CompilerParams(dimension_semantics=("arbitrary", "arbitrary"),
                                             vmem_limit_bytes=VMEM_LIMIT),
        name="epilogue",
    )(att, qr, krT, vr, gr, x2, sf, sb, dmask, qdec, gnw, gate, w_out, fnw)


def _rope_tables(seq):
    rows = jnp.arange(seq // GRID_W, dtype=F32)
    cols = jnp.arange(GRID_W, dtype=F32)

    def parts(head_dim):
        n_axis = head_dim // 4
        inv_freq = ROPE_THETA ** (-jnp.arange(n_axis, dtype=F32) / n_axis)
        reps = LANES // head_dim

        def lanes(pos, first):
            ang = pos[:, None] * inv_freq
            z = jnp.zeros_like(ang)
            c = jnp.concatenate([jnp.cos(ang), z] if first else [z, jnp.cos(ang)], axis=-1)
            s = jnp.concatenate([jnp.sin(ang), z] if first else [z, jnp.sin(ang)], axis=-1)
            return (jnp.tile(jnp.concatenate([c, c], axis=-1), (1, reps)),
                    jnp.tile(jnp.concatenate([-s, s], axis=-1), (1, reps)))

        return lanes(rows, True), lanes(cols, False)

    (rca, rsa), (cca, csa) = parts(ATT_HEAD_DIM)
    (rcr, rsr), (ccr, csr) = parts(RET_HEAD_DIM)
    return jnp.stack([rca, rsa, rcr, rsr]), jnp.stack([cca, csa, ccr, csr])


def kernel(x, c, ctx, c_ctx, norm_w, w_mod, b_mod, w_in, q_norm_w, k_norm_w,
           ret_decay_fwd, ret_decay_bwd, ret_gn_w, w_out, final_norm_w):
    batch, seq, d = x.shape
    ctx_len = ctx.shape[1]
    depth = norm_w.shape[0]
    assert d == D_MODEL and depth == 1 and seq % (TILE * ATT_Q_TILES) == 0
    assert ctx_len % LANES == 0 and ctx_len <= TILE
    assert batch + 1 <= MOD_ROWS and w_in.shape[-1] == IN_WIDTH
    layer = 0

    cc = jnp.zeros((MOD_ROWS, D_MODEL), F32).at[:batch].set(c).at[batch].set(c_ctx)
    dl = jnp.broadcast_to(
        jnp.concatenate([ret_decay_fwd[layer], ret_decay_bwd[layer]]).astype(F32)[:, None],
        (N_DIR, LANES))
    mod, dmask, qdec, kdec, kdec_c, cdec = _run_mod(
        cc, w_mod[layer], b_mod[layer].reshape(3, 1, D_MODEL), dl, ctx_len=ctx_len)
    shift, scale, gate = mod[0], mod[1], mod[2]

    w_in_b = w_in[layer].astype(BF16)
    nw = norm_w[layer].reshape(1, D_MODEL)
    qnw = jnp.tile(q_norm_w[layer], LANES // ATT_HEAD_DIM).reshape(1, LANES)
    knw = jnp.tile(k_norm_w[layer], LANES // ATT_HEAD_DIM).reshape(1, LANES)

    ctx_rows = ctx_len // GRID_W
    one = jnp.ones((ctx_rows, LANES), F32)
    zero = jnp.zeros((ctx_rows, LANES), F32)
    row_tab_c = jnp.stack([one, zero, one, zero])
    col_tab_c = jnp.zeros((4, ctx_len, LANES), F32)
    sc_c = jnp.broadcast_to(scale[batch].reshape(1, 1, D_MODEL), (batch, 1, D_MODEL))
    sh_c = jnp.broadcast_to(shift[batch].reshape(1, 1, D_MODEL), (batch, 1, D_MODEL))
    (_, kz_c, vT_c, _, _, _, _, _, u_c) = _run_inproj(
        ctx.reshape(batch * ctx_len, d), sc_c, sh_c, nw, w_in_b, qnw, knw,
        row_tab_c, col_tab_c, kdec_c, batch=batch, seq=ctx_len, tm=ctx_len, tiles_per_step=1)

    x2 = x.reshape(batch * seq, d)
    sc_l = scale[:batch].reshape(batch, 1, D_MODEL)
    sh_l = shift[:batch].reshape(batch, 1, D_MODEL)
    row_tab, col_tab = _rope_tables(seq)
    col_tab = jnp.tile(col_tab, (1, TILE // GRID_W, 1))
    qT, kz, vT, ga, qr, krT, vr, gr, u = _run_inproj(
        x2, sc_l, sh_l, nw, w_in_b, qnw, knw, row_tab, col_tab, kdec,
        batch=batch, seq=seq, tm=TILE, tiles_per_step=IN_TILES)

    score_bound = (ATT_HEAD_DIM * Q_SCALE_LOG2 * BF16_ROUND_UP ** 2
                   * jnp.max(jnp.abs(q_norm_w[layer])) * jnp.max(jnp.abs(k_norm_w[layer])))
    attn = functools.partial(_run_attn, batch=batch, seq=seq, ctx_len=ctx_len)
    att, sf, sb = lax.cond(score_bound <= SCORE_LOG2_LIMIT,
                           functools.partial(attn, bounded=True),
                           functools.partial(attn, bounded=False),
                           qT, kz_c, vT_c, kz, vT, ga, u_c, u, cdec)
    y = _run_epilogue(att, qr, krT, vr, gr, x2, sf, sb, dmask, qdec,
                      ret_gn_w[layer].reshape(1, RET_WIDTH),
                      gate[:batch].reshape(batch, 1, D_MODEL),
                      w_out[layer].astype(BF16), final_norm_w.reshape(1, D_MODEL),
                      batch=batch, seq=seq)
    return y.reshape(batch, seq, d)
```

```python
import functools

import jax
import jax.numpy as jnp
from jax import lax
from jax.experimental import pallas as pl
from jax.experimental.pallas import tpu as pltpu

F32 = jnp.float32
BF16 = jnp.bfloat16

D_MODEL = 1024
GRID_W = 64
ATT_HEADS = 8
ATT_KV_HEADS = 2
ATT_GROUP = ATT_HEADS // ATT_KV_HEADS
ATT_HEAD_DIM = 64
ATT_WIDTH = ATT_HEADS * ATT_HEAD_DIM
ATT_KV_WIDTH = ATT_KV_HEADS * ATT_HEAD_DIM
RET_HEADS = 4
RET_HEAD_DIM = 128
RET_WIDTH = RET_HEADS * RET_HEAD_DIM
ROPE_THETA = 10000.0
NORM_EPS = 1e-6
LANES = 128
TILE = 512
MOD_ROWS = 8
VT_ROWS = ATT_HEAD_DIM
SUBLANES = 8
N_DIR = 2 * RET_HEADS
MXU_TILE = 256
UNIT_COLS = MXU_TILE
UNIT_KEYS = 256
IN_TILES = 1
EPI_TILES = 2
ATT_Q_TILES = 2
PV_LAG = 4

O_QA = 0
O_KA = O_QA + ATT_WIDTH
O_VA = O_KA + ATT_KV_WIDTH
O_GA = O_VA + ATT_KV_WIDTH
O_QR = O_GA + ATT_WIDTH
O_KR = O_QR + RET_WIDTH
O_VR = O_KR + RET_WIDTH
O_GR = O_VR + RET_WIDTH
IN_WIDTH = O_GR + RET_WIDTH

VMEM_LIMIT = 48 * 1024 * 1024

LOG2_E = 1.4426950408889634
Q_SCALE_LOG2 = ATT_HEAD_DIM ** -0.5 * LOG2_E
SCORE_LOG2_LIMIT = 64.0
BF16_ROUND_UP = 1.0 + 2.0 ** -8


def _silu(x):
    return x * (1.0 / (1.0 + jnp.exp(-x)))


def _silu_tanh(x):
    hx = 0.5 * x
    return hx + hx * jnp.tanh(hx)


def _mod_kernel(cc_ref, w_ref, b_ref, dl_ref,
                mod_ref, dmask_ref, qdec_ref, kdec_ref, kdec_c_ref, cdec_ref):
    sc = _silu(cc_ref[...])
    mod_ref[0] = jnp.dot(sc, w_ref[...], preferred_element_type=F32,
                         precision=lax.Precision.HIGHEST) + b_ref[0]

    @pl.when(pl.program_id(0) == 0)
    def _():
        x = dl_ref[...]
        lg = jnp.minimum(x, 0.0) - jnp.log1p(jnp.exp(-jnp.abs(x)))
        n = float(TILE)
        ii = lax.broadcasted_iota(jnp.int32, (TILE, TILE), 0).astype(F32)
        jj = lax.broadcasted_iota(jnp.int32, (TILE, TILE), 1).astype(F32)
        d = ii - jj
        i_col = lax.broadcasted_iota(jnp.int32, (TILE, LANES), 0).astype(F32)
        for h in range(RET_HEADS):
            lf = lg[h:h + 1, :]
            lb = lg[RET_HEADS + h:RET_HEADS + h + 1, :]
            lf_t = jnp.concatenate([lf] * (TILE // LANES), axis=1)
            lb_t = jnp.concatenate([lb] * (TILE // LANES), axis=1)
            fwd = jnp.exp(lf_t * jnp.maximum(d, 0.0))
            bwd = jnp.exp(lb_t * jnp.maximum(-d, 0.0))
            dmask_ref[h] = jnp.where(d > 0, fwd, jnp.where(d < 0, bwd, 2.0))
            qdec_ref[h] = jnp.exp(lf * (i_col + 1.0))
            qdec_ref[RET_HEADS + h] = jnp.exp(lb * (n - i_col))
        for ref in (kdec_ref, kdec_c_ref):
            length = ref.shape[1]
            tok = lax.broadcasted_iota(jnp.int32, (N_DIR, length), 1).astype(F32)
            row = lax.broadcasted_iota(jnp.int32, (N_DIR, length), 0)
            lg_t = jnp.concatenate([lg] * (length // LANES), axis=1)
            ref[...] = jnp.where(row < RET_HEADS, jnp.exp(lg_t * (float(length) - 1.0 - tok)),
                                 jnp.exp(lg_t * tok))
        cdec_ref[...] = jnp.exp(lg * n)


def _run_mod(cc, w_mod, b_mod, dl, *, ctx_len):
    n3 = 3
    const2 = lambda j: (0, 0)
    const3 = lambda j: (0, 0, 0)
    return pl.pallas_call(
        _mod_kernel,
        grid=(n3,),
        in_specs=[
            pl.BlockSpec((MOD_ROWS, D_MODEL), const2),
            pl.BlockSpec((D_MODEL, D_MODEL), lambda j: (0, j)),
            pl.BlockSpec((1, 1, D_MODEL), lambda j: (j, 0, 0)),
            pl.BlockSpec((N_DIR, LANES), const2),
        ],
        out_specs=[
            pl.BlockSpec((1, MOD_ROWS, D_MODEL), lambda j: (j, 0, 0)),
            pl.BlockSpec((RET_HEADS, TILE, TILE), const3),
            pl.BlockSpec((N_DIR, TILE, LANES), const3),
            pl.BlockSpec((N_DIR, TILE), const2),
            pl.BlockSpec((N_DIR, ctx_len), const2),
            pl.BlockSpec((N_DIR, LANES), const2),
        ],
        out_shape=[
            jax.ShapeDtypeStruct((n3, MOD_ROWS, D_MODEL), F32),
            jax.ShapeDtypeStruct((RET_HEADS, TILE, TILE), F32),
            jax.ShapeDtypeStruct((N_DIR, TILE, LANES), F32),
            jax.ShapeDtypeStruct((N_DIR, TILE), F32),
            jax.ShapeDtypeStruct((N_DIR, ctx_len), F32),
            jax.ShapeDtypeStruct((N_DIR, LANES), F32),
        ],
        compiler_params=pltpu.CompilerParams(dimension_semantics=("arbitrary",),
                                             vmem_limit_bytes=VMEM_LIMIT),
        name="mod",
    )(cc, w_mod, b_mod, dl)


def _inproj_kernel(x_ref, sc_ref, sh_ref, nw_ref, w_ref, qnw_ref, knw_ref, rt_ref, ct_ref, kdec_ref,
                   qT_o, kz_o, vT_o, ga_o, qr_o, krT_o, vr_o, gr_o, u_o):
    tm = ct_ref.shape[1]
    for s in range(x_ref.shape[0] // tm):
        _inproj_tile(s, tm, x_ref, sc_ref, sh_ref, nw_ref, w_ref, qnw_ref, knw_ref, rt_ref, ct_ref,
                     kdec_ref, qT_o, kz_o, vT_o, ga_o, qr_o, krT_o, vr_o, gr_o, u_o)


def _inproj_tile(s, tm, x_ref, sc_ref, sh_ref, nw_ref, w_ref, qnw_ref, knw_ref, rt_ref, ct_ref,
                 kdec_ref, qT_o, kz_o, vT_o, ga_o, qr_o, krT_o, vr_o, gr_o, u_o):
    rs = slice(s * tm, (s + 1) * tm)
    x = x_ref[rs, :]
    ms = jnp.mean(x * x, axis=-1, keepdims=True)
    a = nw_ref[...] * (1.0 + sc_ref[0])
    hb = ((x * lax.rsqrt(ms + NORM_EPS)) * a + sh_ref[0]).astype(BF16)

    def proj(lo, width):
        return jnp.dot(hb, w_ref[:, lo:lo + width], preferred_element_type=F32)

    grid_rows = tm // GRID_W

    def rope_table(k):
        rt = rt_ref[k, s * grid_rows:(s + 1) * grid_rows, :]
        rows = [jnp.broadcast_to(rt[i:i + 1, :], (GRID_W, LANES)) for i in range(grid_rows)]
        return jnp.concatenate(rows, axis=0) + ct_ref[k]

    ca, sa, cr, sr = (rope_table(k) for k in range(4))
    lane = lax.broadcasted_iota(jnp.int32, (tm, LANES), 1)
    head_lo = lane < ATT_HEAD_DIM
    half_lo = (lane & (ATT_HEAD_DIM // 2)) == 0
    inv_hd = 1.0 / ATT_HEAD_DIM

    def att_norm_rope(v, w):
        sq = v * v
        s_lo = jnp.sum(jnp.where(head_lo, sq, 0.0), axis=-1, keepdims=True)
        s_hi = jnp.sum(jnp.where(head_lo, 0.0, sq), axis=-1, keepdims=True)
        r = jnp.where(head_lo, lax.rsqrt(s_lo * inv_hd + NORM_EPS),
                      lax.rsqrt(s_hi * inv_hd + NORM_EPS))
        vn = v * r * w
        rot = jnp.where(half_lo, pltpu.roll(vn, LANES - ATT_HEAD_DIM // 2, 1),
                        pltpu.roll(vn, ATT_HEAD_DIM // 2, 1))
        return vn * ca + rot * sa

    def ret_rope(v):
        return v * cr + pltpu.roll(v, RET_HEAD_DIM // 2, 1) * sr

    qnw = qnw_ref[...]
    qa = proj(O_QA, ATT_WIDTH)
    for c in range(ATT_WIDTH // LANES):
        cs = slice(c * LANES, (c + 1) * LANES)
        qT = (att_norm_rope(qa[:, cs], qnw) * Q_SCALE_LOG2).T
        qT = qT.astype(BF16)
        for j in range(LANES // ATT_HEAD_DIM):
            qh = qT[j * ATT_HEAD_DIM:(j + 1) * ATT_HEAD_DIM, :]
            row0 = (c * (LANES // ATT_HEAD_DIM) + j) * LANES
            qT_o[0, s, row0:row0 + LANES, :] = jnp.concatenate([qh] * ATT_KV_HEADS, axis=0)

    ka = att_norm_rope(proj(O_KA, ATT_KV_WIDTH), knw_ref[...])
    kz_o[0, rs, :] = jnp.where(head_lo, ka, 0.0).astype(BF16)
    kz_o[1, rs, :] = jnp.where(head_lo, 0.0, ka).astype(BF16)

    va = proj(O_VA, ATT_KV_WIDTH)
    for g in range(ATT_KV_HEADS):
        vg = va if g == 0 else pltpu.roll(va, ATT_HEAD_DIM, 1)
        vT_o[0, s, g] = jnp.where(head_lo, vg, 0.0).T[0:ATT_HEAD_DIM, :].astype(BF16)

    ga_o[rs, :] = _silu_tanh(proj(O_GA, ATT_WIDTH)).astype(BF16)

    vr = proj(O_VR, RET_WIDTH).astype(BF16)
    vr_o[rs, :] = vr
    qr = proj(O_QR, RET_WIDTH)
    kr = proj(O_KR, RET_WIDTH)
    kdec = kdec_ref[...]
    krT = []
    for h in range(RET_HEADS):
        hs = slice(h * RET_HEAD_DIM, (h + 1) * RET_HEAD_DIM)
        qr_o[rs, hs] = ret_rope(qr[:, hs]).astype(BF16)
        krT.append((ret_rope(kr[:, hs]) * (RET_HEAD_DIM ** -0.5)).T)
        krT_o[0, s, hs, :] = krT[h].astype(BF16)

    gr_o[rs, :] = _silu_tanh(proj(O_GR, RET_WIDTH)).astype(BF16)

    for h in range(RET_HEADS):
        hs = slice(h * RET_HEAD_DIM, (h + 1) * RET_HEAD_DIM)
        for row in (h, RET_HEADS + h):
            kd = (krT[h] * kdec[row:row + 1, :]).astype(BF16)
            u_o[0, s, row] = jnp.dot(kd, vr[:, hs], preferred_element_type=F32)


def _run_inproj(x2, scale, shift, norm_w, w_in, qnw, knw, row_tab, col_tab, kdec,
                *, batch, seq, tm, tiles_per_step):
    nt = seq // tm
    ts = tiles_per_step
    ns = nt // ts
    rows = batch * seq
    row_map = lambda t, b: (b * ns + t, 0)
    const2 = lambda t, b: (0, 0)
    vec_spec = pl.BlockSpec((1, 1, D_MODEL), lambda t, b: (b, 0, 0))
    t_map = lambda t, b: (b, t, 0, 0)
    once = pl.Buffered(1)

    def row_out(width):
        return pl.BlockSpec((ts * tm, width), row_map)

    return pl.pallas_call(
        _inproj_kernel,
        grid=(ns, batch),
        in_specs=[
            pl.BlockSpec((ts * tm, D_MODEL), row_map),
            vec_spec, vec_spec,
            pl.BlockSpec((1, D_MODEL), const2),
            pl.BlockSpec((D_MODEL, IN_WIDTH), const2, pipeline_mode=once),
            pl.BlockSpec((1, LANES), const2),
            pl.BlockSpec((1, LANES), const2),
            pl.BlockSpec((4, ts * tm // GRID_W, LANES), lambda t, b: (0, t, 0)),
            pl.BlockSpec((4, tm, LANES), lambda t, b: (0, 0, 0), pipeline_mode=once),
            pl.BlockSpec((N_DIR, tm), const2),
        ],
        out_specs=[
            pl.BlockSpec((1, ts, ATT_HEADS * LANES, tm), t_map),
            pl.BlockSpec((ATT_KV_HEADS, ts * tm, LANES), lambda t, b: (0, b * ns + t, 0)),
            pl.BlockSpec((1, ts, ATT_KV_HEADS, VT_ROWS, tm), lambda t, b: (b, t, 0, 0, 0)),
            row_out(ATT_WIDTH),
            row_out(RET_WIDTH),
            pl.BlockSpec((1, ts, RET_WIDTH, tm), t_map),
            row_out(RET_WIDTH),
            row_out(RET_WIDTH),
            pl.BlockSpec((1, ts, N_DIR, RET_HEAD_DIM, RET_HEAD_DIM), lambda t, b: (b, t, 0, 0, 0)),
        ],
        out_shape=[
            jax.ShapeDtypeStruct((batch, nt, ATT_HEADS * LANES, tm), BF16),
            jax.ShapeDtypeStruct((ATT_KV_HEADS, rows, LANES), BF16),
            jax.ShapeDtypeStruct((batch, nt, ATT_KV_HEADS, VT_ROWS, tm), BF16),
            jax.ShapeDtypeStruct((rows, ATT_WIDTH), BF16),
            jax.ShapeDtypeStruct((rows, RET_WIDTH), BF16),
            jax.ShapeDtypeStruct((batch, nt, RET_WIDTH, tm), BF16),
            jax.ShapeDtypeStruct((rows, RET_WIDTH), BF16),
            jax.ShapeDtypeStruct((rows, RET_WIDTH), BF16),
            jax.ShapeDtypeStruct((batch, nt, N_DIR, RET_HEAD_DIM, RET_HEAD_DIM), F32),
        ],
        compiler_params=pltpu.CompilerParams(dimension_semantics=("arbitrary", "arbitrary"),
                                             vmem_limit_bytes=VMEM_LIMIT),
        name="inproj",
    )(x2, scale, shift, norm_w, w_in, qnw, knw, row_tab, col_tab, kdec)


def _scan_step(first, uc_ref, uf_ref, ub_ref, cdec_ref, sf_o, sb_o, st):
    nh = RET_HEADS
    n = uf_ref.shape[1]
    cdec = cdec_ref[...]

    @pl.when(first)
    def _():
        st[...] = uc_ref[0, 0]

    for t in range(n):
        tb = n - 1 - t
        for h in range(nh):
            s = st[h]
            sf_o[0, 0, t, h] = s.astype(BF16)
            st[h] = cdec[h:h + 1, :] * s + uf_ref[0, t, h]
            s = st[nh + h]
            sb_o[0, 0, tb, h] = s.astype(BF16)
            st[nh + h] = cdec[nh + h:nh + h + 1, :] * s + ub_ref[0, tb, h]


def _attn_kernel(qT_ref, kc_ref, vTc_ref, k_ref, vT_ref, ga_ref, uc_ref, uf_ref, ub_ref, cdec_ref,
                 o_ref, sf_o, sb_o, st, *, tq, tk, n_kv, bounded):
    _scan_step(pl.program_id(2) == 0, uc_ref, uf_ref, ub_ref, cdec_ref, sf_o, sb_o, st)
    hd = ATT_HEAD_DIM
    q_tiles = qT_ref.shape[1]
    cols = q_tiles * ATT_GROUP * tq

    def wq_cols(lo, width):
        parts = []
        while width > 0:
            t, rem = divmod(lo, ATT_GROUP * tq)
            h, q0 = divmod(rem, tq)
            n = min(width, tq - q0)
            parts.append(qT_ref[0, t, h * LANES:(h + 1) * LANES, q0:q0 + n])
            lo, width = lo + n, width - n
        return parts[0] if len(parts) == 1 else jnp.concatenate(parts, axis=1)

    if bounded:
        uc = UNIT_COLS
        n_ct = cols // uc
        uk = UNIT_KEYS
        ctx_len = kc_ref.shape[1]
        kv = [(kc_ref[0, s:s + uk, :], vTc_ref[0, 0, 0, :, s:s + uk]) for s in range(0, ctx_len, uk)]
        kv += [(k_ref[0, j * tk + s:j * tk + s + uk, :], vT_ref[0, j, 0, :, s:s + uk])
               for j in range(n_kv) for s in range(0, tk, uk)]
        units = [(b, c) for b in range(len(kv)) for c in range(n_ct)]
        accs = [jnp.zeros((hd, uc), F32) for _ in range(n_ct)]
        dens = [jnp.zeros((SUBLANES, uc), F32) for _ in range(n_ct)]
        pending = []

        def value_matmul(b, c, sT):
            p = jnp.exp2(sT)
            dens[c] = dens[c] + p.reshape(p.shape[0] // SUBLANES, SUBLANES, uc).sum(axis=0)
            accs[c] = accs[c] + jnp.dot(kv[b][1], p.astype(BF16), preferred_element_type=F32)

        for b, c in units:
            sT = jnp.dot(kv[b][0], wq_cols(c * uc, uc),
                         preferred_element_type=F32)
            pending.append((b, c, sT))
            if len(pending) > PV_LAG:
                value_matmul(*pending.pop(0))
        for item in pending:
            value_matmul(*item)
        acc = jnp.concatenate(accs, axis=1)
        den = jnp.sum(jnp.concatenate(dens, axis=1), axis=0, keepdims=True)
    else:
        wq = wq_cols(0, cols)

        def block(carry, k, vT):
            m, den, acc = carry
            sT = jnp.dot(k, wq, preferred_element_type=F32)
            m_new = jnp.maximum(m, jnp.max(sT, axis=0, keepdims=True))
            p = jnp.exp2(sT - m_new)
            alpha = jnp.exp2(m - m_new)
            den = alpha * den + jnp.sum(p, axis=0, keepdims=True)
            acc = alpha * acc + jnp.dot(vT, p.astype(BF16), preferred_element_type=F32)
            return m_new, den, acc

        carry = (jnp.full((1, cols), -1e30, F32), jnp.zeros((1, cols), F32),
                 jnp.zeros((hd, cols), F32))
        carry = block(carry, kc_ref[0], vTc_ref[0, 0, 0])

        def body(j, carry):
            off = pl.multiple_of(j * tk, tk)
            return block(carry, k_ref[0, pl.ds(off, tk), :], vT_ref[0, j, 0])

        _, den, acc = lax.fori_loop(0, n_kv, body, carry)
    oT = acc * (1.0 / den)
    for t in range(q_tiles):
        base = t * ATT_GROUP * tq
        pairs = [jnp.concatenate([oT[:, base + h * tq:base + (h + 1) * tq] for h in (h0, h0 + 1)],
                                 axis=0).T for h0 in range(0, ATT_GROUP, 2)]
        out = jnp.concatenate(pairs, axis=1)
        rs = slice(t * tq, (t + 1) * tq)
        o_ref[rs, :] = (out * ga_ref[rs, :].astype(F32)).astype(BF16)


def _run_attn(qT, kz_c, vT_c, kz, vT, ga, u_c, u, cdec, *, batch, seq, ctx_len, bounded):
    tq = TILE
    qt = ATT_Q_TILES
    nt = seq // tq
    nq = nt // qt
    n_kv = seq // TILE
    gw = ATT_GROUP * ATT_HEAD_DIM
    hd = RET_HEAD_DIM
    s_blk = (1, qt, RET_HEADS, hd, hd)
    s_out_blk = (1,) + s_blk
    s_shape = jax.ShapeDtypeStruct((ATT_KV_HEADS, batch, nt, RET_HEADS, hd, hd), BF16)
    kern = functools.partial(_attn_kernel, tq=tq, tk=TILE, n_kv=n_kv, bounded=bounded)
    return pl.pallas_call(
        kern,
        grid=(batch, ATT_KV_HEADS, nq),
        in_specs=[
            pl.BlockSpec((1, qt, ATT_GROUP * LANES, tq), lambda b, g, i: (b, i, g, 0)),
            pl.BlockSpec((1, ctx_len, LANES), lambda b, g, i: (g, b, 0)),
            pl.BlockSpec((1, 1, 1, VT_ROWS, ctx_len), lambda b, g, i: (b, 0, g, 0, 0)),
            pl.BlockSpec((1, seq, LANES), lambda b, g, i: (g, b, 0)),
            pl.BlockSpec((1, n_kv, 1, VT_ROWS, TILE), lambda b, g, i: (b, 0, g, 0, 0)),
            pl.BlockSpec((qt * tq, gw), lambda b, g, i: (b * nq + i, g)),
            pl.BlockSpec((1, 1, N_DIR, hd, hd), lambda b, g, i: (b, 0, 0, 0, 0)),
            pl.BlockSpec(s_blk, lambda b, g, i: (b, i, 0, 0, 0)),
            pl.BlockSpec(s_blk, lambda b, g, i: (b, nq - 1 - i, 1, 0, 0)),
            pl.BlockSpec((N_DIR, LANES), lambda b, g, i: (0, 0)),
        ],
        out_specs=[
            pl.BlockSpec((qt * tq, gw), lambda b, g, i: (b * nq + i, g)),
            pl.BlockSpec(s_out_blk, lambda b, g, i: (g, b, i, 0, 0, 0)),
            pl.BlockSpec(s_out_blk, lambda b, g, i: (g, b, nq - 1 - i, 0, 0, 0)),
        ],
        out_shape=[jax.ShapeDtypeStruct((batch * seq, ATT_WIDTH), BF16), s_shape, s_shape],
        scratch_shapes=[pltpu.VMEM((N_DIR, hd, hd), F32)],
        compiler_params=pltpu.CompilerParams(
            dimension_semantics=("arbitrary", "arbitrary", "arbitrary"),
            vmem_limit_bytes=VMEM_LIMIT),
        name="attn_bounded" if bounded else "attn_online",
    )(qT, kz_c, vT_c, kz, vT, ga, u_c, u, u, cdec)


def _epilogue_kernel(att_ref, qr_ref, krT_ref, vr_ref, gr_ref, x_ref, sf_ref, sb_ref,
                     dmask_ref, qdec_ref, gnw_ref, gate_ref, wout_ref, fnw_ref, y_ref, ret_scr):
    nh = RET_HEADS
    heads = [slice(h * RET_HEAD_DIM, (h + 1) * RET_HEAD_DIM) for h in range(nh)]
    for s in range(krT_ref.shape[1]):
        rs = slice(s * TILE, (s + 1) * TILE)
        a = [jnp.dot(qr_ref[rs, hs], krT_ref[0, s, hs, :], preferred_element_type=F32) * dmask_ref[h]
             for h, hs in enumerate(heads)]
        proj_att = jnp.dot(att_ref[rs, :], wout_ref[0:ATT_WIDTH, :], preferred_element_type=F32)
        for h, hs in enumerate(heads):
            qf = qr_ref[rs, hs].astype(F32)
            o = (jnp.dot((qf * qdec_ref[h]).astype(BF16), sf_ref[0, 0, s, h],
                         preferred_element_type=F32)
                 + jnp.dot((qf * qdec_ref[nh + h]).astype(BF16), sb_ref[0, 0, s, h],
                           preferred_element_type=F32)
                 + jnp.dot(a[h].astype(BF16), vr_ref[rs, hs], preferred_element_type=F32))
            mu = jnp.mean(o, axis=-1, keepdims=True)
            d = o - mu
            var = jnp.mean(d * d, axis=-1, keepdims=True)
            yh = d * lax.rsqrt(var + NORM_EPS) * gnw_ref[:, hs] * gr_ref[rs, hs].astype(F32)
            ret_scr[rs, hs] = yh.astype(BF16)
        proj = proj_att + jnp.dot(ret_scr[rs, :], wout_ref[ATT_WIDTH:, :],
                                  preferred_element_type=F32)
        z = x_ref[rs, :] + gate_ref[0] * proj
        ms = jnp.mean(z * z, axis=-1, keepdims=True)
        y_ref[rs, :] = z * lax.rsqrt(ms + NORM_EPS) * fnw_ref[...]


def _run_epilogue(att, qr, krT, vr, gr, x2, sf, sb, dmask, qdec, gnw, gate, w_out, fnw,
                  *, batch, seq):
    et = EPI_TILES
    rows = et * TILE
    nt = seq // rows
    row_map = lambda b, t: (b * nt + t, 0)
    const2 = lambda b, t: (0, 0)
    const3 = lambda b, t: (0, 0, 0)
    s_spec = pl.BlockSpec((1, 1, et, RET_HEADS, RET_HEAD_DIM, RET_HEAD_DIM),
                          lambda b, t: (0, b, t, 0, 0, 0))
    return pl.pallas_call(
        _epilogue_kernel,
        grid=(batch, nt),
        in_specs=[
            pl.BlockSpec((rows, ATT_WIDTH), row_map),
            pl.BlockSpec((rows, RET_WIDTH), row_map),
            pl.BlockSpec((1, et, RET_WIDTH, TILE), lambda b, t: (b, t, 0, 0)),
            pl.BlockSpec((rows, RET_WIDTH), row_map),
            pl.BlockSpec((rows, RET_WIDTH), row_map),
            pl.BlockSpec((rows, D_MODEL), row_map),
            s_spec, s_spec,
            pl.BlockSpec((RET_HEADS, TILE, TILE), const3),
            pl.BlockSpec((N_DIR, TILE, LANES), const3),
            pl.BlockSpec((1, RET_WIDTH), const2),
            pl.BlockSpec((1, 1, D_MODEL), lambda b, t: (b, 0, 0)),
            pl.BlockSpec((ATT_WIDTH + RET_WIDTH, D_MODEL), const2),
            pl.BlockSpec((1, D_MODEL), const2),
        ],
        out_specs=pl.BlockSpec((rows, D_MODEL), row_map),
        out_shape=jax.ShapeDtypeStruct((batch * seq, D_MODEL), F32),
        scratch_shapes=[pltpu.VMEM((rows, RET_WIDTH), BF16)],
        compiler_params=pltpu.CompilerParams(dimension_semantics=("arbitrary", "arbitrary"),
                                             vmem_limit_bytes=VMEM_LIMIT),
        name="epilogue",
    )(att, qr, krT, vr, gr, x2, sf, sb, dmask, qdec, gnw, gate, w_out, fnw)


def _rope_tables(seq):
    rows = jnp.arange(seq // GRID_W, dtype=F32)
    cols = jnp.arange(GRID_W, dtype=F32)

    def parts(head_dim):
        n_axis = head_dim // 4
        inv_freq = ROPE_THETA ** (-jnp.arange(n_axis, dtype=F32) / n_axis)
        reps = LANES // head_dim

        def lanes(pos, first):
            ang = pos[:, None] * inv_freq
            z = jnp.zeros_like(ang)
            c = jnp.concatenate([jnp.cos(ang), z] if first else [z, jnp.cos(ang)], axis=-1)
            s = jnp.concatenate([jnp.sin(ang), z] if first else [z, jnp.sin(ang)], axis=-1)
            return (jnp.tile(jnp.concatenate([c, c], axis=-1), (1, reps)),
                    jnp.tile(jnp.concatenate([-s, s], axis=-1), (1, reps)))

        return lanes(rows, True), lanes(cols, False)

    (rca, rsa), (cca, csa) = parts(ATT_HEAD_DIM)
    (rcr, rsr), (ccr, csr) = parts(RET_HEAD_DIM)
    return jnp.stack([rca, rsa, rcr, rsr]), jnp.stack([cca, csa, ccr, csr])


def kernel(x, c, ctx, c_ctx, norm_w, w_mod, b_mod, w_in, q_norm_w, k_norm_w,
           ret_decay_fwd, ret_decay_bwd, ret_gn_w, w_out, final_norm_w):
    batch, seq, d = x.shape
    ctx_len = ctx.shape[1]
    depth = norm_w.shape[0]
    assert d == D_MODEL and depth == 1 and seq % (TILE * ATT_Q_TILES) == 0
    assert ctx_len % LANES == 0 and ctx_len <= TILE
    assert batch + 1 <= MOD_ROWS and w_in.shape[-1] == IN_WIDTH
    layer = 0

    cc = jnp.zeros((MOD_ROWS, D_MODEL), F32).at[:batch].set(c).at[batch].set(c_ctx)
    dl = jnp.broadcast_to(
        jnp.concatenate([ret_decay_fwd[layer], ret_decay_bwd[layer]]).astype(F32)[:, None],
        (N_DIR, LANES))
    mod, dmask, qdec, kdec, kdec_c, cdec = _run_mod(
        cc, w_mod[layer], b_mod[layer].reshape(3, 1, D_MODEL), dl, ctx_len=ctx_len)
    shift, scale, gate = mod[0], mod[1], mod[2]

    w_in_b = w_in[layer].astype(BF16)
    nw = norm_w[layer].reshape(1, D_MODEL)
    qnw = jnp.tile(q_norm_w[layer], LANES // ATT_HEAD_DIM).reshape(1, LANES)
    knw = jnp.tile(k_norm_w[layer], LANES // ATT_HEAD_DIM).reshape(1, LANES)

    ctx_rows = ctx_len // GRID_W
    one = jnp.ones((ctx_rows, LANES), F32)
    zero = jnp.zeros((ctx_rows, LANES), F32)
    row_tab_c = jnp.stack([one, zero, one, zero])
    col_tab_c = jnp.zeros((4, ctx_len, LANES), F32)
    sc_c = jnp.broadcast_to(scale[batch].reshape(1, 1, D_MODEL), (batch, 1, D_MODEL))
    sh_c = jnp.broadcast_to(shift[batch].reshape(1, 1, D_MODEL), (batch, 1, D_MODEL))
    (_, kz_c, vT_c, _, _, _, _, _, u_c) = _run_inproj(
        ctx.reshape(batch * ctx_len, d), sc_c, sh_c, nw, w_in_b, qnw, knw,
        row_tab_c, col_tab_c, kdec_c, batch=batch, seq=ctx_len, tm=ctx_len, tiles_per_step=1)

    x2 = x.reshape(batch * seq, d)
    sc_l = scale[:batch].reshape(batch, 1, D_MODEL)
    sh_l = shift[:batch].reshape(batch, 1, D_MODEL)
    row_tab, col_tab = _rope_tables(seq)
    col_tab = jnp.tile(col_tab, (1, TILE // GRID_W, 1))
    qT, kz, vT, ga, qr, krT, vr, gr, u = _run_inproj(
        x2, sc_l, sh_l, nw, w_in_b, qnw, knw, row_tab, col_tab, kdec,
        batch=batch, seq=seq, tm=TILE, tiles_per_step=IN_TILES)

    score_bound = (ATT_HEAD_DIM * Q_SCALE_LOG2 * BF16_ROUND_UP ** 2
                   * jnp.max(jnp.abs(q_norm_w[layer])) * jnp.max(jnp.abs(k_norm_w[layer])))
    attn = functools.partial(_run_attn, batch=batch, seq=seq, ctx_len=ctx_len)
    att, sf, sb = lax.cond(score_bound <= SCORE_LOG2_LIMIT,
                           functools.partial(attn, bounded=True),
                           functools.partial(attn, bounded=False),
                           qT, kz_c, vT_c, kz, vT, ga, u_c, u, cdec)
    y = _run_epilogue(att, qr, krT, vr, gr, x2, sf, sb, dmask, qdec,
                      ret_gn_w[layer].reshape(1, RET_WIDTH),
                      gate[:batch].reshape(batch, 1, D_MODEL),
                      w_out[layer].astype(BF16), final_norm_w.reshape(1, D_MODEL),
                      batch=batch, seq=seq)
    return y.reshape(batch, seq, d)
```
